```python
import math
import jax
import jax.numpy as jnp
from jax import lax
import numpy as np

D_MODEL = 1024
BATCH = 8
SEQ = 2048
DEPTH = 4
DEC_BATCH = 128
DEC_SEQ = 1
PAST_LEN = 8192
PAGE_SIZE = 128

N_Q_HEADS = 8
N_KV_HEADS = 2
HEAD_DIM = 64
GQA_GROUP = N_Q_HEADS // N_KV_HEADS
WINDOW = 128
ATTN_BLOCK = 128
ROT_DIM = HEAD_DIM // 4
ROPE_THETA = 500000.0
ATTN_WIDTH = N_Q_HEADS * HEAD_DIM
KV_WIDTH = N_KV_HEADS * HEAD_DIM
LRU_WIDTH = 512
LRU_BLOCKS = 8
LRU_BLOCK_DIM = LRU_WIDTH // LRU_BLOCKS
LRU_C = 8.0
CONV_K = 4
SSD_HEADS = 8
SSD_HEAD_DIM = 64
SSD_INNER = SSD_HEADS * SSD_HEAD_DIM
SSD_GROUPS = 2
SSD_STATE = 64
SSD_CHUNK = 128
SSD_HPG = SSD_HEADS // SSD_GROUPS
SSD_CONV_DIM = SSD_INNER + 2 * SSD_GROUPS * SSD_STATE
N_BRANCHES = 3
D_FF = 2816
N_EXPERTS = 8
TOP_K = 2
N_DENSE = (DEPTH + 1) // 2
N_MOE = DEPTH // 2
EPS = 1e-6
IN_WIDTHS = (ATTN_WIDTH, KV_WIDTH, KV_WIDTH, LRU_WIDTH, LRU_WIDTH, SSD_INNER, SSD_CONV_DIM, SSD_HEADS, N_BRANCHES * D_MODEL)
IN_COLS = sum(IN_WIDTHS)

kernel_name = 'hybrid_swa_rglru_ssd_moe_decode_step'


def _split_cols(t, widths):
    points = np.cumsum(np.array(widths))[:-1].tolist()
    return jnp.split(t, points, axis=-1)


def rmsnorm(x, g):
    xf = x.astype(jnp.float32)
    y = xf * lax.rsqrt(jnp.mean(xf * xf, axis=-1, keepdims=True) + EPS)
    return (y * g.astype(jnp.float32)).astype(x.dtype)


def rope(x, pos):
    half = ROT_DIM // 2
    inv_freq = ROPE_THETA ** (-jnp.arange(0, ROT_DIM, 2, dtype=jnp.float32) / ROT_DIM)
    ang = pos.astype(jnp.float32)[:, None] * inv_freq[None, :]
    cos = jnp.cos(ang)[:, None, :]
    sin = jnp.sin(ang)[:, None, :]
    x1 = x[..., :half].astype(jnp.float32)
    x2 = x[..., half:ROT_DIM].astype(jnp.float32)
    rot = jnp.concatenate([x1 * cos - x2 * sin, x2 * cos + x1 * sin], axis=-1).astype(x.dtype)
    return jnp.concatenate([rot, x[..., ROT_DIM:]], axis=-1)


def sink_attend(q, k, v, mask, sinks):
    s = jnp.einsum('bnqkgd,bnskd->bnkgqs', q, k).astype(jnp.float32) * (HEAD_DIM ** -0.5)
    s = jnp.where(mask, s, -jnp.inf)
    sink = jnp.broadcast_to(sinks.astype(jnp.float32).reshape(N_KV_HEADS, GQA_GROUP, 1, 1), s.shape[:-1] + (1,))
    p = jax.nn.softmax(jnp.concatenate([s, sink], axis=-1), axis=-1)[..., :-1]
    return jnp.einsum('bnkgqs,bnskd->bnqkgd', p.astype(v.dtype), v)


def banded_window_attention(q, k, v, sinks):
    b, L, _, d = q.shape
    n = L // ATTN_BLOCK
    qb = q.reshape(b, n, ATTN_BLOCK, N_KV_HEADS, GQA_GROUP, d)

    def band(t):
        tb = t.reshape(b, n, ATTN_BLOCK, N_KV_HEADS, d)
        prev = jnp.concatenate([jnp.zeros_like(tb[:, :1]), tb[:, :-1]], axis=1)
        return jnp.concatenate([prev, tb], axis=2)

    blk = jnp.arange(n)[:, None, None]
    qpos = blk * ATTN_BLOCK + jnp.arange(ATTN_BLOCK)[None, :, None]
    kpos = (blk - 1) * ATTN_BLOCK + jnp.arange(2 * ATTN_BLOCK)[None, None, :]
    diff = qpos - kpos
    mask = (diff >= 0) & (diff <= WINDOW) & (kpos >= 0)
    o = sink_attend(qb, band(k), band(v), mask[None, :, None, None], sinks)
    return o.reshape(b, L, N_Q_HEADS * d)


def cached_window_attention(q, k, v, cache_k, cache_v, pos, sinks):
    b, S, _, d = q.shape
    W = cache_k.shape[1]
    kk = jnp.concatenate([cache_k, k], axis=1)
    vv = jnp.concatenate([cache_v, v], axis=1)
    kpos = jnp.concatenate([pos[0] - W + jnp.arange(W), pos])
    diff = pos[:, None] - kpos[None, :]
    mask = (diff >= 0) & (diff <= WINDOW)
    o = sink_attend(q.reshape(b, 1, S, N_KV_HEADS, GQA_GROUP, d), kk[:, None], vv[:, None], mask, sinks)
    return o.reshape(b, S, N_Q_HEADS * d), kk[:, -W:], vv[:, -W:]


def causal_conv(x, buf, w, bias):
    L = x.shape[1]
    xp = jnp.concatenate([buf.astype(x.dtype), x], axis=1)
    y = xp[:, 0:L] * w[0]
    for j in range(1, CONV_K):
        y = y + xp[:, j:j + L] * w[j]
    return y + bias, xp[:, L:]


def block_diag(x, w):
    b, L, _ = x.shape
    xb = x.reshape(b, L, LRU_BLOCKS, LRU_BLOCK_DIM)
    return jnp.einsum('blhi,hij->blhj', xb, w).reshape(b, L, LRU_WIDTH)


def _affine_combine(left, right):
    a_l, b_l = left
    a_r, b_r = right
    return a_l * a_r, a_r * b_l + b_r


def rg_lru(x, w_a, b_a, w_x, b_x, lam, h0):
    xf = x.astype(jnp.float32)
    r = jax.nn.sigmoid(block_diag(xf, w_a) + b_a)
    i = jax.nn.sigmoid(block_diag(xf, w_x) + b_x)
    log_a = -LRU_C * r * jax.nn.softplus(-lam.astype(jnp.float32))
    a = jnp.exp(log_a)
    u = jnp.sqrt(-jnp.expm1(2.0 * log_a)) * (i * xf)
    a_cum, u_cum = lax.associative_scan(_affine_combine, (a, u), axis=1)
    h = a_cum * h0.astype(jnp.float32)[:, None, :] + u_cum
    return h, h[:, -1]


def ssd_scan(x, dt, a, bmat, cmat, h0):
    f32 = jnp.float32
    x, dt, bmat, cmat = [t.astype(f32) for t in (x, dt, bmat, cmat)]
    b, L = x.shape[:2]
    q = min(SSD_CHUNK, L)
    pad = (-L) % q
    if pad:
        padw = lambda t: jnp.pad(t, [(0, 0), (0, pad)] + [(0, 0)] * (t.ndim - 2))
        x, dt, bmat, cmat = padw(x), padw(dt), padw(bmat), padw(cmat)
    nc = (L + pad) // q
    ch = lambda t: t.reshape((b, nc, q) + t.shape[2:])
    x, dt, bmat, cmat = ch(x), ch(dt), ch(bmat), ch(cmat)
    a_cs = jnp.cumsum(dt * a.astype(f32), axis=2)
    xdt = x * dt[..., None]
    seg = a_cs[:, :, :, None, :] - a_cs[:, :, None, :, :]
    causal = jnp.tril(jnp.ones((q, q), dtype=bool))[None, None, :, :, None]
    decay = jnp.exp(jnp.where(causal, seg, -jnp.inf))
    scores = jnp.einsum('bclhn,bcshn->bclsh', cmat, bmat) * decay
    y_diag = jnp.einsum('bclsh,bcshp->bclhp', scores, xdt)
    to_end = jnp.exp(a_cs[:, :, -1:, :] - a_cs)
    states = jnp.einsum('bclhn,bclh,bclhp->bchpn', bmat, to_end, xdt)
    chunk_decay = jnp.exp(a_cs[:, :, -1, :])

    def step(h, inp):
        dec, st = inp
        return dec[:, :, None, None] * h + st, h

    h_last, h_in = lax.scan(step, h0.astype(f32), (jnp.swapaxes(chunk_decay, 0, 1), jnp.swapaxes(states, 0, 1)))
    h_in = jnp.swapaxes(h_in, 0, 1)
    y_off = jnp.einsum('bclhn,bchpn,bclh->bclhp', cmat, h_in, jnp.exp(a_cs))
    y = (y_diag + y_off).reshape(b, nc * q, x.shape[-2], x.shape[-1])[:, :L]
    return y, h_last


def gated_rmsnorm(y, z, g):
    u = y.astype(jnp.float32) * jax.nn.silu(z.astype(jnp.float32))
    ug = u.reshape(u.shape[:-1] + (SSD_GROUPS, SSD_INNER // SSD_GROUPS))
    ug = ug * lax.rsqrt(jnp.mean(ug * ug, axis=-1, keepdims=True) + EPS)
    return ug.reshape(u.shape) * g.astype(jnp.float32)


def token_mixers(h, pos, lp, cache_k, cache_v, lru_h0, lru_buf, ssd_h0, ssd_buf):
    b, L, _ = h.shape
    (q, k, v, lru_x, lru_gate, ssd_z, ssd_xbc, ssd_dt, gate_logits) = _split_cols(h @ lp['w_in'], IN_WIDTHS)
    q = rope(rmsnorm(q.reshape(b, L, N_Q_HEADS, HEAD_DIM), lp['g_q']), pos)
    k = rope(rmsnorm(k.reshape(b, L, N_KV_HEADS, HEAD_DIM), lp['g_k']), pos)
    v = v.reshape(b, L, N_KV_HEADS, HEAD_DIM)
    if cache_k is None:
        o_attn = banded_window_attention(q, k, v, lp['sinks'])
        k_win, v_win = k[:, -WINDOW:], v[:, -WINDOW:]
    else:
        o_attn, k_win, v_win = cached_window_attention(q, k, v, cache_k, cache_v, pos, lp['sinks'])
    xc, lru_buf_new = causal_conv(lru_x, lru_buf, lp['lru_conv_w'], lp['lru_conv_b'])
    hs, lru_h_new = rg_lru(xc, lp['lru_w_a'], lp['lru_b_a'], lp['lru_w_x'], lp['lru_b_x'], lp['lru_lambda'], lru_h0)
    o_lru = hs * jax.nn.gelu(lru_gate.astype(jnp.float32))
    xbc, ssd_buf_new = causal_conv(ssd_xbc, ssd_buf, lp['ssd_conv_w'], lp['ssd_conv_b'])
    xbc = jax.nn.silu(xbc)
    xs, bm, cm = _split_cols(xbc, (SSD_INNER, SSD_GROUPS * SSD_STATE, SSD_GROUPS * SSD_STATE))
    xs = xs.reshape(b, L, SSD_HEADS, SSD_HEAD_DIM)
    bm = jnp.repeat(bm.reshape(b, L, SSD_GROUPS, SSD_STATE), SSD_HPG, axis=2)
    cm = jnp.repeat(cm.reshape(b, L, SSD_GROUPS, SSD_STATE), SSD_HPG, axis=2)
    dt = jax.nn.softplus((ssd_dt + lp['ssd_dt_bias']).astype(jnp.float32))
    a = -jnp.exp(lp['ssd_a_log'].astype(jnp.float32))
    ys, ssd_h_new = ssd_scan(xs, dt, a, bm, cm, ssd_h0)
    ys = ys + lp['ssd_d'][:, None] * xs
    o_ssd = gated_rmsnorm(ys.reshape(b, L, SSD_INNER), ssd_z, lp['g_ssd_norm'])
    g_a, g_l, g_s = jnp.split(jax.nn.sigmoid(gate_logits.astype(jnp.float32)), N_BRANCHES, axis=-1)
    merged = g_a * (o_attn @ lp['w_br_attn']) + g_l * (o_lru @ lp['w_br_lru']) + g_s * (o_ssd @ lp['w_br_ssd'])
    return merged @ lp['w_out'], (k_win, v_win, lru_h_new, lru_buf_new, ssd_h_new, ssd_buf_new)


def swiglu(h, w_gate, w_up, w_down):
    return (jax.nn.silu(h @ w_gate) * (h @ w_up)) @ w_down


def moe_swiglu(h, w_router, b_router, w_gate, w_up, w_down):
    logits = jnp.einsum('bld,de->ble', h.astype(jnp.float32), w_router.astype(jnp.float32)) + b_router.astype(jnp.float32)
    top_val, top_idx = lax.top_k(logits, TOP_K)
    top_w = jax.nn.softmax(top_val, axis=-1)
    combine = jnp.sum(jax.nn.one_hot(top_idx, N_EXPERTS, dtype=jnp.float32) * top_w[..., None], axis=-2)
    out = jnp.zeros(h.shape, jnp.float32)
    for e in range(N_EXPERTS):
        out = out + combine[..., e:e + 1] * swiglu(h, w_gate[e], w_up[e], w_down[e])
    return out


def trunk_layer(x, c, lp, ffn_params, is_moe, pos, cache_k, cache_v, lru_h0, lru_buf, ssd_h0, ssd_buf):
    mod = (jax.nn.silu(c) @ lp['w_mod'] + lp['b_mod'])[:, None, :]
    shift1, scale1, gate1, shift2, scale2, gate2 = jnp.split(mod, 6, axis=-1)
    h = rmsnorm(x, lp['g_norm_mix']) * (1.0 + scale1) + shift1
    mix, new_state = token_mixers(h, pos, lp, cache_k, cache_v, lru_h0, lru_buf, ssd_h0, ssd_buf)
    x = x + gate1 * mix
    h = rmsnorm(x, lp['g_norm_ffn']) * (1.0 + scale2) + shift2
    f = moe_swiglu(h, *ffn_params) if is_moe else swiglu(h, *ffn_params)
    x = x + gate2 * f
    return x, new_state


def setup_inputs(seed: int = 0) -> dict:
    key = jax.random.key(seed)
    ks = iter(jax.random.split(key, 64))
    f32 = jnp.float32

    def nrm(shape, scale):
        return scale * jax.random.normal(next(ks), shape, f32)

    def unif(shape, lo, hi):
        return jax.random.uniform(next(ks), shape, f32, minval=lo, maxval=hi)

    win = min(WINDOW, PAST_LEN)
    u = unif((DEPTH, LRU_WIDTH), 0.9, 0.999)
    dt0 = jnp.exp(unif((DEPTH, SSD_HEADS), math.log(1e-3), math.log(1e-1)))
    return {
        'x_prompt': nrm((BATCH, SEQ, D_MODEL), 1.0),
        'x_sample': nrm((DEC_BATCH, DEC_SEQ, D_MODEL), 1.0),
        'c_prompt': nrm((BATCH, D_MODEL), 1.0),
        'c_sample': nrm((DEC_BATCH, D_MODEL), 1.0),
        'cache_k': nrm((DEPTH, DEC_BATCH, win, N_KV_HEADS, HEAD_DIM), 1.0),
        'cache_v': nrm((DEPTH, DEC_BATCH, win, N_KV_HEADS, HEAD_DIM), 1.0),
        'state_lru_h': nrm((DEPTH, DEC_BATCH, LRU_WIDTH), 0.5),
        'state_lru_conv': nrm((DEPTH, DEC_BATCH, CONV_K - 1, LRU_WIDTH), 1.0),
        'state_ssd_h': nrm((DEPTH, DEC_BATCH, SSD_HEADS, SSD_HEAD_DIM, SSD_STATE), 0.3),
        'state_ssd_conv': nrm((DEPTH, DEC_BATCH, CONV_K - 1, SSD_CONV_DIM), 1.0),
        'w_mod': nrm((DEPTH, D_MODEL, 6 * D_MODEL), 0.5 * D_MODEL ** -0.5),
        'b_mod': nrm((DEPTH, 6 * D_MODEL), 0.02),
        'g_norm_mix': 1.0 + nrm((DEPTH, D_MODEL), 0.02),
        'g_norm_ffn': 1.0 + nrm((DEPTH, D_MODEL), 0.02),
        'w_in': nrm((DEPTH, D_MODEL, IN_COLS), D_MODEL ** -0.5),
        'g_q': 1.0 + nrm((DEPTH, HEAD_DIM), 0.02),
        'g_k': 1.0 + nrm((DEPTH, HEAD_DIM), 0.02),
        'attn_sinks': nrm((DEPTH, N_Q_HEADS), 0.5),
        'lru_conv_w': nrm((DEPTH, CONV_K, LRU_WIDTH), CONV_K ** -0.5),
        'lru_conv_b': nrm((DEPTH, LRU_WIDTH), 0.02),
        'lru_w_a': nrm((DEPTH, LRU_BLOCKS, LRU_BLOCK_DIM, LRU_BLOCK_DIM), LRU_BLOCK_DIM ** -0.5),
        'lru_b_a': nrm((DEPTH, LRU_WIDTH), 0.02),
        'lru_w_x': nrm((DEPTH, LRU_BLOCKS, LRU_BLOCK_DIM, LRU_BLOCK_DIM), LRU_BLOCK_DIM ** -0.5),
        'lru_b_x': nrm((DEPTH, LRU_WIDTH), 0.02),
        'lru_lambda': jnp.log(u) - jnp.log1p(-u),
        'ssd_conv_w': nrm((DEPTH, CONV_K, SSD_CONV_DIM), CONV_K ** -0.5),
        'ssd_conv_b': nrm((DEPTH, SSD_CONV_DIM), 0.02),
        'ssd_dt_bias': dt0 + jnp.log(-jnp.expm1(-dt0)),
        'ssd_a_log': jnp.log(unif((DEPTH, SSD_HEADS), 1.0, 16.0)),
        'ssd_d': 1.0 + nrm((DEPTH, SSD_HEADS), 0.1),
        'g_ssd_norm': 1.0 + nrm((DEPTH, SSD_INNER), 0.02),
        'w_br_attn': nrm((DEPTH, ATTN_WIDTH, D_MODEL), ATTN_WIDTH ** -0.5),
        'w_br_lru': nrm((DEPTH, LRU_WIDTH, D_MODEL), LRU_WIDTH ** -0.5),
        'w_br_ssd': nrm((DEPTH, SSD_INNER, D_MODEL), SSD_INNER ** -0.5),
        'w_out': nrm((DEPTH, D_MODEL, D_MODEL), D_MODEL ** -0.5),
        'ffn_w_gate': nrm((N_DENSE, D_MODEL, D_FF), D_MODEL ** -0.5),
        'ffn_w_up': nrm((N_DENSE, D_MODEL, D_FF), D_MODEL ** -0.5),
        'ffn_w_down': nrm((N_DENSE, D_FF, D_MODEL), D_FF ** -0.5),
        'moe_w_router': nrm((N_MOE, D_MODEL, N_EXPERTS), D_MODEL ** -0.5),
        'moe_b_router': nrm((N_MOE, N_EXPERTS), 0.01),
        'moe_w_gate': nrm((N_MOE, N_EXPERTS, D_MODEL, D_FF), D_MODEL ** -0.5),
        'moe_w_up': nrm((N_MOE, N_EXPERTS, D_MODEL, D_FF), D_MODEL ** -0.5),
        'moe_w_down': nrm((N_MOE, N_EXPERTS, D_FF, D_MODEL), D_FF ** -0.5),
    }


def reference(x_prompt, x_sample, c_prompt, c_sample, cache_k, cache_v, state_lru_h, state_lru_conv, state_ssd_h, state_ssd_conv, w_mod, b_mod, g_norm_mix, g_norm_ffn, w_in, g_q, g_k, attn_sinks, lru_conv_w, lru_conv_b, lru_w_a, lru_b_a, lru_w_x, lru_b_x, lru_lambda, ssd_conv_w, ssd_conv_b, ssd_dt_bias, ssd_a_log, ssd_d, g_ssd_norm, w_br_attn, w_br_lru, w_br_ssd, w_out, ffn_w_gate, ffn_w_up, ffn_w_down, moe_w_router, moe_b_router, moe_w_gate, moe_w_up, moe_w_down):
    bp, lp_len = x_prompt.shape[:2]
    pos_prompt = jnp.arange(lp_len)
    pos_sample = PAST_LEN + jnp.arange(x_sample.shape[1])
    zdt = x_prompt.dtype
    xp, xs = x_prompt, x_sample
    prompt_states = [[] for _ in range(6)]
    sample_states = [[] for _ in range(6)]
    for l in range(DEPTH):
        lp = dict(w_mod=w_mod[l], b_mod=b_mod[l], g_norm_mix=g_norm_mix[l], g_norm_ffn=g_norm_ffn[l], w_in=w_in[l], g_q=g_q[l], g_k=g_k[l], sinks=attn_sinks[l], lru_conv_w=lru_conv_w[l], lru_conv_b=lru_conv_b[l], lru_w_a=lru_w_a[l], lru_b_a=lru_b_a[l], lru_w_x=lru_w_x[l], lru_b_x=lru_b_x[l], lru_lambda=lru_lambda[l], ssd_conv_w=ssd_conv_w[l], ssd_conv_b=ssd_conv_b[l], ssd_dt_bias=ssd_dt_bias[l], ssd_a_log=ssd_a_log[l], ssd_d=ssd_d[l], g_ssd_norm=g_ssd_norm[l], w_br_attn=w_br_attn[l], w_br_lru=w_br_lru[l], w_br_ssd=w_br_ssd[l], w_out=w_out[l])
        j = l // 2
        is_moe = (l % 2 == 1)
        if is_moe:
            ffn = (moe_w_router[j], moe_b_router[j], moe_w_gate[j], moe_w_up[j], moe_w_down[j])
        else:
            ffn = (ffn_w_gate[j], ffn_w_up[j], ffn_w_down[j])
        xp, st_p = trunk_layer(xp, c_prompt, lp, ffn, is_moe, pos_prompt, None, None,
                               jnp.zeros((bp, LRU_WIDTH), zdt), jnp.zeros((bp, CONV_K - 1, LRU_WIDTH), zdt),
                               jnp.zeros((bp, SSD_HEADS, SSD_HEAD_DIM, SSD_STATE), zdt), jnp.zeros((bp, CONV_K - 1, SSD_CONV_DIM), zdt))
        xs, st_s = trunk_layer(xs, c_sample, lp, ffn, is_moe, pos_sample, cache_k[l], cache_v[l],
                               state_lru_h[l], state_lru_conv[l], state_ssd_h[l], state_ssd_conv[l])
        for acc, t in zip(prompt_states, st_p):
            acc.append(t)
        for acc, t in zip(sample_states, st_s):
            acc.append(t)
    k_win_p, v_win_p, lru_h_p, lru_conv_p, ssd_h_p, ssd_conv_p = [jnp.stack(a) for a in prompt_states]
    k_win_s, v_win_s, lru_h_s, lru_conv_s, ssd_h_s, ssd_conv_s = [jnp.stack(a) for a in sample_states]
    return (xp, xs, k_win_p, v_win_p, lru_h_p, lru_conv_p, ssd_h_p, ssd_conv_p, k_win_s, v_win_s, lru_h_s, lru_conv_s, ssd_h_s, ssd_conv_s)
```

```python
import functools
import math

import numpy as np
import jax
import jax.numpy as jnp
from jax import lax
from jax.experimental import pallas as pl
from jax.experimental.pallas import tpu as pltpu

F32 = jnp.float32
BF16 = jnp.bfloat16
I32 = jnp.int32

D_MODEL = 1024
DEPTH = 4
PAST_LEN = 8192
N_Q_HEADS = 8
N_KV_HEADS = 2
HEAD_DIM = 64
WINDOW = 128
ATTN_BLOCK = 128
ROT_DIM = HEAD_DIM // 4
ROPE_THETA = 500000.0
ATTN_WIDTH = N_Q_HEADS * HEAD_DIM
KV_WIDTH = N_KV_HEADS * HEAD_DIM
LRU_WIDTH = 512
LRU_BLOCKS = 8
LRU_BLOCK_DIM = LRU_WIDTH // LRU_BLOCKS
LRU_C = 8.0
CONV_K = 4
SSD_HEADS = 8
SSD_HEAD_DIM = 64
SSD_INNER = SSD_HEADS * SSD_HEAD_DIM
SSD_GROUPS = 2
SSD_STATE = 64
SSD_CHUNK = 128
SSD_CONV_DIM = SSD_INNER + 2 * SSD_GROUPS * SSD_STATE
D_FF = 2816
N_EXPERTS = 8
TOP_K = 2
EPS = 1e-6

LANES = 128
SUBLANES = 8
QKV_W = ATTN_WIDTH + 2 * KV_WIDTH
GATES_W = 3 * D_MODEL
C_QKV = 0
C_LX = C_QKV + QKV_W
C_LG = C_LX + LRU_WIDTH
C_Z = C_LG + LRU_WIDTH
C_XBC = C_Z + SSD_INNER
C_GT = C_XBC + SSD_CONV_DIM
C_DT = C_GT + GATES_W
IN_PACKED = C_DT + LANES
FF_CHUNK = D_FF // 2
VMEM_LIMIT = 56 * 1024 * 1024


def _params(sem):
    return pltpu.CompilerParams(dimension_semantics=sem, vmem_limit_bytes=VMEM_LIMIT)


def _sigmoid(x):
    return 1.0 / (1.0 + jnp.exp(-x))


def _silu(x):
    return x * _sigmoid(x)


def _softplus(x):
    return jnp.maximum(x, 0.0) + jnp.log(1.0 + jnp.exp(-jnp.abs(x)))


def _gelu_tanh(x):
    return 0.5 * x * (1.0 + jnp.tanh(math.sqrt(2.0 / math.pi) * (x + 0.044715 * (x * x * x))))


def _split3(a):
    a1 = a.astype(BF16)
    r1 = a - a1.astype(F32)
    a2 = r1.astype(BF16)
    a3 = (r1 - a2.astype(F32)).astype(BF16)
    return a1, a2, a3


def _dot_exact_rhs(a, m):
    a1, a2, a3 = _split3(a)
    d = lambda t: jnp.dot(t, m, preferred_element_type=F32)
    return d(a1) + d(a2) + d(a3)


def _dot_exact_lhs(m, a):
    a1, a2, a3 = _split3(a)
    d = lambda t: jnp.dot(m, t, preferred_element_type=F32)
    return d(a1) + d(a2) + d(a3)


def _rms(x, g):
    return x * lax.rsqrt(jnp.mean(x * x, axis=-1, keepdims=True) + EPS) * g


def _mod_kernel(c_ref, w_ref, b_ref, o_ref):
    s = _silu(c_ref[...]).astype(BF16)
    o_ref[...] = jnp.dot(s, w_ref[...].astype(BF16), preferred_element_type=F32) + b_ref[...]


def _modulation(c, w_mod, b_mod):
    rows = c.shape[0]
    tn = 1536
    nt = (6 * D_MODEL) // tn
    return pl.pallas_call(
        _mod_kernel,
        grid=(DEPTH, nt),
        in_specs=[
            pl.BlockSpec((rows, D_MODEL), lambda l, j: (0, 0)),
            pl.BlockSpec((None, D_MODEL, tn), lambda l, j: (l, 0, j)),
            pl.BlockSpec((None, 1, tn), lambda l, j: (l, 0, j)),
        ],
        out_specs=pl.BlockSpec((None, rows, tn), lambda l, j: (l, 0, j)),
        out_shape=jax.ShapeDtypeStruct((DEPTH, rows, 6 * D_MODEL), F32),
        compiler_params=_params(("arbitrary", "arbitrary")),
        name="modulation",
    )(c, w_mod, b_mod.reshape(DEPTH, 1, 6 * D_MODEL))


def _mod_spec(per_row, tm, seq_len):
    if per_row:
        return pl.BlockSpec((tm, D_MODEL), lambda i: (i, 0))
    return pl.BlockSpec((None, 1, D_MODEL), lambda i: ((i * tm) // seq_len, 0, 0))


def _in_kernel(x_ref, g_ref, sc_ref, sh_ref, w_ref,
               qkv_ref, lx_ref, lg_ref, z_ref, xbc_ref, gt_ref, dt_ref):
    h = (_rms(x_ref[...], g_ref[...]) * (1.0 + sc_ref[...]) + sh_ref[...]).astype(BF16)

    def mm(lo, hi):
        return jnp.dot(h, w_ref[:, lo:hi], preferred_element_type=F32)

    qkv_ref[...] = mm(C_QKV, C_LX).astype(BF16)
    lx_ref[...] = mm(C_LX, C_LG)
    lg_ref[...] = mm(C_LG, C_Z).astype(BF16)
    z_ref[...] = mm(C_Z, C_XBC).astype(BF16)
    xbc_ref[...] = mm(C_XBC, C_GT)
    gt_ref[...] = mm(C_GT, C_DT).astype(BF16)
    dt_ref[...] = mm(C_DT, IN_PACKED)


def _in_proj(x, g, scale, shift, w_packed, per_row, seq_len, tm):
    m = x.shape[0]
    row = lambda w: pl.BlockSpec((tm, w), lambda i: (i, 0))
    ms = _mod_spec(per_row, tm, seq_len)
    return pl.pallas_call(
        _in_kernel,
        grid=(m // tm,),
        in_specs=[row(D_MODEL), pl.BlockSpec((1, D_MODEL), lambda i: (0, 0)), ms, ms,
                  pl.BlockSpec((D_MODEL, IN_PACKED), lambda i: (0, 0))],
        out_specs=[row(QKV_W), row(LRU_WIDTH), row(LRU_WIDTH), row(SSD_INNER),
                   row(SSD_CONV_DIM), row(GATES_W), row(LANES)],
        out_shape=[jax.ShapeDtypeStruct((m, QKV_W), BF16),
                   jax.ShapeDtypeStruct((m, LRU_WIDTH), F32),
                   jax.ShapeDtypeStruct((m, LRU_WIDTH), BF16),
                   jax.ShapeDtypeStruct((m, SSD_INNER), BF16),
                   jax.ShapeDtypeStruct((m, SSD_CONV_DIM), F32),
                   jax.ShapeDtypeStruct((m, GATES_W), BF16),
                   jax.ShapeDtypeStruct((m, LANES), F32)],
        compiler_params=_params(("arbitrary",)),
        name="in_proj",
    )(x, g, scale, shift, w_packed)


def _head_norm_rope(t, g2, e2, cos_t, sin_lo, sin_hi):
    ms = _dot_exact_rhs(t * t, e2)
    tn = t * lax.rsqrt(ms + EPS) * g2
    up = pltpu.roll(tn, LANES - ROT_DIM // 2, 1)
    dn = pltpu.roll(tn, ROT_DIM // 2, 1)
    return tn * cos_t + up * sin_lo + dn * sin_hi


def _rope_tables(pos):
    half = ROT_DIM // 2
    inv_freq = ROPE_THETA ** (-jnp.arange(0, ROT_DIM, 2, dtype=F32) / ROT_DIM)
    ang = pos.astype(F32)[:, None] * inv_freq[None, :]
    cos, sin = jnp.cos(ang), jnp.sin(ang)
    n = pos.shape[0]
    ones = jnp.ones((n, HEAD_DIM - ROT_DIM), F32)
    zeros = jnp.zeros((n, HEAD_DIM - ROT_DIM), F32)
    zh = jnp.zeros((n, half), F32)
    cos_t = jnp.concatenate([cos, cos, ones], axis=1)
    sin_lo = jnp.concatenate([-sin, zh, zeros], axis=1)
    sin_hi = jnp.concatenate([zh, sin, zeros], axis=1)
    rep = lambda t: jnp.concatenate([t, t], axis=1)
    return rep(cos_t), rep(sin_lo), rep(sin_hi)


def _head_avg_matrix():
    r = np.arange(LANES)
    return jnp.asarray((r[:, None] // HEAD_DIM == r[None, :] // HEAD_DIM).astype(np.float32) / HEAD_DIM, dtype=BF16)


def _attn_kernel(sink_ref, qkv_ref, cos_ref, slo_ref, shi_ref, gq_ref, gk_ref, e2_ref,
                 o_ref, kwin_ref, vwin_ref, kprev, vprev):
    n = pl.program_id(1)
    blk = ATTN_BLOCK

    @pl.when(n == 0)
    def _():
        kprev[...] = jnp.zeros_like(kprev)
        vprev[...] = jnp.zeros_like(vprev)

    cos_t, slo, shi, e2 = cos_ref[...], slo_ref[...], shi_ref[...], e2_ref[...]
    qkv = qkv_ref[...].astype(F32)
    k_cur = _head_norm_rope(qkv[:, ATTN_WIDTH:ATTN_WIDTH + KV_WIDTH], gk_ref[...], e2, cos_t, slo, shi)
    v_cur = qkv[:, ATTN_WIDTH + KV_WIDTH:]
    k_all = jnp.concatenate([kprev[...], k_cur], axis=0)
    v_all = jnp.concatenate([vprev[...], v_cur], axis=0)
    lane = lax.broadcasted_iota(I32, k_all.shape, 1)
    left = lane < HEAD_DIM

    def placed(t):
        sw = pltpu.roll(t, HEAD_DIM, 1)
        z = jnp.zeros_like(t)
        return ((jnp.where(left, t, z).astype(BF16), jnp.where(left, z, sw).astype(BF16)),
                (jnp.where(left, sw, z).astype(BF16), jnp.where(left, z, t).astype(BF16)))

    kp, vp = placed(k_all), placed(v_all)

    r = lax.broadcasted_iota(I32, (blk, 2 * blk), 0)
    c = lax.broadcasted_iota(I32, (blk, 2 * blk), 1)
    c_min = jnp.where(n == 0, blk, 0)
    valid = (c >= r) & (c <= r + WINDOW) & (c >= c_min)

    for j in range(N_Q_HEADS // 2):
        q_slab = _head_norm_rope(qkv[:, j * LANES:(j + 1) * LANES], gq_ref[...], e2, cos_t, slo, shi)
        q_slab = (q_slab * (HEAD_DIM ** -0.5)).astype(BF16)
        acc = jnp.zeros((blk, LANES), F32)
        for side in range(2):
            h = 2 * j + side
            g = h // (N_Q_HEADS // N_KV_HEADS)
            s = lax.dot_general(q_slab, kp[g][side], (((1,), (1,)), ((), ())), preferred_element_type=F32)
            s = jnp.where(valid, s, -jnp.inf)
            sink = sink_ref[h]
            m = jnp.maximum(jnp.max(s, axis=-1, keepdims=True), sink)
            p = jnp.exp(s - m)
            den = jnp.sum(p, axis=-1, keepdims=True) + jnp.exp(sink - m)
            p = (p * (1.0 / den)).astype(BF16)
            acc = acc + jnp.dot(p, vp[g][side], preferred_element_type=F32)
        o_ref[:, j * LANES:(j + 1) * LANES] = acc.astype(BF16)

    kprev[...] = k_cur
    vprev[...] = v_cur
    kwin_ref[...] = k_cur
    vwin_ref[...] = v_cur


def _attention_prompt(qkv, sinks, g_q, g_k, batch, seq_len):
    nb = seq_len // ATTN_BLOCK
    cos_t, slo, shi = _rope_tables(jnp.arange(seq_len))
    tab = pl.BlockSpec((ATTN_BLOCK, LANES), lambda b, n: (n, 0))
    vec = pl.BlockSpec((1, LANES), lambda b, n: (0, 0))
    win = pl.BlockSpec((None, ATTN_BLOCK, KV_WIDTH), lambda b, n: (b, 0, 0))
    return pl.pallas_call(
        _attn_kernel,
        grid=(batch, nb),
        in_specs=[pl.BlockSpec(memory_space=pltpu.SMEM),
                  pl.BlockSpec((ATTN_BLOCK, QKV_W), lambda b, n: (b * nb + n, 0)),
                  tab, tab, tab, vec, vec,
                  pl.BlockSpec((LANES, LANES), lambda b, n: (0, 0))],
        out_specs=[pl.BlockSpec((ATTN_BLOCK, ATTN_WIDTH), lambda b, n: (b * nb + n, 0)), win, win],
        out_shape=[jax.ShapeDtypeStruct((batch * seq_len, ATTN_WIDTH), BF16),
                   jax.ShapeDtypeStruct((batch, ATTN_BLOCK, KV_WIDTH), F32),
                   jax.ShapeDtypeStruct((batch, ATTN_BLOCK, KV_WIDTH), F32)],
        scratch_shapes=[pltpu.VMEM((ATTN_BLOCK, KV_WIDTH), F32), pltpu.VMEM((ATTN_BLOCK, KV_WIDTH), F32)],
        compiler_params=_params(("arbitrary", "arbitrary")),
        name="attn_prompt",
    )(sinks, qkv, cos_t, slo, shi, jnp.tile(g_q, 2)[None, :], jnp.tile(g_k, 2)[None, :], _head_avg_matrix())


def _conv_chunk(xbuf, x, w_ref, b_ref, rows):
    xbuf[SUBLANES:SUBLANES + rows, :] = x
    y = x * w_ref[CONV_K - 1:CONV_K, :] + b_ref[...]
    for j in range(1, CONV_K):
        y = y + xbuf[pl.ds(SUBLANES - j, rows), :] * w_ref[CONV_K - 1 - j:CONV_K - j, :]
    return y


def _lru_gates(xc, wa_ref, ba_ref, wx_ref, bx_ref, negsp_ref):
    xb = xc.astype(BF16)
    ra, ri = [], []
    for s in range(LRU_WIDTH // LANES):
        sl = slice(s * LANES, (s + 1) * LANES)
        ra.append(jnp.dot(xb[:, sl], wa_ref[s], preferred_element_type=F32))
        ri.append(jnp.dot(xb[:, sl], wx_ref[s], preferred_element_type=F32))
    r = _sigmoid(jnp.concatenate(ra, axis=1) + ba_ref[...])
    i = _sigmoid(jnp.concatenate(ri, axis=1) + bx_ref[...])
    log_a = r * negsp_ref[...]
    a = jnp.exp(log_a)
    u = jnp.sqrt(1.0 - a * a) * (i * xc)
    return a, u


def _lru_kernel(x_ref, gate_ref, cw_ref, cb_ref, wa_ref, ba_ref, wx_ref, bx_ref, negsp_ref,
                o_ref, h_ref, buf_ref, xbuf, hcar, *, rows):
    n = pl.program_id(1)

    @pl.when(n == 0)
    def _():
        xbuf[0:SUBLANES, :] = jnp.zeros((SUBLANES, LRU_WIDTH), F32)
        hcar[...] = jnp.zeros_like(hcar)

    x = x_ref[...]
    xc = _conv_chunk(xbuf, x, cw_ref, cb_ref, rows)
    a, u = _lru_gates(xc, wa_ref, ba_ref, wx_ref, bx_ref, negsp_ref)

    t = lax.broadcasted_iota(I32, a.shape, 0)
    d = 1
    while d < rows:
        keep = t >= d
        a_s = jnp.where(keep, pltpu.roll(a, d, 0), 1.0)
        u_s = jnp.where(keep, pltpu.roll(u, d, 0), 0.0)
        u = a * u_s + u
        a = a * a_s
        d *= 2
    h = a * hcar[...] + u
    o_ref[...] = (h * _gelu_tanh(gate_ref[...].astype(F32))).astype(BF16)
    hcar[...] = h[rows - 1:rows, :]
    h_ref[...] = h[rows - 1:rows, :]
    buf_ref[...] = x[rows - (CONV_K - 1):rows, :]
    xbuf[0:SUBLANES, :] = x[rows - SUBLANES:rows, :]


def _lru_weights(lru_w_a, lru_w_x, lru_lambda):
    def pair(w):
        w = w.astype(BF16).reshape(LRU_BLOCKS // 2, 2, LRU_BLOCK_DIM, LRU_BLOCK_DIM)
        z = jnp.zeros((LRU_BLOCKS // 2, LRU_BLOCK_DIM, LRU_BLOCK_DIM), BF16)
        top = jnp.concatenate([w[:, 0], z], axis=2)
        bot = jnp.concatenate([z, w[:, 1]], axis=2)
        return jnp.concatenate([top, bot], axis=1)
    return pair(lru_w_a), pair(lru_w_x), (-LRU_C * jax.nn.softplus(-lru_lambda.astype(F32)))[None, :]


def _lru_prompt(lx, lgate, cw, cb, wa, ba, wx, bx, negsp, batch, seq_len, rows):
    nc = seq_len // rows
    full = lambda shape: pl.BlockSpec(shape, lambda b, n: (0,) * len(shape))
    tok = lambda w: pl.BlockSpec((rows, w), lambda b, n: (b * nc + n, 0))
    return pl.pallas_call(
        functools.partial(_lru_kernel, rows=rows),
        grid=(batch, nc),
        in_specs=[tok(LRU_WIDTH), tok(LRU_WIDTH), full((CONV_K, LRU_WIDTH)), full((1, LRU_WIDTH)),
                  full((LRU_WIDTH // LANES, LANES, LANES)), full((1, LRU_WIDTH)),
                  full((LRU_WIDTH // LANES, LANES, LANES)), full((1, LRU_WIDTH)), full((1, LRU_WIDTH))],
        out_specs=[tok(LRU_WIDTH),
                   pl.BlockSpec((None, 1, LRU_WIDTH), lambda b, n: (b, 0, 0)),
                   pl.BlockSpec((None, CONV_K - 1, LRU_WIDTH), lambda b, n: (b, 0, 0))],
        out_shape=[jax.ShapeDtypeStruct((batch * seq_len, LRU_WIDTH), BF16),
                   jax.ShapeDtypeStruct((batch, 1, LRU_WIDTH), F32),
                   jax.ShapeDtypeStruct((batch, CONV_K - 1, LRU_WIDTH), F32)],
        scratch_shapes=[pltpu.VMEM((rows + SUBLANES, LRU_WIDTH), F32), pltpu.VMEM((1, LRU_WIDTH), F32)],
        compiler_params=_params(("arbitrary", "arbitrary")),
        name="lru_prompt",
    )(lx, lgate, cw, cb, wa, ba, wx, bx, negsp)


def _ssd_kernel(xbc_ref, dt_ref, z_ref, cw_ref, cb_ref, dtb_ref, aneg_ref, rexp_ref, d_ref, gn_ref,
                o_ref, st_ref, buf_ref, xbuf, state):
    n = pl.program_id(1)
    q = SSD_CHUNK
    gw = SSD_INNER // SSD_GROUPS

    @pl.when(n == 0)
    def _():
        xbuf[0:SUBLANES, :] = jnp.zeros((SUBLANES, SSD_CONV_DIM), F32)
        state[...] = jnp.zeros_like(state)

    xbc = xbc_ref[...]
    xc = _silu(_conv_chunk(xbuf, xbc, cw_ref, cb_ref, q))
    xs = xc[:, :SSD_INNER]
    bm = xc[:, SSD_INNER:SSD_INNER + LANES]
    cm = xc[:, SSD_INNER + LANES:]

    rexp = rexp_ref[...]
    dt = _softplus(dt_ref[...] + dtb_ref[...])
    dta = dt * aneg_ref[...]
    li = lax.broadcasted_iota(I32, (q, q), 0)
    si = lax.broadcasted_iota(I32, (q, q), 1)
    causal = li >= si
    tri = jnp.where(causal, 1.0, 0.0).astype(BF16)
    a_cs = _dot_exact_lhs(tri, dta)
    a_cs_t = a_cs.T
    dt_e = _dot_exact_rhs(dt, rexp)
    a_e = _dot_exact_rhs(a_cs, rexp)
    a_last = a_e[q - 1:q, :]

    xdt = xs * dt_e
    lane = lax.broadcasted_iota(I32, (q, LANES), 1)
    left = lane < SSD_STATE
    zero = jnp.zeros((q, LANES), F32)
    cmb = cm.astype(BF16)
    bmb = bm.astype(BF16)
    cm_g = (jnp.where(left, cm, zero).astype(BF16), jnp.where(left, zero, cm).astype(BF16))
    scores = [lax.dot_general(cm_g[g], bmb, (((1,), (1,)), ((), ())), preferred_element_type=F32)
              for g in range(SSD_GROUPS)]

    y_off = jnp.dot(cmb, state[...].astype(BF16), preferred_element_type=F32) * jnp.exp(a_e)

    ys = []
    for j in range(SSD_INNER // LANES):
        xs_slab = xdt[:, j * LANES:(j + 1) * LANES]
        x_side = (jnp.where(left, xs_slab, zero).astype(BF16), jnp.where(left, zero, xs_slab).astype(BF16))
        acc = y_off[:, j * LANES:(j + 1) * LANES]
        for side in range(2):
            h = 2 * j + side
            g = h // (SSD_HEADS // SSD_GROUPS)
            col = a_e[:, h * SSD_HEAD_DIM:h * SSD_HEAD_DIM + 1]
            row = a_cs_t[h:h + 1, :]
            decay = jnp.exp(jnp.where(causal, col - row, -jnp.inf))
            acc = acc + jnp.dot((scores[g] * decay).astype(BF16), x_side[side], preferred_element_type=F32)
        ys.append(acc + d_ref[:, j * LANES:(j + 1) * LANES] * xs[:, j * LANES:(j + 1) * LANES])
    y = jnp.concatenate(ys, axis=1)
    o_ref[...] = _ssd_gate_norm(y, z_ref[...].astype(F32), gn_ref[...]).astype(BF16)

    x_end = (xdt * jnp.exp(a_last - a_e)).astype(BF16)
    upd = jnp.dot(bm.T.astype(BF16), x_end, preferred_element_type=F32)
    rg = lax.broadcasted_iota(I32, (LANES, SSD_INNER), 0) // SSD_STATE
    cg = lax.broadcasted_iota(I32, (LANES, SSD_INNER), 1) // gw
    new_state = jnp.where(rg == cg, jnp.exp(a_last) * state[...] + upd, 0.0)
    state[...] = new_state
    st_ref[...] = new_state
    buf_ref[...] = xbc[q - (CONV_K - 1):q, :]
    xbuf[0:SUBLANES, :] = xbc[q - SUBLANES:q, :]


def _ssd_gate_norm(y, z, gn):
    u = y * _silu(z)
    gw = SSD_INNER // SSD_GROUPS
    parts = []
    for g in range(SSD_GROUPS):
        ug = u[:, g * gw:(g + 1) * gw]
        parts.append(ug * lax.rsqrt(jnp.mean(ug * ug, axis=-1, keepdims=True) + EPS))
    return jnp.concatenate(parts, axis=1) * gn


def _ssd_consts(ssd_dt_bias, ssd_a_log, ssd_d):
    pad = lambda v: jnp.pad(v.astype(F32), (0, LANES - SSD_HEADS))[None, :]
    h = np.arange(LANES)[:, None]
    c = np.arange(SSD_INNER)[None, :]
    rexp = jnp.asarray((h == c // SSD_HEAD_DIM).astype(np.float32), dtype=BF16)
    return (pad(ssd_dt_bias), pad(-jnp.exp(ssd_a_log.astype(F32))), rexp,
            jnp.repeat(ssd_d.astype(F32), SSD_HEAD_DIM)[None, :])


def _ssd_prompt(xbc, dtr, z, cw, cb, dtb, aneg, rexp, d_e, gn, batch, seq_len):
    nc = seq_len // SSD_CHUNK
    full = lambda shape: pl.BlockSpec(shape, lambda b, n: (0,) * len(shape))
    tok = lambda w: pl.BlockSpec((SSD_CHUNK, w), lambda b, n: (b * nc + n, 0))
    return pl.pallas_call(
        _ssd_kernel,
        grid=(batch, nc),
        in_specs=[tok(SSD_CONV_DIM), tok(LANES), tok(SSD_INNER),
                  full((CONV_K, SSD_CONV_DIM)), full((1, SSD_CONV_DIM)), full((1, LANES)), full((1, LANES)),
                  full((LANES, SSD_INNER)), full((1, SSD_INNER)), full((1, SSD_INNER))],
        out_specs=[tok(SSD_INNER),
                   pl.BlockSpec((None, LANES, SSD_INNER), lambda b, n: (b, 0, 0)),
                   pl.BlockSpec((None, CONV_K - 1, SSD_CONV_DIM), lambda b, n: (b, 0, 0))],
        out_shape=[jax.ShapeDtypeStruct((batch * seq_len, SSD_INNER), BF16),
                   jax.ShapeDtypeStruct((batch, LANES, SSD_INNER), F32),
                   jax.ShapeDtypeStruct((batch, CONV_K - 1, SSD_CONV_DIM), F32)],
        scratch_shapes=[pltpu.VMEM((SSD_CHUNK + SUBLANES, SSD_CONV_DIM), F32), pltpu.VMEM((LANES, SSD_INNER), F32)],
        compiler_params=_params(("arbitrary", "arbitrary")),
        name="ssd_prompt",
    )(xbc, dtr, z, cw, cb, dtb, aneg, rexp, d_e, gn)


def _ssd_state_to_hpn(st):
    b = st.shape[0]
    hpg = SSD_HEADS // SSD_GROUPS
    gw = SSD_INNER // SSD_GROUPS
    parts = []
    for g in range(SSD_GROUPS):
        blk = st[:, g * SSD_STATE:(g + 1) * SSD_STATE, g * gw:(g + 1) * gw]
        parts.append(blk.reshape(b, SSD_STATE, hpg, SSD_HEAD_DIM).transpose(0, 2, 3, 1))
    return jnp.concatenate(parts, axis=1)


def _merge_kernel(*refs, with_router):
    (oa_ref, ol_ref, os_ref, gt_ref, x_ref, g1_ref, sc2_ref, sh2_ref, gn_ref,
     wa_ref, wl_ref, ws_ref, wo_ref) = refs[:13]
    if with_router:
        wr_ref, br_ref, x1_ref, h2_ref, lg_ref = refs[13:]
    else:
        x1_ref, h2_ref = refs[13:]
    d = D_MODEL
    mm = lambda a, w: jnp.dot(a[...], w[...], preferred_element_type=F32)
    merged = (_sigmoid(gt_ref[:, 0:d].astype(F32)) * mm(oa_ref, wa_ref)
              + _sigmoid(gt_ref[:, d:2 * d].astype(F32)) * mm(ol_ref, wl_ref)
              + _sigmoid(gt_ref[:, 2 * d:3 * d].astype(F32)) * mm(os_ref, ws_ref))
    mix = jnp.dot(merged.astype(BF16), wo_ref[...], preferred_element_type=F32)
    x1 = x_ref[...] + g1_ref[...] * mix
    x1_ref[...] = x1
    h2 = _rms(x1, gn_ref[...]) * (1.0 + sc2_ref[...]) + sh2_ref[...]
    h2_ref[...] = h2.astype(h2_ref.dtype)
    if with_router:
        lg_ref[...] = jnp.dot(h2, wr_ref[...], preferred_element_type=F32,
                              precision=lax.Precision.HIGHEST) + br_ref[...]


def _merge(oa, ol, os_, gt, x, gate1, scale2, shift2, gn, wa, wl, ws, wo, router, per_row, seq_len, tm):
    m = x.shape[0]
    row = lambda w: pl.BlockSpec((tm, w), lambda i: (i, 0))
    full = lambda shape: pl.BlockSpec(shape, lambda i: (0,) * len(shape))
    ms = _mod_spec(per_row, tm, seq_len)
    in_specs = [row(ATTN_WIDTH), row(LRU_WIDTH), row(SSD_INNER), row(GATES_W), row(D_MODEL), ms, ms, ms,
                full((1, D_MODEL)), full((ATTN_WIDTH, D_MODEL)), full((LRU_WIDTH, D_MODEL)),
                full((SSD_INNER, D_MODEL)), full((D_MODEL, D_MODEL))]
    args = [oa, ol, os_, gt, x, gate1, scale2, shift2, gn, wa, wl, ws, wo]
    out_specs = [row(D_MODEL), row(D_MODEL)]
    out_shape = [jax.ShapeDtypeStruct((m, D_MODEL), F32),
                 jax.ShapeDtypeStruct((m, D_MODEL), F32 if router is not None else BF16)]
    if router is not None:
        in_specs += [full((D_MODEL, LANES)), full((1, LANES))]
        args += list(router)
        out_specs.append(row(LANES))
        out_shape.append(jax.ShapeDtypeStruct((m, LANES), F32))
    return pl.pallas_call(
        functools.partial(_merge_kernel, with_router=router is not None),
        grid=(m // tm,),
        in_specs=in_specs, out_specs=out_specs, out_shape=out_shape,
        compiler_params=_params(("arbitrary",)),
        name="merge",
    )(*args)


def _ffn_kernel(te_ref, nv_ref, *refs, fused_residual):
    if fused_residual:
        x_ref, wg_ref, wu_ref, wd_ref, res_ref, g2_ref, o_ref = refs
    else:
        x_ref, wg_ref, wu_ref, wd_ref, o_ref = refs
    i = pl.program_id(0)
    j = pl.program_id(1)

    @pl.when(j == 0)
    def _():
        o_ref[...] = jnp.zeros_like(o_ref)

    @pl.when(i < nv_ref[0])
    def _():
        h = x_ref[...].astype(BF16)
        a = jnp.dot(h, wg_ref[...], preferred_element_type=F32)
        b = jnp.dot(h, wu_ref[...], preferred_element_type=F32)
        t = (_silu(a) * b).astype(BF16)
        o_ref[...] += jnp.dot(t, wd_ref[...], preferred_element_type=F32)

    if fused_residual:
        @pl.when(j == pl.num_programs(1) - 1)
        def _():
            o_ref[...] = res_ref[...] + g2_ref[...] * o_ref[...]


def _ffn(x, wg, wu, wd, tile_expert, n_valid, tm, residual=None, gate2=None, per_row=False, seq_len=1):
    m = x.shape[0]
    nj = D_FF // FF_CHUNK
    last = nj - 1

    def jj(i, j, nv):
        return jnp.where(i < nv[0], j, last)

    in_specs = [pl.BlockSpec((tm, D_MODEL), lambda i, j, te, nv: (jnp.minimum(i, nv[0] - 1), 0)),
                pl.BlockSpec((None, D_MODEL, FF_CHUNK), lambda i, j, te, nv: (te[i], 0, jj(i, j, nv))),
                pl.BlockSpec((None, D_MODEL, FF_CHUNK), lambda i, j, te, nv: (te[i], 0, jj(i, j, nv))),
                pl.BlockSpec((None, FF_CHUNK, D_MODEL), lambda i, j, te, nv: (te[i], jj(i, j, nv), 0))]
    args = [x, wg, wu, wd]
    fused = residual is not None
    if fused:
        in_specs.append(pl.BlockSpec((tm, D_MODEL), lambda i, j, te, nv: (i, 0)))
        if per_row:
            in_specs.append(pl.BlockSpec((tm, D_MODEL), lambda i, j, te, nv: (i, 0)))
        else:
            in_specs.append(pl.BlockSpec((None, 1, D_MODEL), lambda i, j, te, nv: ((i * tm) // seq_len, 0, 0)))
        args += [residual, gate2]
    return pl.pallas_call(
        functools.partial(_ffn_kernel, fused_residual=fused),
        grid_spec=pltpu.PrefetchScalarGridSpec(
            num_scalar_prefetch=2, grid=(m // tm, nj), in_specs=in_specs,
            out_specs=pl.BlockSpec((tm, D_MODEL), lambda i, j, te, nv: (i, 0))),
        out_shape=jax.ShapeDtypeStruct((m, D_MODEL), F32),
        compiler_params=_params(("arbitrary", "arbitrary")),
        name="ffn",
    )(tile_expert, n_valid, *args)


def _route_kernel(lg_ref, i1_ref, i2_ref, r1_ref, r2_ref, w1_ref, w2_ref, cnt_ref, carry, *, tb):
    @pl.when(pl.program_id(0) == 0)
    def _():
        carry[...] = jnp.zeros_like(carry)

    l8 = lg_ref[...].T[0:N_EXPERTS, :]
    e = lax.broadcasted_iota(I32, l8.shape, 0)
    m1 = jnp.max(l8, axis=0, keepdims=True)
    i1 = jnp.min(jnp.where(l8 == m1, e, N_EXPERTS), axis=0, keepdims=True)
    rest = jnp.where(e == i1, -jnp.inf, l8)
    m2 = jnp.max(rest, axis=0, keepdims=True)
    i2 = jnp.min(jnp.where(rest == m2, e, N_EXPERTS), axis=0, keepdims=True)
    t = jnp.exp(m2 - m1)
    w1 = 1.0 / (1.0 + t)
    sel1 = jnp.where(e == i1, 1.0, 0.0)
    sel2 = jnp.where(e == i2, 1.0, 0.0)
    sel = sel1 + sel2
    a = lax.broadcasted_iota(I32, (tb, tb), 0)
    b = lax.broadcasted_iota(I32, (tb, tb), 1)
    upper = jnp.where(a <= b, 1.0, 0.0).astype(BF16)
    incl = jnp.dot(sel.astype(BF16), upper, preferred_element_type=F32)
    rank = carry[:, 0:1] + incl - sel
    i1_ref[...] = i1
    i2_ref[...] = i2
    r1_ref[...] = jnp.sum(sel1 * rank, axis=0, keepdims=True).astype(I32)
    r2_ref[...] = jnp.sum(sel2 * rank, axis=0, keepdims=True).astype(I32)
    w1_ref[...] = w1
    w2_ref[...] = t * w1
    carry[...] = carry[...] + incl[:, tb - 1:tb]
    cnt_ref[...] = carry[...]


def _route(logits, tb):
    m = logits.shape[0]
    vec = pl.BlockSpec((1, tb), lambda i: (0, i))
    return pl.pallas_call(
        functools.partial(_route_kernel, tb=tb),
        grid=(m // tb,),
        in_specs=[pl.BlockSpec((tb, LANES), lambda i: (i, 0))],
        out_specs=[vec, vec, vec, vec, vec, vec, pl.BlockSpec((N_EXPERTS, LANES), lambda i: (0, 0))],
        out_shape=[jax.ShapeDtypeStruct((1, m), I32), jax.ShapeDtypeStruct((1, m), I32),
                   jax.ShapeDtypeStruct((1, m), I32), jax.ShapeDtypeStruct((1, m), I32),
                   jax.ShapeDtypeStruct((1, m), F32), jax.ShapeDtypeStruct((1, m), F32),
                   jax.ShapeDtypeStruct((N_EXPERTS, LANES), F32)],
        scratch_shapes=[pltpu.VMEM((N_EXPERTS, LANES), F32)],
        compiler_params=_params(("arbitrary",)),
        name="route",
    )(logits)


def _dispatch_kernel(p1_ref, p2_ref, x_ref, dst_in_ref, dst_ref, sem, *, tb):
    del dst_in_ref

    def copies(r):
        src = x_ref.at[pl.ds(r, 1), :]
        return (pltpu.make_async_copy(src, dst_ref.at[pl.ds(p1_ref[0, r], 1), :], sem.at[0]),
                pltpu.make_async_copy(src, dst_ref.at[pl.ds(p2_ref[0, r], 1), :], sem.at[1]))

    def start(r, c):
        for cp in copies(r):
            cp.start()
        return c

    def wait(r, c):
        for cp in copies(r):
            cp.wait()
        return c

    lax.fori_loop(0, tb, start, 0)
    lax.fori_loop(0, tb, wait, 0)


def _dispatch(x, pos1, pos2, n_rows, tb):
    m = x.shape[0]
    smem = pl.BlockSpec((None, 1, tb), lambda i: (i, 0, 0), memory_space=pltpu.SMEM)
    return pl.pallas_call(
        functools.partial(_dispatch_kernel, tb=tb),
        grid=(m // tb,),
        in_specs=[smem, smem, pl.BlockSpec((tb, D_MODEL), lambda i: (i, 0)),
                  pl.BlockSpec(memory_space=pl.ANY)],
        out_specs=pl.BlockSpec(memory_space=pl.ANY),
        out_shape=jax.ShapeDtypeStruct((n_rows, D_MODEL), F32),
        scratch_shapes=[pltpu.SemaphoreType.DMA((2,))],
        input_output_aliases={3: 0},
        compiler_params=_params(("arbitrary",)),
        name="dispatch",
    )(pos1.reshape(m // tb, 1, tb), pos2.reshape(m // tb, 1, tb), x, jnp.zeros((n_rows, D_MODEL), F32))


def _combine_kernel(p1_ref, p2_ref, y_ref, x1_ref, g2_ref, w1_ref, w2_ref, o_ref, buf1, buf2, sem, *, tb):
    def copies(r):
        return (pltpu.make_async_copy(y_ref.at[pl.ds(p1_ref[0, r], 1), :], buf1.at[pl.ds(r, 1), :], sem.at[0]),
                pltpu.make_async_copy(y_ref.at[pl.ds(p2_ref[0, r], 1), :], buf2.at[pl.ds(r, 1), :], sem.at[1]))

    def start(r, c):
        for cp in copies(r):
            cp.start()
        return c

    def wait(r, c):
        for cp in copies(r):
            cp.wait()
        return c

    lax.fori_loop(0, tb, start, 0)
    lax.fori_loop(0, tb, wait, 0)
    f = w1_ref[...] * buf1[...] + w2_ref[...] * buf2[...]
    o_ref[...] = x1_ref[...] + g2_ref[...] * f


def _combine(y, pos1, pos2, w1, w2, x1, gate2, per_row, seq_len, tb):
    m = x1.shape[0]
    smem = pl.BlockSpec((None, 1, tb), lambda i: (i, 0, 0), memory_space=pltpu.SMEM)
    row = lambda w: pl.BlockSpec((tb, w), lambda i: (i, 0))
    return pl.pallas_call(
        functools.partial(_combine_kernel, tb=tb),
        grid=(m // tb,),
        in_specs=[smem, smem, pl.BlockSpec(memory_space=pl.ANY), row(D_MODEL),
                  _mod_spec(per_row, tb, seq_len), row(1), row(1)],
        out_specs=row(D_MODEL),
        out_shape=jax.ShapeDtypeStruct((m, D_MODEL), F32),
        scratch_shapes=[pltpu.VMEM((tb, D_MODEL), F32), pltpu.VMEM((tb, D_MODEL), F32),
                        pltpu.SemaphoreType.DMA((2,))],
        compiler_params=_params(("arbitrary",)),
        name="combine",
    )(pos1.reshape(m // tb, 1, tb), pos2.reshape(m // tb, 1, tb), y, x1, gate2,
      w1.reshape(m, 1), w2.reshape(m, 1))


def _moe(h2, logits, x1, gate2, wg, wu, wd, per_row, seq_len, tm, tb):
    m = h2.shape[0]
    i1, i2, r1, r2, w1, w2, cnt = _route(logits, tb)
    counts = cnt[:, 0].astype(I32)
    padded = ((counts + tm - 1) // tm) * tm
    ends = jnp.cumsum(padded)
    starts = ends - padded
    n_tiles = (m * TOP_K) // tm + N_EXPERTS
    n_valid = (ends[-1] // tm).astype(I32)
    tile_start = jnp.arange(n_tiles, dtype=I32) * tm
    te = jnp.sum((tile_start[:, None] >= ends[None, :]).astype(I32), axis=1)
    te_last = jnp.sum((jnp.maximum(ends[-1] - tm, 0) >= ends).astype(I32))
    te = jnp.where(jnp.arange(n_tiles) < n_valid, te, te_last).astype(I32)
    pos1 = (starts[i1[0]] + r1[0]).astype(I32)
    pos2 = (starts[i2[0]] + r2[0]).astype(I32)
    hs = _dispatch(h2, pos1, pos2, n_tiles * tm, tb)
    y = _ffn(hs, wg, wu, wd, te, n_valid[None], tm)
    return _combine(y, pos1, pos2, w1[0], w2[0], x1, gate2, per_row, seq_len, tb)


def _qk_decode_kernel(qkv_ref, cos_ref, slo_ref, shi_ref, gq_ref, gk_ref, e2_ref, q_ref, k_ref, v_ref):
    qkv = qkv_ref[...].astype(F32)
    cos_t, slo, shi, e2 = cos_ref[...], slo_ref[...], shi_ref[...], e2_ref[...]
    for j in range(N_Q_HEADS // 2):
        qs = _head_norm_rope(qkv[:, j * LANES:(j + 1) * LANES], gq_ref[...], e2, cos_t, slo, shi)
        q_ref[:, j * LANES:(j + 1) * LANES] = qs * (HEAD_DIM ** -0.5)
    k_ref[...] = _head_norm_rope(qkv[:, ATTN_WIDTH:ATTN_WIDTH + KV_WIDTH], gk_ref[...], e2, cos_t, slo, shi)
    v_ref[...] = qkv[:, ATTN_WIDTH + KV_WIDTH:]


def _attn_decode_kernel(q_ref, ck_ref, cv_ref, kn_ref, vn_ref, sink_ref, o_ref, kw_ref, vw_ref):
    q = q_ref[...].astype(BF16)
    ck, cv = ck_ref[...], cv_ref[...]
    kn, vn = kn_ref[...], vn_ref[...]
    s = jnp.einsum('bhd,bwd->bhw', q, ck.astype(BF16), preferred_element_type=F32)
    knb = kn.astype(BF16).astype(F32)
    s_new = jnp.sum(q.astype(F32) * knb[:, None, :], axis=-1, keepdims=True)
    sink = sink_ref[...]
    m = jnp.maximum(jnp.maximum(jnp.max(s, axis=-1, keepdims=True), s_new), sink)
    p = jnp.exp(s - m)
    pn = jnp.exp(s_new - m)
    inv = 1.0 / (jnp.sum(p, axis=-1, keepdims=True) + pn + jnp.exp(sink - m))
    o = jnp.einsum('bhw,bwd->bhd', (p * inv).astype(BF16), cv.astype(BF16), preferred_element_type=F32)
    o_ref[...] = o + (pn * inv).astype(BF16).astype(F32) * vn.astype(BF16).astype(F32)[:, None, :]
    w = lax.broadcasted_iota(I32, ck.shape, 1)
    last = w == ck.shape[1] - 1
    kw_ref[...] = jnp.where(last, kn[:, None, :], pltpu.roll(ck, ck.shape[1] - 1, 1))
    vw_ref[...] = jnp.where(last, vn[:, None, :], pltpu.roll(cv, cv.shape[1] - 1, 1))


def _attention_decode(qkv, cache_k, cache_v, sinks, g_q, g_k, tb):
    n = qkv.shape[0]
    win = cache_k.shape[1]
    assert win == WINDOW and PAST_LEN >= WINDOW
    cos_t, slo, shi = _rope_tables(jnp.full((1,), PAST_LEN))
    qn, kn, vn = pl.pallas_call(
        _qk_decode_kernel,
        out_shape=[jax.ShapeDtypeStruct((n, ATTN_WIDTH), F32),
                   jax.ShapeDtypeStruct((n, KV_WIDTH), F32),
                   jax.ShapeDtypeStruct((n, KV_WIDTH), F32)],
        compiler_params=pltpu.CompilerParams(vmem_limit_bytes=VMEM_LIMIT),
        name="qk_decode",
    )(qkv, cos_t, slo, shi, jnp.tile(g_q, 2)[None, :], jnp.tile(g_k, 2)[None, :], _head_avg_matrix())
    grp = N_Q_HEADS // N_KV_HEADS
    qh = qn.reshape(n, N_KV_HEADS, grp, HEAD_DIM)
    z = jnp.zeros_like(qh[:, 0])
    q_pl = jnp.concatenate([jnp.concatenate([qh[:, 0], z], axis=-1),
                            jnp.concatenate([z, qh[:, 1]], axis=-1)], axis=1)
    ck = cache_k.reshape(n, win, KV_WIDTH)
    cv = cache_v.reshape(n, win, KV_WIDTH)
    blk3 = lambda a, b: pl.BlockSpec((tb, a, b), lambda i: (i, 0, 0))
    blk2 = pl.BlockSpec((tb, KV_WIDTH), lambda i: (i, 0))
    o, kw, vw = pl.pallas_call(
        _attn_decode_kernel,
        grid=(n // tb,),
        in_specs=[blk3(N_Q_HEADS, LANES), blk3(win, KV_WIDTH), blk3(win, KV_WIDTH), blk2, blk2,
                  pl.BlockSpec((1, N_Q_HEADS, 1), lambda i: (0, 0, 0))],
        out_specs=[blk3(N_Q_HEADS, LANES), blk3(win, KV_WIDTH), blk3(win, KV_WIDTH)],
        out_shape=[jax.ShapeDtypeStruct((n, N_Q_HEADS, LANES), F32),
                   jax.ShapeDtypeStruct((n, win, KV_WIDTH), F32),
                   jax.ShapeDtypeStruct((n, win, KV_WIDTH), F32)],
        compiler_params=_params(("arbitrary",)),
        name="attn_decode",
    )(q_pl, ck, cv, kn, vn, sinks.astype(F32).reshape(1, N_Q_HEADS, 1))
    o = o.reshape(n, N_KV_HEADS, grp, N_KV_HEADS, HEAD_DIM)
    o = jnp.concatenate([o[:, 0, :, 0], o[:, 1, :, 1]], axis=1).reshape(n, ATTN_WIDTH)
    return o.astype(BF16), kw, vw


def _lru_decode_kernel(x_ref, gate_ref, buf_ref, h0_ref, cw_ref, cb_ref, wa_ref, ba_ref, wx_ref, bx_ref,
                       negsp_ref, o_ref, h_ref, nbuf_ref):
    x = x_ref[...]
    xc = x * cw_ref[CONV_K - 1:CONV_K, :] + cb_ref[...]
    for j in range(CONV_K - 1):
        xc = xc + buf_ref[j] * cw_ref[j:j + 1, :]
    a, u = _lru_gates(xc, wa_ref, ba_ref, wx_ref, bx_ref, negsp_ref)
    h = a * h0_ref[...] + u
    o_ref[...] = (h * _gelu_tanh(gate_ref[...].astype(F32))).astype(BF16)
    h_ref[...] = h
    for j in range(CONV_K - 2):
        nbuf_ref[j] = buf_ref[j + 1]
    nbuf_ref[CONV_K - 2] = x


def _lru_decode(lx, lgate, buf, h0, cw, cb, wa, ba, wx, bx, negsp):
    n = lx.shape[0]
    o, h, nbuf = pl.pallas_call(
        _lru_decode_kernel,
        out_shape=[jax.ShapeDtypeStruct((n, LRU_WIDTH), BF16),
                   jax.ShapeDtypeStruct((n, LRU_WIDTH), F32),
                   jax.ShapeDtypeStruct((CONV_K - 1, n, LRU_WIDTH), F32)],
        compiler_params=pltpu.CompilerParams(vmem_limit_bytes=VMEM_LIMIT),
        name="lru_decode",
    )(lx, lgate, buf.transpose(1, 0, 2), h0, cw, cb, wa, ba, wx, bx, negsp)
    return o, h, nbuf.transpose(1, 0, 2)


def _ssd_pre_decode_kernel(xbc_ref, dt_ref, buf_ref, cw_ref, cb_ref, dtb_ref, aneg_ref, rexp_ref,
                           xs_ref, xdt_ref, dec_ref, bm_ref, cm_ref, nbuf_ref):
    xbc = xbc_ref[...]
    xc = xbc * cw_ref[CONV_K - 1:CONV_K, :] + cb_ref[...]
    for j in range(CONV_K - 1):
        xc = xc + buf_ref[j] * cw_ref[j:j + 1, :]
    xc = _silu(xc)
    xs = xc[:, :SSD_INNER]
    rexp = rexp_ref[...]
    dt = _softplus(dt_ref[...] + dtb_ref[...])
    xs_ref[...] = xs
    xdt_ref[...] = xs * _dot_exact_rhs(dt, rexp)
    dec_ref[...] = jnp.exp(_dot_exact_rhs(dt * aneg_ref[...], rexp))
    bm_ref[...] = xc[:, SSD_INNER:SSD_INNER + LANES]
    cm_ref[...] = xc[:, SSD_INNER + LANES:]
    for j in range(CONV_K - 2):
        nbuf_ref[j] = buf_ref[j + 1]
    nbuf_ref[CONV_K - 2] = xbc


def _ssd_state_decode_kernel(h0_ref, x_ref, dec_ref, bm_ref, cm_ref, rep_ref, tile_ref, sum_ref,
                             hn_ref, y_ref, *, tb):
    pn = SSD_HEAD_DIM * SSD_STATE
    x = x_ref[...].reshape(tb * SSD_HEADS, SSD_HEAD_DIM).astype(BF16)
    xrep = jnp.dot(x, rep_ref[...], preferred_element_type=F32).reshape(tb, SSD_HEADS, pn)
    bmb, cmb = bm_ref[...].astype(BF16), cm_ref[...].astype(BF16)
    first = lax.broadcasted_iota(I32, (tb, SSD_HEADS, pn), 1) < SSD_HEADS // SSD_GROUPS

    def tiled(v):
        t0 = jnp.dot(v, tile_ref[0], preferred_element_type=F32)
        t1 = jnp.dot(v, tile_ref[1], preferred_element_type=F32)
        return jnp.where(first, t0[:, None, :], t1[:, None, :])

    h_new = dec_ref[...][:, :, 0:1] * h0_ref[...] + xrep * tiled(bmb)
    hn_ref[...] = h_new
    hc = (h_new * tiled(cmb)).reshape(tb * SSD_HEADS, pn).astype(BF16)
    y_ref[...] = jnp.dot(hc, sum_ref[...], preferred_element_type=F32).reshape(tb, SSD_HEADS, SSD_HEAD_DIM)


def _ssd_post_decode_kernel(y_ref, xs_ref, z_ref, d_ref, gn_ref, o_ref):
    y = y_ref[...] + d_ref[...] * xs_ref[...]
    o_ref[...] = _ssd_gate_norm(y, z_ref[...].astype(F32), gn_ref[...]).astype(BF16)


def _ssd_decode_consts():
    pn = SSD_HEAD_DIM * SSD_STATE
    col_p = np.arange(pn) // SSD_STATE
    col_n = np.arange(pn) % SSD_STATE
    rep = (np.arange(SSD_HEAD_DIM)[:, None] == col_p[None, :]).astype(np.float32)
    row_g = np.arange(LANES) // SSD_STATE
    row_n = np.arange(LANES) % SSD_STATE
    tile = np.stack([((row_g[:, None] == g) & (row_n[:, None] == col_n[None, :])).astype(np.float32)
                     for g in range(SSD_GROUPS)])
    summ = rep.T
    return jnp.asarray(rep, BF16), jnp.asarray(tile, BF16), jnp.asarray(summ, BF16)


def _ssd_decode(xbc, dtr, z, buf, h0, cw, cb, dtb, aneg, rexp, d_e, gn, tb):
    n = xbc.shape[0]
    pn = SSD_HEAD_DIM * SSD_STATE
    f = lambda w: jax.ShapeDtypeStruct((n, w), F32)
    xs, xdt, dec, bm, cm, nbuf = pl.pallas_call(
        _ssd_pre_decode_kernel,
        out_shape=[f(SSD_INNER), f(SSD_INNER), f(SSD_INNER), f(LANES), f(LANES),
                   jax.ShapeDtypeStruct((CONV_K - 1, n, SSD_CONV_DIM), F32)],
        compiler_params=pltpu.CompilerParams(vmem_limit_bytes=VMEM_LIMIT),
        name="ssd_pre_decode",
    )(xbc, dtr, buf.transpose(1, 0, 2), cw, cb, dtb, aneg, rexp)
    rep, tile, summ = _ssd_decode_consts()
    b3 = lambda a, b: pl.BlockSpec((tb, a, b), lambda i: (i, 0, 0))
    b2 = pl.BlockSpec((tb, LANES), lambda i: (i, 0))
    full = lambda shape: pl.BlockSpec(shape, lambda i: (0,) * len(shape))
    h_new, y = pl.pallas_call(
        functools.partial(_ssd_state_decode_kernel, tb=tb),
        grid=(n // tb,),
        in_specs=[b3(SSD_HEADS, pn), b3(SSD_HEADS, SSD_HEAD_DIM), b3(SSD_HEADS, SSD_HEAD_DIM), b2, b2,
                  full((SSD_HEAD_DIM, pn)), full((SSD_GROUPS, LANES, pn)), full((pn, SSD_HEAD_DIM))],
        out_specs=[b3(SSD_HEADS, pn), b3(SSD_HEADS, SSD_HEAD_DIM)],
        out_shape=[jax.ShapeDtypeStruct((n, SSD_HEADS, pn), F32),
                   jax.ShapeDtypeStruct((n, SSD_HEADS, SSD_HEAD_DIM), F32)],
        compiler_params=_params(("arbitrary",)),
        name="ssd_state_decode",
    )(h0.reshape(n, SSD_HEADS, pn), xdt.reshape(n, SSD_HEADS, SSD_HEAD_DIM),
      dec.reshape(n, SSD_HEADS, SSD_HEAD_DIM), bm, cm, rep, tile, summ)
    o = pl.pallas_call(
        _ssd_post_decode_kernel,
        out_shape=jax.ShapeDtypeStruct((n, SSD_INNER), BF16),
        compiler_params=pltpu.CompilerParams(vmem_limit_bytes=VMEM_LIMIT),
        name="ssd_post_decode",
    )(y.reshape(n, SSD_INNER), xs, z, d_e, gn)
    return o, h_new.reshape(n, SSD_HEADS, SSD_HEAD_DIM, SSD_STATE), nbuf.transpose(1, 0, 2)


def _pack_w_in(w):
    dt0 = C_GT
    dt = jnp.pad(w[:, dt0:dt0 + SSD_HEADS], ((0, 0), (0, LANES - SSD_HEADS)))
    return jnp.concatenate([w[:, :dt0], w[:, dt0 + SSD_HEADS:], dt], axis=1).astype(BF16)


def kernel(x_prompt, x_sample, c_prompt, c_sample, cache_k, cache_v, state_lru_h, state_lru_conv, state_ssd_h, state_ssd_conv, w_mod, b_mod, g_norm_mix, g_norm_ffn, w_in, g_q, g_k, attn_sinks, lru_conv_w, lru_conv_b, lru_w_a, lru_b_a, lru_w_x, lru_b_x, lru_lambda, ssd_conv_w, ssd_conv_b, ssd_dt_bias, ssd_a_log, ssd_d, g_ssd_norm, w_br_attn, w_br_lru, w_br_ssd, w_out, ffn_w_gate, ffn_w_up, ffn_w_down, moe_w_router, moe_b_router, moe_w_gate, moe_w_up, moe_w_down):
    bp, seq = x_prompt.shape[:2]
    ns = x_sample.shape[0]
    assert x_sample.shape[1] == 1
    mp = bp * seq
    tm_p, tm_s = 512, ns
    lru_rows = 256

    mod = _modulation(jnp.concatenate([c_prompt, c_sample], axis=0), w_mod, b_mod)
    xp = x_prompt.reshape(mp, D_MODEL)
    xs = x_sample.reshape(ns, D_MODEL)
    outs_p = [[] for _ in range(6)]
    outs_s = [[] for _ in range(6)]
    row = lambda v: v.astype(F32)[None, :]

    for l in range(DEPTH):
        mod_p = mod[l, :bp].reshape(bp, 6, 1, D_MODEL)
        mod_s = mod[l, bp:].reshape(ns, 6, D_MODEL)
        mp_ = [mod_p[:, k] for k in range(6)]
        ms_ = [mod_s[:, k] for k in range(6)]
        w_packed = _pack_w_in(w_in[l])
        wa, wx, negsp = _lru_weights(lru_w_a[l], lru_w_x[l], lru_lambda[l])
        lru_args = (lru_conv_w[l], row(lru_conv_b[l]), wa, row(lru_b_a[l]), wx, row(lru_b_x[l]), negsp)
        dtb, aneg, rexp, d_e = _ssd_consts(ssd_dt_bias[l], ssd_a_log[l], ssd_d[l])
        gn_ssd = row(g_ssd_norm[l])
        br = (w_br_attn[l].astype(BF16), w_br_lru[l].astype(BF16), w_br_ssd[l].astype(BF16), w_out[l].astype(BF16))
        is_moe = l % 2 == 1
        j = l // 2
        if is_moe:
            router = (jnp.pad(moe_w_router[j].astype(F32), ((0, 0), (0, LANES - N_EXPERTS))),
                      jnp.pad(moe_b_router[j].astype(F32), (0, LANES - N_EXPERTS))[None, :])
            wg, wu, wd = moe_w_gate[j].astype(BF16), moe_w_up[j].astype(BF16), moe_w_down[j].astype(BF16)
        else:
            router = None
            wg, wu, wd = (ffn_w_gate[j].astype(BF16)[None], ffn_w_up[j].astype(BF16)[None],
                          ffn_w_down[j].astype(BF16)[None])

        qkv, lx, lg, z, xbc, gt, dtr = _in_proj(xp, row(g_norm_mix[l]), mp_[1], mp_[0], w_packed, False, seq, tm_p)
        o_attn, k_win, v_win = _attention_prompt(qkv, attn_sinks[l].astype(F32), g_q[l], g_k[l], bp, seq)
        o_lru, lru_h, lru_buf = _lru_prompt(lx, lg, *lru_args, bp, seq, lru_rows)
        o_ssd, ssd_st, ssd_buf = _ssd_prompt(xbc, dtr, z, ssd_conv_w[l], row(ssd_conv_b[l]), dtb, aneg, rexp,
                                             d_e, gn_ssd, bp, seq)
        merged = _merge(o_attn, o_lru, o_ssd, gt, xp, mp_[2], mp_[4], mp_[3], row(g_norm_ffn[l]), *br,
                        router, False, seq, tm_p)
        if is_moe:
            x1, h2, logits = merged
            xp = _moe(h2, logits, x1, mp_[5], wg, wu, wd, False, seq, tm_p, 256)
        else:
            x1, h2 = merged
            nt = mp // tm_p
            xp = _ffn(h2, wg, wu, wd, jnp.zeros((nt,), I32), jnp.full((1,), nt, I32), tm_p,
                      residual=x1, gate2=mp_[5], per_row=False, seq_len=seq)
        for acc, t in zip(outs_p, (k_win.reshape(bp, WINDOW, N_KV_HEADS, HEAD_DIM),
                                   v_win.reshape(bp, WINDOW, N_KV_HEADS, HEAD_DIM),
                                   lru_h.reshape(bp, LRU_WIDTH), lru_buf, _ssd_state_to_hpn(ssd_st), ssd_buf)):
            acc.append(t)

        qkv, lx, lg, z, xbc, gt, dtr = _in_proj(xs, row(g_norm_mix[l]), ms_[1], ms_[0], w_packed, True, 1, tm_s)
        o_attn, k_win, v_win = _attention_decode(qkv, cache_k[l], cache_v[l], attn_sinks[l], g_q[l], g_k[l], 32)
        o_lru, lru_h, lru_buf = _lru_decode(lx, lg, state_lru_conv[l], state_lru_h[l], *lru_args)
        o_ssd, ssd_h, ssd_buf = _ssd_decode(xbc, dtr, z, state_ssd_conv[l], state_ssd_h[l], ssd_conv_w[l],
                                            row(ssd_conv_b[l]), dtb, aneg, rexp, d_e, gn_ssd, 16)
        merged = _merge(o_attn, o_lru, o_ssd, gt, xs, ms_[2], ms_[4], ms_[3], row(g_norm_ffn[l]), *br,
                        router, True, 1, tm_s)
        if is_moe:
            x1, h2, logits = merged
            xs = _moe(h2, logits, x1, ms_[5], wg, wu, wd, True, 1, tm_s, ns)
        else:
            x1, h2 = merged
            xs = _ffn(h2, wg, wu, wd, jnp.zeros((1,), I32), jnp.ones((1,), I32), tm_s,
                      residual=x1, gate2=ms_[5], per_row=True, seq_len=1)
        for acc, t in zip(outs_s, (k_win.reshape(ns, WINDOW, N_KV_HEADS, HEAD_DIM),
                                   v_win.reshape(ns, WINDOW, N_KV_HEADS, HEAD_DIM),
                                   lru_h, lru_buf, ssd_h, ssd_buf)):
            acc.append(t)

    return (xp.reshape(bp, seq, D_MODEL), xs.reshape(ns, 1, D_MODEL),
            *[jnp.stack(a) for a in outs_p], *[jnp.stack(a) for a in outs_s])
```

```python
import functools
import math

import numpy as np
import jax
import jax.numpy as jnp
from jax import lax
from jax.experimental import pallas as pl
from jax.experimental.pallas import tpu as pltpu

F32 = jnp.float32
BF16 = jnp.bfloat16
I32 = jnp.int32

D_MODEL = 1024
DEPTH = 4
PAST_LEN = 8192
N_Q_HEADS = 8
N_KV_HEADS = 2
HEAD_DIM = 64
GQA_GROUP = N_Q_HEADS // N_KV_HEADS
WINDOW = 128
ATTN_BLOCK = 128
ROT_DIM = HEAD_DIM // 4
ROPE_THETA = 500000.0
ATTN_WIDTH = N_Q_HEADS * HEAD_DIM
KV_WIDTH = N_KV_HEADS * HEAD_DIM
LRU_WIDTH = 512
LRU_BLOCKS = 8
LRU_BLOCK_DIM = LRU_WIDTH // LRU_BLOCKS
LRU_C = 8.0
CONV_K = 4
SSD_HEADS = 8
SSD_HEAD_DIM = 64
SSD_INNER = SSD_HEADS * SSD_HEAD_DIM
SSD_GROUPS = 2
SSD_STATE = 64
SSD_CHUNK = 128
SSD_CONV_DIM = SSD_INNER + 2 * SSD_GROUPS * SSD_STATE
D_FF = 2816
N_EXPERTS = 8
TOP_K = 2
EPS = 1e-6

LANES = 128
SUBLANES = 8
QKV_W = ATTN_WIDTH + 2 * KV_WIDTH
GATES_W = 3 * D_MODEL
C_QKV = 0
C_LX = C_QKV + QKV_W
C_LG = C_LX + LRU_WIDTH
C_Z = C_LG + LRU_WIDTH
C_XBC = C_Z + SSD_INNER
C_GT = C_XBC + SSD_CONV_DIM
C_DT = C_GT + GATES_W
IN_PACKED = C_DT + LANES
FF_CHUNK = D_FF // 2
VMEM_LIMIT = 56 * 1024 * 1024


def _params(sem=None):
    if sem is None:
        return pltpu.CompilerParams(vmem_limit_bytes=VMEM_LIMIT)
    return pltpu.CompilerParams(dimension_semantics=sem, vmem_limit_bytes=VMEM_LIMIT)


def _layer_spec(shape, layer, ngrid):
    zeros = (0,) * len(shape)
    return pl.BlockSpec((None,) + tuple(shape), lambda *g: (layer,) + zeros)


def _sigmoid(x):
    return 0.5 * jnp.tanh(0.5 * x) + 0.5


def _silu(x):
    return x * _sigmoid(x)


def _softplus(x):
    return jnp.maximum(x, 0.0) + jnp.log(1.0 + jnp.exp(-jnp.abs(x)))


def _gelu_tanh(x):
    return 0.5 * x * (1.0 + jnp.tanh(math.sqrt(2.0 / math.pi) * (x + 0.044715 * (x * x * x))))


def _split3(a):
    a1 = a.astype(BF16)
    r1 = a - a1.astype(F32)
    a2 = r1.astype(BF16)
    a3 = (r1 - a2.astype(F32)).astype(BF16)
    return a1, a2, a3


def _dot_exact_rhs(a, m):
    a1, a2, a3 = _split3(a)
    d = lambda t: jnp.dot(t, m, preferred_element_type=F32)
    return d(a1) + d(a2) + d(a3)


def _dot_exact_lhs(m, a):
    a1, a2, a3 = _split3(a)
    d = lambda t: jnp.dot(m, t, preferred_element_type=F32)
    return d(a1) + d(a2) + d(a3)


def _rms(x, g):
    return x * lax.rsqrt(jnp.mean(x * x, axis=-1, keepdims=True) + EPS) * g


def _mod_kernel(c_ref, w_ref, b_ref, o_ref):
    s = _silu(c_ref[...]).astype(BF16)
    o_ref[...] = jnp.dot(s, w_ref[...].astype(BF16), preferred_element_type=F32) + b_ref[...]


def _modulation(c, w_mod, b_mod):
    rows = c.shape[0]
    tn = 1536
    nt = (6 * D_MODEL) // tn
    return pl.pallas_call(
        _mod_kernel,
        grid=(DEPTH, nt),
        in_specs=[
            pl.BlockSpec((rows, D_MODEL), lambda l, j: (0, 0)),
            pl.BlockSpec((None, D_MODEL, tn), lambda l, j: (l, 0, j)),
            pl.BlockSpec((None, 1, tn), lambda l, j: (l, 0, j)),
        ],
        out_specs=pl.BlockSpec((None, rows, tn), lambda l, j: (l, 0, j)),
        out_shape=jax.ShapeDtypeStruct((DEPTH, rows, 6 * D_MODEL), F32),
        compiler_params=_params(("arbitrary", "arbitrary")),
        name="modulation",
    )(c, w_mod, b_mod.reshape(DEPTH, 1, 6 * D_MODEL))


class _Mod:
    def __init__(self, arr, per_row, batch, seq_len):
        self.arr, self.per_row, self.batch, self.seq_len = arr, per_row, batch, seq_len

    def spec(self, layer, k, tm, nprefetch=0):
        if self.per_row:
            return pl.BlockSpec((None, tm, D_MODEL), lambda i, *_: (layer, i, k))
        b, s = self.batch, self.seq_len
        return pl.BlockSpec((None, 1, D_MODEL), lambda i, *_: ((layer * b + (i * tm) // s) * 6 + k, 0, 0))


def _in_kernel(x_ref, g_ref, sc_ref, sh_ref, w_ref,
               qkv_ref, lx_ref, lg_ref, z_ref, xbc_ref, gt_ref, dt_ref):
    h = (_rms(x_ref[...], g_ref[...]) * (1.0 + sc_ref[...]) + sh_ref[...]).astype(BF16)

    def mm(lo, hi):
        return jnp.dot(h, w_ref[:, lo:hi], preferred_element_type=F32)

    qkv_ref[...] = mm(C_QKV, C_LX).astype(BF16)
    lx_ref[...] = mm(C_LX, C_LG)
    lg_ref[...] = mm(C_LG, C_Z).astype(BF16)
    z_ref[...] = mm(C_Z, C_XBC).astype(BF16)
    xbc_ref[...] = mm(C_XBC, C_GT)
    gt_ref[...] = mm(C_GT, C_DT).astype(BF16)
    dt_ref[...] = mm(C_DT, IN_PACKED)


def _in_proj(x, layer, g_all, mod, w_all, tm):
    m = x.shape[0]
    row = lambda w: pl.BlockSpec((tm, w), lambda i: (i, 0))
    return pl.pallas_call(
        _in_kernel,
        grid=(m // tm,),
        in_specs=[row(D_MODEL), _layer_spec((1, D_MODEL), layer, 1), mod.spec(layer, 1, tm), mod.spec(layer, 0, tm),
                  _layer_spec((D_MODEL, IN_PACKED), layer, 1)],
        out_specs=[row(QKV_W), row(LRU_WIDTH), row(LRU_WIDTH), row(SSD_INNER),
                   row(SSD_CONV_DIM), row(GATES_W), row(LANES)],
        out_shape=[jax.ShapeDtypeStruct((m, QKV_W), BF16),
                   jax.ShapeDtypeStruct((m, LRU_WIDTH), F32),
                   jax.ShapeDtypeStruct((m, LRU_WIDTH), BF16),
                   jax.ShapeDtypeStruct((m, SSD_INNER), BF16),
                   jax.ShapeDtypeStruct((m, SSD_CONV_DIM), F32),
                   jax.ShapeDtypeStruct((m, GATES_W), BF16),
                   jax.ShapeDtypeStruct((m, LANES), F32)],
        compiler_params=_params(("arbitrary",)),
        name="in_proj",
    )(x, g_all, mod.arr, mod.arr, w_all)


def _head_norm_rope(t, g2, e2, cos_t, sin_lo, sin_hi):
    ms = _dot_exact_rhs(t * t, e2)
    tn = t * lax.rsqrt(ms + EPS) * g2
    up = pltpu.roll(tn, LANES - ROT_DIM // 2, 1)
    dn = pltpu.roll(tn, ROT_DIM // 2, 1)
    return tn * cos_t + up * sin_lo + dn * sin_hi


def _rope_tables(pos):
    half = ROT_DIM // 2
    inv_freq = ROPE_THETA ** (-jnp.arange(0, ROT_DIM, 2, dtype=F32) / ROT_DIM)
    ang = pos.astype(F32)[:, None] * inv_freq[None, :]
    cos, sin = jnp.cos(ang), jnp.sin(ang)
    n = pos.shape[0]
    ones = jnp.ones((n, HEAD_DIM - ROT_DIM), F32)
    zeros = jnp.zeros((n, HEAD_DIM - ROT_DIM), F32)
    zh = jnp.zeros((n, half), F32)
    cos_t = jnp.concatenate([cos, cos, ones], axis=1)
    sin_lo = jnp.concatenate([-sin, zh, zeros], axis=1)
    sin_hi = jnp.concatenate([zh, sin, zeros], axis=1)
    rep = lambda t: jnp.concatenate([t, t], axis=1)
    return rep(cos_t), rep(sin_lo), rep(sin_hi)


def _head_avg_matrix():
    r = np.arange(LANES)
    return jnp.asarray((r[:, None] // HEAD_DIM == r[None, :] // HEAD_DIM).astype(np.float32) / HEAD_DIM, dtype=BF16)


def _attn_kernel(sink_ref, qkv_ref, cos_ref, slo_ref, shi_ref, gq_ref, gk_ref, e2_ref,
                 o_ref, kwin_ref, vwin_ref, kprev, vprev, *, layer):
    n = pl.program_id(1)
    blk = ATTN_BLOCK

    @pl.when(n == 0)
    def _():
        kprev[...] = jnp.zeros_like(kprev)
        vprev[...] = jnp.zeros_like(vprev)

    cos_t, slo, shi, e2 = cos_ref[...], slo_ref[...], shi_ref[...], e2_ref[...]
    qkv = qkv_ref[...].astype(F32)
    k_cur = _head_norm_rope(qkv[:, ATTN_WIDTH:ATTN_WIDTH + KV_WIDTH], gk_ref[...], e2, cos_t, slo, shi)
    v_cur = qkv[:, ATTN_WIDTH + KV_WIDTH:]
    k_all = jnp.concatenate([kprev[...], k_cur], axis=0)
    v_all = jnp.concatenate([vprev[...], v_cur], axis=0)
    lane = lax.broadcasted_iota(I32, k_all.shape, 1)
    left = lane < HEAD_DIM
    ones_col = jnp.where(lane == HEAD_DIM, 1.0, 0.0)
    k_sw = pltpu.roll(k_all, HEAD_DIM, 1)
    v_sw = pltpu.roll(v_all, HEAD_DIM, 1)
    k_g = (jnp.where(left, k_all, 0.0).astype(BF16), jnp.where(left, k_sw, 0.0).astype(BF16))
    v_g = (jnp.where(left, v_all, ones_col).astype(BF16), jnp.where(left, v_sw, ones_col).astype(BF16))

    rows = GQA_GROUP * blk
    r = lax.broadcasted_iota(I32, (rows, 2 * blk), 0) & (blk - 1)
    c = lax.broadcasted_iota(I32, (rows, 2 * blk), 1)
    c_min = jnp.where(n == 0, blk, 0)
    valid = (c >= r) & (c <= r + WINDOW) & (c >= c_min)
    head_in_group = lax.shift_right_logical(lax.broadcasted_iota(I32, (rows, 1), 0), blk.bit_length() - 1)
    left_o = lax.broadcasted_iota(I32, (rows, LANES), 1) < HEAD_DIM

    for g in range(N_KV_HEADS):
        stack = []
        for j in (2 * g, 2 * g + 1):
            q_slab = _head_norm_rope(qkv[:, j * LANES:(j + 1) * LANES], gq_ref[...], e2, cos_t, slo, shi)
            q_slab = q_slab * (HEAD_DIM ** -0.5)
            stack += [q_slab, pltpu.roll(q_slab, HEAD_DIM, 1)]
        q4 = jnp.concatenate(stack, axis=0).astype(BF16)
        s = lax.dot_general(q4, k_g[g], (((1,), (1,)), ((), ())), preferred_element_type=F32)
        s = jnp.where(valid, s, -jnp.inf)
        sink = jnp.zeros((rows, 1), F32)
        for t in range(GQA_GROUP):
            sink = jnp.where(head_in_group == t, sink_ref[layer, GQA_GROUP * g + t], sink)
        m = jnp.maximum(jnp.max(s, axis=-1, keepdims=True), sink)
        p = jnp.exp(s - m).astype(BF16)
        o = jnp.dot(p, v_g[g], preferred_element_type=F32)
        den = o[:, HEAD_DIM:HEAD_DIM + 1] + jnp.exp(sink - m)
        o = jnp.where(left_o, o * (1.0 / den), 0.0)
        for jj in range(2):
            even = o[(2 * jj) * blk:(2 * jj + 1) * blk]
            odd = o[(2 * jj + 1) * blk:(2 * jj + 2) * blk]
            slab = even + pltpu.roll(odd, HEAD_DIM, 1)
            o_ref[:, (2 * g + jj) * LANES:(2 * g + jj + 1) * LANES] = slab.astype(BF16)

    kprev[...] = k_cur
    vprev[...] = v_cur
    kwin_ref[...] = k_cur
    vwin_ref[...] = v_cur


def _attention_prompt(qkv, layer, sinks_all, gq_all, gk_all, tables, e2, batch, seq_len):
    nb = seq_len // ATTN_BLOCK
    tab = pl.BlockSpec((ATTN_BLOCK, LANES), lambda b, n: (n, 0))
    win = pl.BlockSpec((None, ATTN_BLOCK, KV_WIDTH), lambda b, n: (b, 0, 0))
    return pl.pallas_call(
        functools.partial(_attn_kernel, layer=layer),
        grid=(batch, nb),
        in_specs=[pl.BlockSpec(memory_space=pltpu.SMEM),
                  pl.BlockSpec((ATTN_BLOCK, QKV_W), lambda b, n: (b * nb + n, 0)),
                  tab, tab, tab, _layer_spec((1, LANES), layer, 2), _layer_spec((1, LANES), layer, 2),
                  pl.BlockSpec((LANES, LANES), lambda b, n: (0, 0))],
        out_specs=[pl.BlockSpec((ATTN_BLOCK, ATTN_WIDTH), lambda b, n: (b * nb + n, 0)), win, win],
        out_shape=[jax.ShapeDtypeStruct((batch * seq_len, ATTN_WIDTH), BF16),
                   jax.ShapeDtypeStruct((batch, ATTN_BLOCK, KV_WIDTH), F32),
                   jax.ShapeDtypeStruct((batch, ATTN_BLOCK, KV_WIDTH), F32)],
        scratch_shapes=[pltpu.VMEM((ATTN_BLOCK, KV_WIDTH), F32), pltpu.VMEM((ATTN_BLOCK, KV_WIDTH), F32)],
        compiler_params=_params(("arbitrary", "arbitrary")),
        name="attn_prompt",
    )(sinks_all, qkv, *tables, gq_all, gk_all, e2)


def _conv_chunk(xbuf, x, w_ref, b_ref, rows):
    xbuf[SUBLANES:SUBLANES + rows, :] = x
    y = x * w_ref[CONV_K - 1:CONV_K, :] + b_ref[...]
    for j in range(1, CONV_K):
        y = y + xbuf[pl.ds(SUBLANES - j, rows), :] * w_ref[CONV_K - 1 - j:CONV_K - j, :]
    return y


def _lru_gates(xc, wa_ref, ba_ref, wx_ref, bx_ref, negsp_ref):
    xb = xc.astype(BF16)
    ra, ri = [], []
    for s in range(LRU_WIDTH // LANES):
        sl = slice(s * LANES, (s + 1) * LANES)
        ra.append(jnp.dot(xb[:, sl], wa_ref[s], preferred_element_type=F32))
        ri.append(jnp.dot(xb[:, sl], wx_ref[s], preferred_element_type=F32))
    r = _sigmoid(jnp.concatenate(ra, axis=1) + ba_ref[...])
    i = _sigmoid(jnp.concatenate(ri, axis=1) + bx_ref[...])
    log_a = r * negsp_ref[...]
    a = jnp.exp(log_a)
    u = jnp.sqrt(1.0 - a * a) * (i * xc)
    return a, u


def _lru_kernel(x_ref, gate_ref, cw_ref, cb_ref, wa_ref, ba_ref, wx_ref, bx_ref, negsp_ref,
                o_ref, h_ref, buf_ref, xbuf, hcar, *, rows):
    n = pl.program_id(1)

    @pl.when(n == 0)
    def _():
        xbuf[0:SUBLANES, :] = jnp.zeros((SUBLANES, LRU_WIDTH), F32)
        hcar[...] = jnp.zeros_like(hcar)

    x = x_ref[...]
    xc = _conv_chunk(xbuf, x, cw_ref, cb_ref, rows)
    a, u = _lru_gates(xc, wa_ref, ba_ref, wx_ref, bx_ref, negsp_ref)

    t = lax.broadcasted_iota(I32, a.shape, 0)
    d = 1
    while d < rows:
        keep = t >= d
        a_s = jnp.where(keep, pltpu.roll(a, d, 0), 1.0)
        u_s = jnp.where(keep, pltpu.roll(u, d, 0), 0.0)
        u = a * u_s + u
        a = a * a_s
        d *= 2
    h = a * hcar[...] + u
    o_ref[...] = (h * _gelu_tanh(gate_ref[...].astype(F32))).astype(BF16)
    hcar[...] = h[rows - 1:rows, :]
    h_ref[...] = h[rows - 1:rows, :]
    buf_ref[...] = x[rows - (CONV_K - 1):rows, :]
    xbuf[0:SUBLANES, :] = x[rows - SUBLANES:rows, :]


def _lru_weight_stack(lru_w_a, lru_w_x, lru_lambda):
    def pair(w):
        w = w.astype(BF16).reshape(DEPTH, LRU_BLOCKS // 2, 2, LRU_BLOCK_DIM, LRU_BLOCK_DIM)
        z = jnp.zeros((DEPTH, LRU_BLOCKS // 2, LRU_BLOCK_DIM, LRU_BLOCK_DIM), BF16)
        top = jnp.concatenate([w[:, :, 0], z], axis=3)
        bot = jnp.concatenate([z, w[:, :, 1]], axis=3)
        return jnp.concatenate([top, bot], axis=2)
    return pair(lru_w_a), pair(lru_w_x), (-LRU_C * jax.nn.softplus(-lru_lambda.astype(F32)))[:, None, :]


def _lru_param_specs(layer, ngrid):
    ls = lambda shape: _layer_spec(shape, layer, ngrid)
    nb = LRU_WIDTH // LANES
    return [ls((CONV_K, LRU_WIDTH)), ls((1, LRU_WIDTH)), ls((nb, LANES, LANES)), ls((1, LRU_WIDTH)),
            ls((nb, LANES, LANES)), ls((1, LRU_WIDTH)), ls((1, LRU_WIDTH))]


def _lru_prompt(lx, lgate, layer, lru_params, batch, seq_len, rows):
    nc = seq_len // rows
    tok = lambda w: pl.BlockSpec((rows, w), lambda b, n: (b * nc + n, 0))
    return pl.pallas_call(
        functools.partial(_lru_kernel, rows=rows),
        grid=(batch, nc),
        in_specs=[tok(LRU_WIDTH), tok(LRU_WIDTH)] + _lru_param_specs(layer, 2),
        out_specs=[tok(LRU_WIDTH),
                   pl.BlockSpec((None, 1, LRU_WIDTH), lambda b, n: (b, 0, 0)),
                   pl.BlockSpec((None, CONV_K - 1, LRU_WIDTH), lambda b, n: (b, 0, 0))],
        out_shape=[jax.ShapeDtypeStruct((batch * seq_len, LRU_WIDTH), BF16),
                   jax.ShapeDtypeStruct((batch, 1, LRU_WIDTH), F32),
                   jax.ShapeDtypeStruct((batch, CONV_K - 1, LRU_WIDTH), F32)],
        scratch_shapes=[pltpu.VMEM((rows + SUBLANES, LRU_WIDTH), F32), pltpu.VMEM((1, LRU_WIDTH), F32)],
        compiler_params=_params(("arbitrary", "arbitrary")),
        name="lru_prompt",
    )(lx, lgate, *lru_params)


def _ssd_gate_norm(y, z, gn):
    u = y * _silu(z)
    gw = SSD_INNER // SSD_GROUPS
    parts = []
    for g in range(SSD_GROUPS):
        ug = u[:, g * gw:(g + 1) * gw]
        parts.append(ug * lax.rsqrt(jnp.mean(ug * ug, axis=-1, keepdims=True) + EPS))
    return jnp.concatenate(parts, axis=1) * gn


def _ssd_kernel(xbc_ref, dt_ref, z_ref, cw_ref, cb_ref, dtb_ref, aneg_ref, d_ref, gn_ref, rexp_ref,
                o_ref, st_ref, buf_ref, xbuf, state):
    n = pl.program_id(1)
    q = SSD_CHUNK
    gw = SSD_INNER // SSD_GROUPS

    @pl.when(n == 0)
    def _():
        xbuf[0:SUBLANES, :] = jnp.zeros((SUBLANES, SSD_CONV_DIM), F32)
        state[...] = jnp.zeros_like(state)

    xbc = xbc_ref[...]
    xc = _silu(_conv_chunk(xbuf, xbc, cw_ref, cb_ref, q))
    xs = xc[:, :SSD_INNER]
    bm = xc[:, SSD_INNER:SSD_INNER + LANES]
    cm = xc[:, SSD_INNER + LANES:]

    rexp = rexp_ref[...]
    dt = _softplus(dt_ref[...] + dtb_ref[...])
    dta = dt * aneg_ref[...]
    li = lax.broadcasted_iota(I32, (q, q), 0)
    si = lax.broadcasted_iota(I32, (q, q), 1)
    causal = li >= si
    tri = jnp.where(causal, 1.0, 0.0).astype(BF16)
    a_cs = _dot_exact_lhs(tri, dta)
    a_cs_t = a_cs.T
    dt_e = _dot_exact_rhs(dt, rexp)
    a_e = _dot_exact_rhs(a_cs, rexp)
    a_last = a_e[q - 1:q, :]

    xdt = xs * dt_e
    lane = lax.broadcasted_iota(I32, (q, LANES), 1)
    left = lane < SSD_STATE
    zero = jnp.zeros((q, LANES), F32)
    cmb = cm.astype(BF16)
    bmb = bm.astype(BF16)
    cm_g = (jnp.where(left, cm, zero).astype(BF16), jnp.where(left, zero, cm).astype(BF16))
    scores = [lax.dot_general(cm_g[g], bmb, (((1,), (1,)), ((), ())), preferred_element_type=F32)
              for g in range(SSD_GROUPS)]

    y_off = jnp.dot(cmb, state[...].astype(BF16), preferred_element_type=F32) * jnp.exp(a_e)

    ys = []
    for j in range(SSD_INNER // LANES):
        xs_slab = xdt[:, j * LANES:(j + 1) * LANES]
        x_side = (jnp.where(left, xs_slab, zero).astype(BF16), jnp.where(left, zero, xs_slab).astype(BF16))
        acc = y_off[:, j * LANES:(j + 1) * LANES]
        for side in range(2):
            h = 2 * j + side
            g = h // (SSD_HEADS // SSD_GROUPS)
            col = a_e[:, h * SSD_HEAD_DIM:h * SSD_HEAD_DIM + 1]
            row = a_cs_t[h:h + 1, :]
            decay = jnp.exp(jnp.where(causal, col - row, -jnp.inf))
            acc = acc + jnp.dot((scores[g] * decay).astype(BF16), x_side[side], preferred_element_type=F32)
        ys.append(acc + d_ref[:, j * LANES:(j + 1) * LANES] * xs[:, j * LANES:(j + 1) * LANES])
    y = jnp.concatenate(ys, axis=1)
    o_ref[...] = _ssd_gate_norm(y, z_ref[...].astype(F32), gn_ref[...]).astype(BF16)

    x_end = (xdt * jnp.exp(a_last - a_e)).astype(BF16)
    upd = jnp.dot(bm.T.astype(BF16), x_end, preferred_element_type=F32)
    rg = lax.broadcasted_iota(I32, (LANES, SSD_INNER), 0) // SSD_STATE
    cg = lax.broadcasted_iota(I32, (LANES, SSD_INNER), 1) // gw
    new_state = jnp.where(rg == cg, jnp.exp(a_last) * state[...] + upd, 0.0)
    state[...] = new_state
    st_ref[...] = new_state
    buf_ref[...] = xbc[q - (CONV_K - 1):q, :]
    xbuf[0:SUBLANES, :] = xbc[q - SUBLANES:q, :]


def _ssd_param_stack(ssd_conv_w, ssd_conv_b, ssd_dt_bias, ssd_a_log, ssd_d, g_ssd_norm):
    pad = lambda v: jnp.pad(v.astype(F32), ((0, 0), (0, LANES - SSD_HEADS)))[:, None, :]
    return (ssd_conv_w.astype(F32), ssd_conv_b.astype(F32)[:, None, :], pad(ssd_dt_bias),
            pad(-jnp.exp(ssd_a_log.astype(F32))),
            jnp.repeat(ssd_d.astype(F32), SSD_HEAD_DIM, axis=1)[:, None, :],
            g_ssd_norm.astype(F32)[:, None, :])


def _ssd_param_specs(layer, ngrid):
    ls = lambda shape: _layer_spec(shape, layer, ngrid)
    return [ls((CONV_K, SSD_CONV_DIM)), ls((1, SSD_CONV_DIM)), ls((1, LANES)), ls((1, LANES)),
            ls((1, SSD_INNER)), ls((1, SSD_INNER))]


def _head_expand_matrix():
    h = np.arange(LANES)[:, None]
    c = np.arange(SSD_INNER)[None, :]
    return jnp.asarray((h == c // SSD_HEAD_DIM).astype(np.float32), dtype=BF16)


def _ssd_prompt(xbc, dtr, z, layer, ssd_params, rexp, batch, seq_len):
    nc = seq_len // SSD_CHUNK
    tok = lambda w: pl.BlockSpec((SSD_CHUNK, w), lambda b, n: (b * nc + n, 0))
    return pl.pallas_call(
        _ssd_kernel,
        grid=(batch, nc),
        in_specs=[tok(SSD_CONV_DIM), tok(LANES), tok(SSD_INNER)] + _ssd_param_specs(layer, 2)
                 + [pl.BlockSpec((LANES, SSD_INNER), lambda b, n: (0, 0))],
        out_specs=[tok(SSD_INNER),
                   pl.BlockSpec((None, LANES, SSD_INNER), lambda b, n: (b, 0, 0)),
                   pl.BlockSpec((None, CONV_K - 1, SSD_CONV_DIM), lambda b, n: (b, 0, 0))],
        out_shape=[jax.ShapeDtypeStruct((batch * seq_len, SSD_INNER), BF16),
                   jax.ShapeDtypeStruct((batch, LANES, SSD_INNER), F32),
                   jax.ShapeDtypeStruct((batch, CONV_K - 1, SSD_CONV_DIM), F32)],
        scratch_shapes=[pltpu.VMEM((SSD_CHUNK + SUBLANES, SSD_CONV_DIM), F32), pltpu.VMEM((LANES, SSD_INNER), F32)],
        compiler_params=_params(("arbitrary", "arbitrary")),
        name="ssd_prompt",
    )(xbc, dtr, z, *ssd_params, rexp)


def _ssd_state_to_hpn(st):
    lead = st.shape[:-2]
    hpg = SSD_HEADS // SSD_GROUPS
    gw = SSD_INNER // SSD_GROUPS
    parts = []
    for g in range(SSD_GROUPS):
        blk = st[..., g * SSD_STATE:(g + 1) * SSD_STATE, g * gw:(g + 1) * gw]
        blk = blk.reshape(lead + (SSD_STATE, hpg, SSD_HEAD_DIM))
        parts.append(jnp.moveaxis(blk, -3, -1))
    return jnp.concatenate(parts, axis=-3)


def _merge_kernel(*refs, with_router):
    (oa_ref, ol_ref, os_ref, gt_ref, x_ref, g1_ref, sc2_ref, sh2_ref, gn_ref,
     wa_ref, wl_ref, ws_ref, wo_ref) = refs[:13]
    if with_router:
        wr_ref, br_ref, x1_ref, h2_ref, lg_ref = refs[13:]
    else:
        x1_ref, h2_ref = refs[13:]
    d = D_MODEL
    mm = lambda a, w: jnp.dot(a[...], w[...], preferred_element_type=F32)
    merged = (_sigmoid(gt_ref[:, 0:d].astype(F32)) * mm(oa_ref, wa_ref)
              + _sigmoid(gt_ref[:, d:2 * d].astype(F32)) * mm(ol_ref, wl_ref)
              + _sigmoid(gt_ref[:, 2 * d:3 * d].astype(F32)) * mm(os_ref, ws_ref))
    mix = jnp.dot(merged.astype(BF16), wo_ref[...], preferred_element_type=F32)
    x1 = x_ref[...] + g1_ref[...] * mix
    x1_ref[...] = x1
    h2 = _rms(x1, gn_ref[...]) * (1.0 + sc2_ref[...]) + sh2_ref[...]
    h2_ref[...] = h2.astype(h2_ref.dtype)
    if with_router:
        hi = h2.astype(BF16)
        lo = (h2 - hi.astype(F32)).astype(BF16)
        wr = wr_ref[...]
        t = jnp.dot(hi, wr, preferred_element_type=F32) + jnp.dot(lo, wr, preferred_element_type=F32)
        lg_ref[...] = t + pltpu.roll(t, LANES - N_EXPERTS, 1) + br_ref[...]


def _router_stack(moe_w_router, moe_b_router):
    w = moe_w_router.astype(F32)
    hi = w.astype(BF16)
    lo = (w - hi.astype(F32)).astype(BF16)
    n = w.shape[0]
    wr = jnp.concatenate([hi, lo, jnp.zeros((n, D_MODEL, LANES - 2 * N_EXPERTS), BF16)], axis=2)
    br = jnp.pad(moe_b_router.astype(F32), ((0, 0), (0, LANES - N_EXPERTS)))[:, None, :]
    return wr, br


def _merge(oa, ol, os_, gt, x, layer, mod, gn_all, wa_all, wl_all, ws_all, wo_all, router, tm):
    m = x.shape[0]
    row = lambda w: pl.BlockSpec((tm, w), lambda i: (i, 0))
    ls = lambda shape: _layer_spec(shape, layer, 1)
    in_specs = [row(ATTN_WIDTH), row(LRU_WIDTH), row(SSD_INNER), row(GATES_W), row(D_MODEL),
                mod.spec(layer, 2, tm), mod.spec(layer, 4, tm), mod.spec(layer, 3, tm),
                ls((1, D_MODEL)), ls((ATTN_WIDTH, D_MODEL)), ls((LRU_WIDTH, D_MODEL)),
                ls((SSD_INNER, D_MODEL)), ls((D_MODEL, D_MODEL))]
    args = [oa, ol, os_, gt, x, mod.arr, mod.arr, mod.arr, gn_all, wa_all, wl_all, ws_all, wo_all]
    out_specs = [row(D_MODEL), row(D_MODEL)]
    out_shape = [jax.ShapeDtypeStruct((m, D_MODEL), F32),
                 jax.ShapeDtypeStruct((m, D_MODEL), F32 if router is not None else BF16)]
    if router is not None:
        wr_all, br_all, j = router
        in_specs += [_layer_spec((D_MODEL, LANES), j, 1), _layer_spec((1, LANES), j, 1)]
        args += [wr_all, br_all]
        out_specs.append(row(LANES))
        out_shape.append(jax.ShapeDtypeStruct((m, LANES), F32))
    return pl.pallas_call(
        functools.partial(_merge_kernel, with_router=router is not None),
        grid=(m // tm,),
        in_specs=in_specs, out_specs=out_specs, out_shape=out_shape,
        compiler_params=_params(("arbitrary",)),
        name="merge",
    )(*args)


def _ffn_kernel(te_ref, nv_ref, *refs, fused_residual):
    if fused_residual:
        x_ref, wg_ref, wu_ref, wd_ref, res_ref, g2_ref, o_ref = refs
    else:
        x_ref, wg_ref, wu_ref, wd_ref, o_ref = refs
    i = pl.program_id(0)
    j = pl.program_id(1)

    @pl.when(j == 0)
    def _():
        o_ref[...] = jnp.zeros_like(o_ref)

    @pl.when(i < nv_ref[0])
    def _():
        h = x_ref[...].astype(BF16)
        a = jnp.dot(h, wg_ref[...], preferred_element_type=F32)
        b = jnp.dot(h, wu_ref[...], preferred_element_type=F32)
        t = (_silu(a) * b).astype(BF16)
        o_ref[...] += jnp.dot(t, wd_ref[...], preferred_element_type=F32)

    if fused_residual:
        @pl.when(j == pl.num_programs(1) - 1)
        def _():
            o_ref[...] = res_ref[...] + g2_ref[...] * o_ref[...]


def _ffn(x, wg, wu, wd, tile_expert, n_valid, tm, expert_base=0, residual=None, mod=None, layer=None):
    m = x.shape[0]
    nj = D_FF // FF_CHUNK
    last = nj - 1

    def jj(i, j, nv):
        return jnp.where(i < nv[0], j, last)

    in_specs = [pl.BlockSpec((tm, D_MODEL), lambda i, j, te, nv: (jnp.minimum(i, nv[0] - 1), 0)),
                pl.BlockSpec((None, D_MODEL, FF_CHUNK), lambda i, j, te, nv: (expert_base + te[i], 0, jj(i, j, nv))),
                pl.BlockSpec((None, D_MODEL, FF_CHUNK), lambda i, j, te, nv: (expert_base + te[i], 0, jj(i, j, nv))),
                pl.BlockSpec((None, FF_CHUNK, D_MODEL), lambda i, j, te, nv: (expert_base + te[i], jj(i, j, nv), 0))]
    args = [x, wg, wu, wd]
    fused = residual is not None
    if fused:
        in_specs += [pl.BlockSpec((tm, D_MODEL), lambda i, j, te, nv: (i, 0)), mod.spec(layer, 5, tm)]
        args += [residual, mod.arr]
    return pl.pallas_call(
        functools.partial(_ffn_kernel, fused_residual=fused),
        grid_spec=pltpu.PrefetchScalarGridSpec(
            num_scalar_prefetch=2, grid=(m // tm, nj), in_specs=in_specs,
            out_specs=pl.BlockSpec((tm, D_MODEL), lambda i, j, te, nv: (i, 0))),
        out_shape=jax.ShapeDtypeStruct((m, D_MODEL), F32),
        compiler_params=_params(("arbitrary", "arbitrary")),
        name="ffn",
    )(tile_expert, n_valid, *args)


def _route_kernel(lg_ref, p1_ref, p2_ref, w1_ref, w2_ref, te_ref, nv_ref, carry, e1s, e2s, r1s, r2s,
                  *, tb, tm, nblk):
    step = pl.program_id(0)

    @pl.when(step == 0)
    def _():
        carry[...] = jnp.zeros_like(carry)

    l8 = lg_ref[...].T[0:N_EXPERTS, :]
    e = lax.broadcasted_iota(I32, l8.shape, 0)
    m1 = jnp.max(l8, axis=0, keepdims=True)
    i1 = jnp.min(jnp.where(l8 == m1, e, N_EXPERTS), axis=0, keepdims=True)
    rest = jnp.where(e == i1, -jnp.inf, l8)
    m2 = jnp.max(rest, axis=0, keepdims=True)
    i2 = jnp.min(jnp.where(rest == m2, e, N_EXPERTS), axis=0, keepdims=True)
    t = jnp.exp(m2 - m1)
    w1 = 1.0 / (1.0 + t)
    sel1 = jnp.where(e == i1, 1.0, 0.0)
    sel2 = jnp.where(e == i2, 1.0, 0.0)
    sel = sel1 + sel2
    a = lax.broadcasted_iota(I32, (tb, tb), 0)
    b = lax.broadcasted_iota(I32, (tb, tb), 1)
    upper = jnp.where(a <= b, 1.0, 0.0).astype(BF16)
    incl = jnp.dot(sel.astype(BF16), upper, preferred_element_type=F32)
    rank = carry[:, 0:1] + incl - sel
    e1s[pl.ds(step, 1), :] = i1
    e2s[pl.ds(step, 1), :] = i2
    r1s[pl.ds(step, 1), :] = jnp.sum(sel1 * rank, axis=0, keepdims=True).astype(I32)
    r2s[pl.ds(step, 1), :] = jnp.sum(sel2 * rank, axis=0, keepdims=True).astype(I32)
    w1_ref[...] = w1
    w2_ref[...] = t * w1
    carry[...] = carry[...] + incl[:, tb - 1:tb]

    @pl.when(step == nblk - 1)
    def _():
        counts = carry[...].astype(I32)
        shift = tm.bit_length() - 1
        padded = lax.shift_right_logical(counts + (tm - 1), shift) << shift
        ex = lax.broadcasted_iota(I32, padded.shape, 0)
        ends = padded
        d = 1
        while d < N_EXPERTS:
            ends = ends + jnp.where(ex >= d, pltpu.roll(ends, d, 0), 0)
            d *= 2
        starts = ends - padded
        total = ends[N_EXPERTS - 1:N_EXPERTS, :]
        nvalid = lax.shift_right_logical(total, shift)
        tile0 = lax.broadcasted_iota(I32, padded.shape, 1) * tm
        te = jnp.sum(jnp.where(ends <= tile0, 1, 0), axis=0, keepdims=True)
        te_last = jnp.sum(jnp.where(ends <= total - tm, 1, 0), axis=0, keepdims=True)
        te_ref[...] = jnp.where(tile0[0:1, :] < total, te, te_last)
        nv_ref[...] = nvalid
        reps = tb // LANES
        e1, e2 = e1s[...], e2s[...]
        off1 = jnp.zeros(e1.shape, I32)
        off2 = jnp.zeros(e2.shape, I32)
        for k in range(N_EXPERTS):
            row = starts[k:k + 1, :]
            row = jnp.concatenate([row] * reps, axis=1) if reps > 1 else row
            off1 = jnp.where(e1 == k, row, off1)
            off2 = jnp.where(e2 == k, row, off2)
        p1_ref[...] = off1 + r1s[...]
        p2_ref[...] = off2 + r2s[...]


def _route(logits, tb, tm):
    m = logits.shape[0]
    nblk = m // tb
    assert tb % LANES == 0 and (m * TOP_K) // tm + N_EXPERTS <= LANES and tm & (tm - 1) == 0
    vec = pl.BlockSpec((1, tb), lambda i: (0, i))
    res = pl.BlockSpec((nblk, tb), lambda i: (0, 0))
    one = pl.BlockSpec((1, LANES), lambda i: (0, 0))
    return pl.pallas_call(
        functools.partial(_route_kernel, tb=tb, tm=tm, nblk=nblk),
        grid=(nblk,),
        in_specs=[pl.BlockSpec((tb, LANES), lambda i: (i, 0))],
        out_specs=[res, res, vec, vec, one, one],
        out_shape=[jax.ShapeDtypeStruct((nblk, tb), I32), jax.ShapeDtypeStruct((nblk, tb), I32),
                   jax.ShapeDtypeStruct((1, m), F32), jax.ShapeDtypeStruct((1, m), F32),
                   jax.ShapeDtypeStruct((1, LANES), I32), jax.ShapeDtypeStruct((1, LANES), I32)],
        scratch_shapes=[pltpu.VMEM((N_EXPERTS, LANES), F32)] + [pltpu.VMEM((nblk, tb), I32)] * 4,
        compiler_params=_params(("arbitrary",)),
        name="route",
    )(logits)


def _dispatch_kernel(p1_ref, p2_ref, x_ref, dst_in_ref, dst_ref, sem, *, tb):
    del dst_in_ref

    def start(r, c):
        src = x_ref.at[pl.ds(r, 1), :]
        pltpu.make_async_copy(src, dst_ref.at[pl.ds(p1_ref[0, r], 1), :], sem.at[0]).start()
        pltpu.make_async_copy(src, dst_ref.at[pl.ds(p2_ref[0, r], 1), :], sem.at[1]).start()
        return c

    lax.fori_loop(0, tb, start, 0, unroll=8)
    row0 = x_ref.at[pl.ds(0, 1), :]
    for _ in range(tb):
        pltpu.make_async_copy(row0, dst_ref.at[pl.ds(0, 1), :], sem.at[0]).wait()
        pltpu.make_async_copy(row0, dst_ref.at[pl.ds(0, 1), :], sem.at[1]).wait()


def _dispatch(x, pos1, pos2, dst, tb):
    m = x.shape[0]
    smem = pl.BlockSpec((None, 1, tb), lambda i: (i, 0, 0), memory_space=pltpu.SMEM)
    return pl.pallas_call(
        functools.partial(_dispatch_kernel, tb=tb),
        grid=(m // tb,),
        in_specs=[smem, smem, pl.BlockSpec((tb, D_MODEL), lambda i: (i, 0)),
                  pl.BlockSpec(memory_space=pl.ANY)],
        out_specs=pl.BlockSpec(memory_space=pl.ANY),
        out_shape=jax.ShapeDtypeStruct(dst.shape, F32),
        scratch_shapes=[pltpu.SemaphoreType.DMA((2,))],
        input_output_aliases={3: 0},
        compiler_params=_params(("arbitrary",)),
        name="dispatch",
    )(pos1.reshape(m // tb, 1, tb), pos2.reshape(m // tb, 1, tb), x, dst)


def _combine_kernel(p1_ref, p2_ref, y_ref, x1_ref, g2_ref, w1_ref, w2_ref, o_ref, buf1, buf2, sem, *, tb):
    def start(r, c):
        pltpu.make_async_copy(y_ref.at[pl.ds(p1_ref[0, r], 1), :], buf1.at[pl.ds(r, 1), :], sem.at[0]).start()
        pltpu.make_async_copy(y_ref.at[pl.ds(p2_ref[0, r], 1), :], buf2.at[pl.ds(r, 1), :], sem.at[1]).start()
        return c

    lax.fori_loop(0, tb, start, 0, unroll=8)
    for _ in range(tb):
        pltpu.make_async_copy(y_ref.at[pl.ds(0, 1), :], buf1.at[pl.ds(0, 1), :], sem.at[0]).wait()
        pltpu.make_async_copy(y_ref.at[pl.ds(0, 1), :], buf2.at[pl.ds(0, 1), :], sem.at[1]).wait()
    f = w1_ref[...] * buf1[...] + w2_ref[...] * buf2[...]
    o_ref[...] = x1_ref[...] + g2_ref[...] * f


def _combine(y, pos1, pos2, w1, w2, x1, layer, mod, tb):
    m = x1.shape[0]
    smem = pl.BlockSpec((None, 1, tb), lambda i: (i, 0, 0), memory_space=pltpu.SMEM)
    row = lambda w: pl.BlockSpec((tb, w), lambda i: (i, 0))
    return pl.pallas_call(
        functools.partial(_combine_kernel, tb=tb),
        grid=(m // tb,),
        in_specs=[smem, smem, pl.BlockSpec(memory_space=pl.ANY), row(D_MODEL),
                  mod.spec(layer, 5, tb), row(1), row(1)],
        out_specs=row(D_MODEL),
        out_shape=jax.ShapeDtypeStruct((m, D_MODEL), F32),
        scratch_shapes=[pltpu.VMEM((tb, D_MODEL), F32), pltpu.VMEM((tb, D_MODEL), F32),
                        pltpu.SemaphoreType.DMA((2,))],
        compiler_params=_params(("arbitrary",)),
        name="combine",
    )(pos1.reshape(m // tb, 1, tb), pos2.reshape(m // tb, 1, tb), y, x1, mod.arr,
      w1.reshape(m, 1), w2.reshape(m, 1))


def _moe(h2, logits, x1, layer, mod, wg, wu, wd, expert_base, sorted_buf, tm, tb):
    m = h2.shape[0]
    n_tiles = (m * TOP_K) // tm + N_EXPERTS
    pos1, pos2, w1, w2, te, nv = _route(logits, tb, tm)
    hs = _dispatch(h2, pos1, pos2, sorted_buf, tb)
    y = _ffn(hs, wg, wu, wd, te[0, :n_tiles], nv[0, :1], tm, expert_base=expert_base)
    return _combine(y, pos1, pos2, w1, w2, x1, layer, mod, tb), hs


def _qk_decode_kernel(qkv_ref, cos_ref, slo_ref, shi_ref, gq_ref, gk_ref, e2_ref, q_ref, k_ref, v_ref):
    qkv = qkv_ref[...].astype(F32)
    cos_t, slo, shi, e2 = cos_ref[...], slo_ref[...], shi_ref[...], e2_ref[...]
    for j in range(N_Q_HEADS // 2):
        qs = _head_norm_rope(qkv[:, j * LANES:(j + 1) * LANES], gq_ref[...], e2, cos_t, slo, shi)
        q_ref[:, j * LANES:(j + 1) * LANES] = qs * (HEAD_DIM ** -0.5)
    k_ref[...] = _head_norm_rope(qkv[:, ATTN_WIDTH:ATTN_WIDTH + KV_WIDTH], gk_ref[...], e2, cos_t, slo, shi)
    v_ref[...] = qkv[:, ATTN_WIDTH + KV_WIDTH:]


def _attn_decode_kernel(q_ref, ck_ref, cv_ref, kn_ref, vn_ref, sink_ref, o_ref, kw_ref, vw_ref):
    q = q_ref[...].astype(BF16)
    ck, cv = ck_ref[...], cv_ref[...]
    kn, vn = kn_ref[...], vn_ref[...]
    s = jnp.einsum('bhd,bwd->bhw', q, ck.astype(BF16), preferred_element_type=F32)
    knb = kn.astype(BF16).astype(F32)
    s_new = jnp.sum(q.astype(F32) * knb[:, None, :], axis=-1, keepdims=True)
    sink = sink_ref[...]
    m = jnp.maximum(jnp.maximum(jnp.max(s, axis=-1, keepdims=True), s_new), sink)
    p = jnp.exp(s - m)
    pn = jnp.exp(s_new - m)
    inv = 1.0 / (jnp.sum(p, axis=-1, keepdims=True) + pn + jnp.exp(sink - m))
    o = jnp.einsum('bhw,bwd->bhd', (p * inv).astype(BF16), cv.astype(BF16), preferred_element_type=F32)
    o_ref[...] = o + (pn * inv).astype(BF16).astype(F32) * vn.astype(BF16).astype(F32)[:, None, :]
    w = lax.broadcasted_iota(I32, ck.shape, 1)
    last = w == ck.shape[1] - 1
    kw_ref[...] = jnp.where(last, kn[:, None, :], pltpu.roll(ck, ck.shape[1] - 1, 1))
    vw_ref[...] = jnp.where(last, vn[:, None, :], pltpu.roll(cv, cv.shape[1] - 1, 1))


def _attention_decode(qkv, layer, ck_all, cv_all, sinks_col, gq_all, gk_all, tables, e2, tb):
    n = qkv.shape[0]
    win = ck_all.shape[2]
    one = lambda shape: pl.BlockSpec(shape, lambda i: (0,) * len(shape))
    qn, kn, vn = pl.pallas_call(
        _qk_decode_kernel,
        in_specs=[one((n, QKV_W)), one((1, LANES)), one((1, LANES)), one((1, LANES)),
                  pl.BlockSpec((None, 1, LANES), lambda i: (layer, 0, 0)),
                  pl.BlockSpec((None, 1, LANES), lambda i: (layer, 0, 0)), one((LANES, LANES))],
        out_shape=[jax.ShapeDtypeStruct((n, ATTN_WIDTH), F32),
                   jax.ShapeDtypeStruct((n, KV_WIDTH), F32),
                   jax.ShapeDtypeStruct((n, KV_WIDTH), F32)],
        grid=(1,),
        compiler_params=_params(("arbitrary",)),
        name="qk_decode",
    )(qkv, *tables, gq_all, gk_all, e2)
    qh = qn.reshape(n, N_KV_HEADS, GQA_GROUP, HEAD_DIM)
    z = jnp.zeros_like(qh[:, 0])
    q_pl = jnp.concatenate([jnp.concatenate([qh[:, 0], z], axis=-1),
                            jnp.concatenate([z, qh[:, 1]], axis=-1)], axis=1)
    blk3 = lambda a, b: pl.BlockSpec((tb, a, b), lambda i: (i, 0, 0))
    cache = pl.BlockSpec((None, tb, win, KV_WIDTH), lambda i: (layer, i, 0, 0))
    blk2 = pl.BlockSpec((tb, KV_WIDTH), lambda i: (i, 0))
    o, kw, vw = pl.pallas_call(
        _attn_decode_kernel,
        grid=(n // tb,),
        in_specs=[blk3(N_Q_HEADS, LANES), cache, cache, blk2, blk2,
                  pl.BlockSpec((None, N_Q_HEADS, 1), lambda i: (layer, 0, 0))],
        out_specs=[blk3(N_Q_HEADS, LANES), blk3(win, KV_WIDTH), blk3(win, KV_WIDTH)],
        out_shape=[jax.ShapeDtypeStruct((n, N_Q_HEADS, LANES), F32),
                   jax.ShapeDtypeStruct((n, win, KV_WIDTH), F32),
                   jax.ShapeDtypeStruct((n, win, KV_WIDTH), F32)],
        compiler_params=_params(("arbitrary",)),
        name="attn_decode",
    )(q_pl, ck_all, cv_all, kn, vn, sinks_col)
    o = o.reshape(n, N_KV_HEADS, GQA_GROUP, N_KV_HEADS, HEAD_DIM)
    o = jnp.concatenate([o[:, 0, :, 0], o[:, 1, :, 1]], axis=1).reshape(n, ATTN_WIDTH)
    return o.astype(BF16), kw, vw


def _lru_decode_kernel(x_ref, gate_ref, buf_ref, h0_ref, cw_ref, cb_ref, wa_ref, ba_ref, wx_ref, bx_ref,
                       negsp_ref, o_ref, h_ref, nbuf_ref):
    x = x_ref[...]
    xc = x * cw_ref[CONV_K - 1:CONV_K, :] + cb_ref[...]
    for j in range(CONV_K - 1):
        xc = xc + buf_ref[:, j, :] * cw_ref[j:j + 1, :]
    a, u = _lru_gates(xc, wa_ref, ba_ref, wx_ref, bx_ref, negsp_ref)
    h = a * h0_ref[...] + u
    o_ref[...] = (h * _gelu_tanh(gate_ref[...].astype(F32))).astype(BF16)
    h_ref[...] = h
    for j in range(CONV_K - 2):
        nbuf_ref[:, j, :] = buf_ref[:, j + 1, :]
    nbuf_ref[:, CONV_K - 2, :] = x


def _lru_decode(lx, lgate, layer, buf_all, h0_all, lru_params):
    n = lx.shape[0]
    one = lambda shape: pl.BlockSpec(shape, lambda i: (0,) * len(shape))
    lay = lambda shape: pl.BlockSpec((None,) + shape, lambda i: (layer,) + (0,) * len(shape))
    nb = LRU_WIDTH // LANES
    return pl.pallas_call(
        _lru_decode_kernel,
        in_specs=[one((n, LRU_WIDTH)), one((n, LRU_WIDTH)), lay((n, CONV_K - 1, LRU_WIDTH)), lay((n, LRU_WIDTH)),
                  lay((CONV_K, LRU_WIDTH)), lay((1, LRU_WIDTH)), lay((nb, LANES, LANES)), lay((1, LRU_WIDTH)),
                  lay((nb, LANES, LANES)), lay((1, LRU_WIDTH)), lay((1, LRU_WIDTH))],
        out_shape=[jax.ShapeDtypeStruct((n, LRU_WIDTH), BF16),
                   jax.ShapeDtypeStruct((n, LRU_WIDTH), F32),
                   jax.ShapeDtypeStruct((n, CONV_K - 1, LRU_WIDTH), F32)],
        grid=(1,),
        compiler_params=_params(("arbitrary",)),
        name="lru_decode",
    )(lx, lgate, buf_all, h0_all, *lru_params)


def _ssd_pre_decode_kernel(xbc_ref, dt_ref, buf_ref, cw_ref, cb_ref, dtb_ref, aneg_ref, rexp_ref,
                           xs_ref, xdt_ref, dec_ref, bm_ref, cm_ref, nbuf_ref):
    xbc = xbc_ref[...]
    xc = xbc * cw_ref[CONV_K - 1:CONV_K, :] + cb_ref[...]
    for j in range(CONV_K - 1):
        xc = xc + buf_ref[:, j, :] * cw_ref[j:j + 1, :]
    xc = _silu(xc)
    xs = xc[:, :SSD_INNER]
    rexp = rexp_ref[...]
    dt = _softplus(dt_ref[...] + dtb_ref[...])
    xs_ref[...] = xs
    xdt_ref[...] = xs * _dot_exact_rhs(dt, rexp)
    dec_ref[...] = jnp.exp(_dot_exact_rhs(dt * aneg_ref[...], rexp))
    bm_ref[...] = xc[:, SSD_INNER:SSD_INNER + LANES]
    cm_ref[...] = xc[:, SSD_INNER + LANES:]
    for j in range(CONV_K - 2):
        nbuf_ref[:, j, :] = buf_ref[:, j + 1, :]
    nbuf_ref[:, CONV_K - 2, :] = xbc


def _ssd_state_decode_kernel(h0_ref, x_ref, dec_ref, bm_ref, cm_ref, rep_ref, tile_ref, sum_ref,
                             hn_ref, y_ref, *, tb):
    pn = SSD_HEAD_DIM * SSD_STATE
    x = x_ref[...].reshape(tb * SSD_HEADS, SSD_HEAD_DIM).astype(BF16)
    xrep = jnp.dot(x, rep_ref[...], preferred_element_type=F32).reshape(tb, SSD_HEADS, pn)
    bmb, cmb = bm_ref[...].astype(BF16), cm_ref[...].astype(BF16)
    first = lax.broadcasted_iota(I32, (tb, SSD_HEADS, pn), 1) < SSD_HEADS // SSD_GROUPS

    def tiled(v):
        t0 = jnp.dot(v, tile_ref[0], preferred_element_type=F32)
        t1 = jnp.dot(v, tile_ref[1], preferred_element_type=F32)
        return jnp.where(first, t0[:, None, :], t1[:, None, :])

    h_new = dec_ref[...][:, :, 0:1] * h0_ref[...] + xrep * tiled(bmb)
    hn_ref[...] = h_new
    hc = (h_new * tiled(cmb)).reshape(tb * SSD_HEADS, pn).astype(BF16)
    y_ref[...] = jnp.dot(hc, sum_ref[...], preferred_element_type=F32).reshape(tb, SSD_HEADS, SSD_HEAD_DIM)


def _ssd_post_decode_kernel(y_ref, xs_ref, z_ref, d_ref, gn_ref, o_ref):
    y = y_ref[...] + d_ref[...] * xs_ref[...]
    o_ref[...] = _ssd_gate_norm(y, z_ref[...].astype(F32), gn_ref[...]).astype(BF16)


def _ssd_decode_consts():
    pn = SSD_HEAD_DIM * SSD_STATE
    col_p = np.arange(pn) // SSD_STATE
    col_n = np.arange(pn) % SSD_STATE
    rep = (np.arange(SSD_HEAD_DIM)[:, None] == col_p[None, :]).astype(np.float32)
    row_g = np.arange(LANES) // SSD_STATE
    row_n = np.arange(LANES) % SSD_STATE
    tile = np.stack([((row_g[:, None] == g) & (row_n[:, None] == col_n[None, :])).astype(np.float32)
                     for g in range(SSD_GROUPS)])
    summ = rep.T
    return jnp.asarray(rep, BF16), jnp.asarray(tile, BF16), jnp.asarray(summ, BF16)


def _ssd_decode(xbc, dtr, z, layer, buf_all, h0_all, ssd_params, rexp, dec_consts, tb):
    n = xbc.shape[0]
    pn = SSD_HEAD_DIM * SSD_STATE
    cw_all, cb_all, dtb_all, aneg_all, d_all, gn_all = ssd_params
    f = lambda w: jax.ShapeDtypeStruct((n, w), F32)
    one = lambda shape: pl.BlockSpec(shape, lambda i: (0,) * len(shape))
    lay = lambda shape: pl.BlockSpec((None,) + shape, lambda i: (layer,) + (0,) * len(shape))
    xs, xdt, dec, bm, cm, nbuf = pl.pallas_call(
        _ssd_pre_decode_kernel,
        in_specs=[one((n, SSD_CONV_DIM)), one((n, LANES)), lay((n, CONV_K - 1, SSD_CONV_DIM)),
                  lay((CONV_K, SSD_CONV_DIM)), lay((1, SSD_CONV_DIM)), lay((1, LANES)), lay((1, LANES)),
                  one((LANES, SSD_INNER))],
        out_shape=[f(SSD_INNER), f(SSD_INNER), f(SSD_INNER), f(LANES), f(LANES),
                   jax.ShapeDtypeStruct((n, CONV_K - 1, SSD_CONV_DIM), F32)],
        grid=(1,),
        compiler_params=_params(("arbitrary",)),
        name="ssd_pre_decode",
    )(xbc, dtr, buf_all, cw_all, cb_all, dtb_all, aneg_all, rexp)
    rep, tile, summ = dec_consts
    b3 = lambda a, b: pl.BlockSpec((tb, a, b), lambda i: (i, 0, 0))
    b2 = pl.BlockSpec((tb, LANES), lambda i: (i, 0))
    full = lambda shape: pl.BlockSpec(shape, lambda i: (0,) * len(shape))
    h_new, y = pl.pallas_call(
        functools.partial(_ssd_state_decode_kernel, tb=tb),
        grid=(n // tb,),
        in_specs=[pl.BlockSpec((None, tb, SSD_HEADS, pn), lambda i: (layer, i, 0, 0)),
                  b3(SSD_HEADS, SSD_HEAD_DIM), b3(SSD_HEADS, SSD_HEAD_DIM), b2, b2,
                  full((SSD_HEAD_DIM, pn)), full((SSD_GROUPS, LANES, pn)), full((pn, SSD_HEAD_DIM))],
        out_specs=[b3(SSD_HEADS, pn), b3(SSD_HEADS, SSD_HEAD_DIM)],
        out_shape=[jax.ShapeDtypeStruct((n, SSD_HEADS, pn), F32),
                   jax.ShapeDtypeStruct((n, SSD_HEADS, SSD_HEAD_DIM), F32)],
        compiler_params=_params(("arbitrary",)),
        name="ssd_state_decode",
    )(h0_all, xdt.reshape(n, SSD_HEADS, SSD_HEAD_DIM), dec.reshape(n, SSD_HEADS, SSD_HEAD_DIM), bm, cm,
      rep, tile, summ)
    o = pl.pallas_call(
        _ssd_post_decode_kernel,
        in_specs=[one((n, SSD_INNER)), one((n, SSD_INNER)), one((n, SSD_INNER)),
                  lay((1, SSD_INNER)), lay((1, SSD_INNER))],
        out_shape=jax.ShapeDtypeStruct((n, SSD_INNER), BF16),
        grid=(1,),
        compiler_params=_params(("arbitrary",)),
        name="ssd_post_decode",
    )(y.reshape(n, SSD_INNER), xs, z, d_all, gn_all)
    return o, h_new, nbuf


def _pack_w_in(w):
    dt0 = C_GT
    dt = jnp.pad(w[..., dt0:dt0 + SSD_HEADS], ((0, 0), (0, 0), (0, LANES - SSD_HEADS)))
    return jnp.concatenate([w[..., :dt0], w[..., dt0 + SSD_HEADS:], dt], axis=-1).astype(BF16)


def kernel(x_prompt, x_sample, c_prompt, c_sample, cache_k, cache_v, state_lru_h, state_lru_conv, state_ssd_h, state_ssd_conv, w_mod, b_mod, g_norm_mix, g_norm_ffn, w_in, g_q, g_k, attn_sinks, lru_conv_w, lru_conv_b, lru_w_a, lru_b_a, lru_w_x, lru_b_x, lru_lambda, ssd_conv_w, ssd_conv_b, ssd_dt_bias, ssd_a_log, ssd_d, g_ssd_norm, w_br_attn, w_br_lru, w_br_ssd, w_out, ffn_w_gate, ffn_w_up, ffn_w_down, moe_w_router, moe_b_router, moe_w_gate, moe_w_up, moe_w_down):
    bp, seq = x_prompt.shape[:2]
    ns = x_sample.shape[0]
    win = cache_k.shape[2]
    assert x_sample.shape[1] == 1
    assert win == WINDOW and PAST_LEN >= WINDOW
    mp = bp * seq
    tm_p, tm_s = 512, ns
    lru_rows = 256
    pn = SSD_HEAD_DIM * SSD_STATE

    mod = _modulation(jnp.concatenate([c_sample, c_prompt], axis=0), w_mod, b_mod)
    mod_s = _Mod(mod, True, ns, 1)
    mod_p = _Mod(mod[:, ns:].reshape(DEPTH * bp * 6, 1, D_MODEL), False, bp, seq)
    row3 = lambda v: v.astype(F32)[:, None, :]
    w_packed = _pack_w_in(w_in)
    g_mix, g_ffn = row3(g_norm_mix), row3(g_norm_ffn)
    gq2, gk2 = row3(jnp.tile(g_q, (1, 2))), row3(jnp.tile(g_k, (1, 2)))
    sinks = attn_sinks.astype(F32)
    sinks_col = sinks[:, :, None]
    e2 = _head_avg_matrix()
    tab_p = _rope_tables(jnp.arange(seq))
    tab_s = _rope_tables(jnp.full((1,), PAST_LEN))
    wa, wx, negsp = _lru_weight_stack(lru_w_a, lru_w_x, lru_lambda)
    lru_params = (lru_conv_w.astype(F32), row3(lru_conv_b), wa, row3(lru_b_a), wx, row3(lru_b_x), negsp)
    ssd_params = _ssd_param_stack(ssd_conv_w, ssd_conv_b, ssd_dt_bias, ssd_a_log, ssd_d, g_ssd_norm)
    rexp = _head_expand_matrix()
    dec_consts = _ssd_decode_consts()
    br = (w_br_attn.astype(BF16), w_br_lru.astype(BF16), w_br_ssd.astype(BF16), w_out.astype(BF16))
    wr_all, br_all = _router_stack(moe_w_router, moe_b_router)
    n_moe = moe_w_gate.shape[0]
    flat = lambda w: w.astype(BF16).reshape((n_moe * N_EXPERTS,) + w.shape[2:])
    moe_w = (flat(moe_w_gate), flat(moe_w_up), flat(moe_w_down))
    ffn_w = (ffn_w_gate.astype(BF16), ffn_w_up.astype(BF16), ffn_w_down.astype(BF16))
    ck_all = cache_k.reshape(DEPTH, ns, win, KV_WIDTH)
    cv_all = cache_v.reshape(DEPTH, ns, win, KV_WIDTH)
    ssd_h0_all = state_ssd_h.reshape(DEPTH, ns, SSD_HEADS, pn)
    sorted_p = jnp.zeros((((mp * TOP_K) // tm_p + N_EXPERTS) * tm_p, D_MODEL), F32)
    sorted_s = jnp.zeros((((ns * TOP_K) // tm_s + N_EXPERTS) * tm_s, D_MODEL), F32)

    xp = x_prompt.reshape(mp, D_MODEL)
    xs = x_sample.reshape(ns, D_MODEL)
    outs_p = [[] for _ in range(6)]
    outs_s = [[] for _ in range(6)]

    for l in range(DEPTH):
        is_moe = l % 2 == 1
        j = l // 2
        router = (wr_all, br_all, j) if is_moe else None

        qkv, lx, lg, z, xbc, gt, dtr = _in_proj(xp, l, g_mix, mod_p, w_packed, tm_p)
        o_attn, k_win, v_win = _attention_prompt(qkv, l, sinks, gq2, gk2, tab_p, e2, bp, seq)
        o_lru, lru_h, lru_buf = _lru_prompt(lx, lg, l, lru_params, bp, seq, lru_rows)
        o_ssd, ssd_st, ssd_buf = _ssd_prompt(xbc, dtr, z, l, ssd_params, rexp, bp, seq)
        merged = _merge(o_attn, o_lru, o_ssd, gt, xp, l, mod_p, g_ffn, *br, router, tm_p)
        if is_moe:
            x1, h2, logits = merged
            xp, sorted_p = _moe(h2, logits, x1, l, mod_p, *moe_w, j * N_EXPERTS, sorted_p, tm_p, 256)
        else:
            x1, h2 = merged
            nt = mp // tm_p
            xp = _ffn(h2, *ffn_w, jnp.zeros((nt,), I32), jnp.full((1,), nt, I32), tm_p,
                      expert_base=j, residual=x1, mod=mod_p, layer=l)
        for acc, t in zip(outs_p, (k_win, v_win, lru_h, lru_buf, ssd_st, ssd_buf)):
            acc.append(t)

        qkv, lx, lg, z, xbc, gt, dtr = _in_proj(xs, l, g_mix, mod_s, w_packed, tm_s)
        o_attn, k_win, v_win = _attention_decode(qkv, l, ck_all, cv_all, sinks_col, gq2, gk2, tab_s, e2, 32)
        o_lru, lru_h, lru_buf = _lru_decode(lx, lg, l, state_lru_conv, state_lru_h, lru_params)
        o_ssd, ssd_h, ssd_buf = _ssd_decode(xbc, dtr, z, l, state_ssd_conv, ssd_h0_all, ssd_params, rexp,
                                            dec_consts, 16)
        merged = _merge(o_attn, o_lru, o_ssd, gt, xs, l, mod_s, g_ffn, *br, router, tm_s)
        if is_moe:
            x1, h2, logits = merged
            xs, sorted_s = _moe(h2, logits, x1, l, mod_s, *moe_w, j * N_EXPERTS, sorted_s, tm_s, ns)
        else:
            x1, h2 = merged
            xs = _ffn(h2, *ffn_w, jnp.zeros((1,), I32), jnp.ones((1,), I32), tm_s,
                      expert_base=j, residual=x1, mod=mod_s, layer=l)
        for acc, t in zip(outs_s, (k_win, v_win, lru_h, lru_buf, ssd_h, ssd_buf)):
            acc.append(t)

    st = [jnp.stack(a) for a in outs_p]
    ss = [jnp.stack(a) for a in outs_s]
    kv5 = lambda t, b: t.reshape(DEPTH, b, WINDOW, N_KV_HEADS, HEAD_DIM)
    return (xp.reshape(bp, seq, D_MODEL), xs.reshape(ns, 1, D_MODEL),
            kv5(st[0], bp), kv5(st[1], bp), st[2].reshape(DEPTH, bp, LRU_WIDTH), st[3],
            _ssd_state_to_hpn(st[4]), st[5],
            kv5(ss[0], ns), kv5(ss[1], ns), ss[2], ss[3],
            ss[4].reshape(DEPTH, ns, SSD_HEADS, SSD_HEAD_DIM, SSD_STATE), ss[5])
```

```python
import functools
import math

import numpy as np
import jax
import jax.numpy as jnp
from jax import lax
from jax.experimental import pallas as pl
from jax.experimental.pallas import tpu as pltpu

F32 = jnp.float32
BF16 = jnp.bfloat16
I32 = jnp.int32

D_MODEL = 1024
DEPTH = 4
PAST_LEN = 8192
N_Q_HEADS = 8
N_KV_HEADS = 2
HEAD_DIM = 64
GQA_GROUP = N_Q_HEADS // N_KV_HEADS
WINDOW = 128
ATTN_BLOCK = 128
ROT_DIM = HEAD_DIM // 4
ROPE_THETA = 500000.0
ATTN_WIDTH = N_Q_HEADS * HEAD_DIM
KV_WIDTH = N_KV_HEADS * HEAD_DIM
LRU_WIDTH = 512
LRU_BLOCKS = 8
LRU_BLOCK_DIM = LRU_WIDTH // LRU_BLOCKS
LRU_C = 8.0
CONV_K = 4
SSD_HEADS = 8
SSD_HEAD_DIM = 64
SSD_INNER = SSD_HEADS * SSD_HEAD_DIM
SSD_GROUPS = 2
SSD_STATE = 64
SSD_CHUNK = 128
SSD_CONV_DIM = SSD_INNER + 2 * SSD_GROUPS * SSD_STATE
D_FF = 2816
N_EXPERTS = 8
TOP_K = 2
EPS = 1e-6

LANES = 128
SUBLANES = 8
QKV_W = ATTN_WIDTH + 2 * KV_WIDTH
GATES_W = 3 * D_MODEL
C_QKV = 0
C_LX = C_QKV + QKV_W
C_LG = C_LX + LRU_WIDTH
C_Z = C_LG + LRU_WIDTH
C_XBC = C_Z + SSD_INNER
C_GT = C_XBC + SSD_CONV_DIM
C_DT = C_GT + GATES_W
IN_PACKED = C_DT + LANES
FF_CHUNK = D_FF // 2
VMEM_LIMIT = 56 * 1024 * 1024


def _params(sem=None):
    if sem is None:
        return pltpu.CompilerParams(vmem_limit_bytes=VMEM_LIMIT)
    return pltpu.CompilerParams(dimension_semantics=sem, vmem_limit_bytes=VMEM_LIMIT)


def _layer_spec(shape, layer, ngrid):
    zeros = (0,) * len(shape)
    return pl.BlockSpec((None,) + tuple(shape), lambda *g: (layer,) + zeros)


def _sigmoid(x):
    return 0.5 * jnp.tanh(0.5 * x) + 0.5


def _silu(x):
    return x * _sigmoid(x)


def _softplus(x):
    return jnp.maximum(x, 0.0) + jnp.log(1.0 + jnp.exp(-jnp.abs(x)))


def _gelu_tanh(x):
    return 0.5 * x * (1.0 + jnp.tanh(math.sqrt(2.0 / math.pi) * (x + 0.044715 * (x * x * x))))


def _split3(a):
    a1 = a.astype(BF16)
    r1 = a - a1.astype(F32)
    a2 = r1.astype(BF16)
    a3 = (r1 - a2.astype(F32)).astype(BF16)
    return a1, a2, a3


def _dot_exact_rhs(a, m):
    a1, a2, a3 = _split3(a)
    d = lambda t: jnp.dot(t, m, preferred_element_type=F32)
    return d(a1) + d(a2) + d(a3)


def _dot_exact_lhs(m, a):
    a1, a2, a3 = _split3(a)
    d = lambda t: jnp.dot(m, t, preferred_element_type=F32)
    return d(a1) + d(a2) + d(a3)


def _rms(x, g):
    return x * lax.rsqrt(jnp.mean(x * x, axis=-1, keepdims=True) + EPS) * g


def _mod_kernel(c_ref, w_ref, b_ref, o_ref):
    s = _silu(c_ref[...]).astype(BF16)
    o_ref[...] = jnp.dot(s, w_ref[...].astype(BF16), preferred_element_type=F32) + b_ref[...]


def _modulation(c, w_mod, b_mod):
    rows = c.shape[0]
    tn = 1536
    nt = (6 * D_MODEL) // tn
    return pl.pallas_call(
        _mod_kernel,
        grid=(DEPTH, nt),
        in_specs=[
            pl.BlockSpec((rows, D_MODEL), lambda l, j: (0, 0)),
            pl.BlockSpec((None, D_MODEL, tn), lambda l, j: (l, 0, j)),
            pl.BlockSpec((None, 1, tn), lambda l, j: (l, 0, j)),
        ],
        out_specs=pl.BlockSpec((None, rows, tn), lambda l, j: (l, 0, j)),
        out_shape=jax.ShapeDtypeStruct((DEPTH, rows, 6 * D_MODEL), F32),
        compiler_params=_params(("arbitrary", "arbitrary")),
        name="modulation",
    )(c, w_mod, b_mod.reshape(DEPTH, 1, 6 * D_MODEL))


class _Mod:
    def __init__(self, arr, per_row, batch, seq_len):
        self.arr, self.per_row, self.batch, self.seq_len = arr, per_row, batch, seq_len

    def spec(self, layer, k, tm, nprefetch=0):
        if self.per_row:
            return pl.BlockSpec((None, tm, D_MODEL), lambda i, *_: (layer, i, k))
        b, s = self.batch, self.seq_len
        return pl.BlockSpec((None, 1, D_MODEL), lambda i, *_: ((layer * b + (i * tm) // s) * 6 + k, 0, 0))


def _in_kernel(*refs, qk_norm):
    x_ref, g_ref, sc_ref, sh_ref, w_ref = refs[:5]
    qkv_ref, lx_ref, lg_ref, z_ref, xbc_ref, gt_ref, dt_ref = refs[-7:]
    h = (_rms(x_ref[...], g_ref[...]) * (1.0 + sc_ref[...]) + sh_ref[...]).astype(BF16)

    def mm(lo, hi):
        return jnp.dot(h, w_ref[:, lo:hi], preferred_element_type=F32)

    if qk_norm:
        cos_ref, slo_ref, shi_ref, gq_ref, gk_ref, e2_ref = refs[5:11]
        cos_t, slo, shi, e2 = cos_ref[...], slo_ref[...], shi_ref[...], e2_ref[...]
        for j in range(ATTN_WIDTH // LANES):
            q_slab = _head_norm_rope(mm(j * LANES, (j + 1) * LANES), gq_ref[...], e2, cos_t, slo, shi)
            qkv_ref[:, j * LANES:(j + 1) * LANES] = (q_slab * (HEAD_DIM ** -0.5)).astype(BF16)
        k_slab = _head_norm_rope(mm(ATTN_WIDTH, ATTN_WIDTH + KV_WIDTH), gk_ref[...], e2, cos_t, slo, shi)
        qkv_ref[:, ATTN_WIDTH:ATTN_WIDTH + KV_WIDTH] = k_slab.astype(BF16)
        qkv_ref[:, ATTN_WIDTH + KV_WIDTH:] = mm(ATTN_WIDTH + KV_WIDTH, C_LX).astype(BF16)
    else:
        qkv_ref[...] = mm(C_QKV, C_LX).astype(BF16)
    lx_ref[...] = mm(C_LX, C_LG)
    lg_ref[...] = mm(C_LG, C_Z).astype(BF16)
    z_ref[...] = mm(C_Z, C_XBC).astype(BF16)
    xbc_ref[...] = mm(C_XBC, C_GT)
    gt_ref[...] = mm(C_GT, C_DT).astype(BF16)
    dt_ref[...] = mm(C_DT, IN_PACKED)


def _in_proj(x, layer, g_all, mod, w_all, tm, qk_norm=None):
    m = x.shape[0]
    row = lambda w: pl.BlockSpec((tm, w), lambda i: (i, 0))
    in_specs = [row(D_MODEL), _layer_spec((1, D_MODEL), layer, 1), mod.spec(layer, 1, tm), mod.spec(layer, 0, tm),
                _layer_spec((D_MODEL, IN_PACKED), layer, 1)]
    args = [x, g_all, mod.arr, mod.arr, w_all]
    if qk_norm is not None:
        tables, gq_all, gk_all, e2 = qk_norm
        tab = pl.BlockSpec((1, LANES), lambda i: (0, 0))
        in_specs += [tab, tab, tab, _layer_spec((1, LANES), layer, 1), _layer_spec((1, LANES), layer, 1),
                     pl.BlockSpec((LANES, LANES), lambda i: (0, 0))]
        args += [*tables, gq_all, gk_all, e2]
    return pl.pallas_call(
        functools.partial(_in_kernel, qk_norm=qk_norm is not None),
        grid=(m // tm,),
        in_specs=in_specs,
        out_specs=[row(QKV_W), row(LRU_WIDTH), row(LRU_WIDTH), row(SSD_INNER),
                   row(SSD_CONV_DIM), row(GATES_W), row(LANES)],
        out_shape=[jax.ShapeDtypeStruct((m, QKV_W), BF16),
                   jax.ShapeDtypeStruct((m, LRU_WIDTH), F32),
                   jax.ShapeDtypeStruct((m, LRU_WIDTH), BF16),
                   jax.ShapeDtypeStruct((m, SSD_INNER), BF16),
                   jax.ShapeDtypeStruct((m, SSD_CONV_DIM), F32),
                   jax.ShapeDtypeStruct((m, GATES_W), BF16),
                   jax.ShapeDtypeStruct((m, LANES), F32)],
        compiler_params=_params(("arbitrary",)),
        name="in_proj",
    )(*args)


def _head_norm_rope(t, g2, e2, cos_t, sin_lo, sin_hi):
    sq = t * t
    hi = sq.astype(BF16)
    lo = (sq - hi.astype(F32)).astype(BF16)
    ms = jnp.dot(hi, e2, preferred_element_type=F32) + jnp.dot(lo, e2, preferred_element_type=F32)
    tn = t * lax.rsqrt(ms + EPS) * g2
    up = pltpu.roll(tn, LANES - ROT_DIM // 2, 1)
    dn = pltpu.roll(tn, ROT_DIM // 2, 1)
    return tn * cos_t + up * sin_lo + dn * sin_hi


def _rope_tables(pos):
    half = ROT_DIM // 2
    inv_freq = ROPE_THETA ** (-jnp.arange(0, ROT_DIM, 2, dtype=F32) / ROT_DIM)
    ang = pos.astype(F32)[:, None] * inv_freq[None, :]
    cos, sin = jnp.cos(ang), jnp.sin(ang)
    n = pos.shape[0]
    ones = jnp.ones((n, HEAD_DIM - ROT_DIM), F32)
    zeros = jnp.zeros((n, HEAD_DIM - ROT_DIM), F32)
    zh = jnp.zeros((n, half), F32)
    cos_t = jnp.concatenate([cos, cos, ones], axis=1)
    sin_lo = jnp.concatenate([-sin, zh, zeros], axis=1)
    sin_hi = jnp.concatenate([zh, sin, zeros], axis=1)
    rep = lambda t: jnp.concatenate([t, t], axis=1)
    return rep(cos_t), rep(sin_lo), rep(sin_hi)


def _head_avg_matrix():
    r = np.arange(LANES)
    return jnp.asarray((r[:, None] // HEAD_DIM == r[None, :] // HEAD_DIM).astype(np.float32) / HEAD_DIM, dtype=BF16)


def _place_kv(t, fill_left, fill_right):
    lane = lax.broadcasted_iota(I32, t.shape, 1)
    left = lane < HEAD_DIM
    sw = pltpu.roll(t, HEAD_DIM, 1)
    return [jnp.where(left, t, fill_left).astype(BF16), jnp.where(left, fill_right, sw).astype(BF16),
            jnp.where(left, sw, fill_left).astype(BF16), jnp.where(left, fill_right, t).astype(BF16)]


def _attn_kernel(sink_ref, qkv_ref, bias0_ref, bias1_ref, cos_ref, slo_ref, shi_ref, gq_ref, gk_ref, e2_ref,
                 o_ref, kwin_ref, vwin_ref, kvar, vvar, *, layer):
    n = pl.program_id(1)
    blk = ATTN_BLOCK

    @pl.when(n == 0)
    def _():
        kvar[...] = jnp.zeros_like(kvar)
        vvar[...] = jnp.zeros_like(vvar)

    lane = lax.broadcasted_iota(I32, (blk, LANES), 1)
    zero = jnp.zeros((blk, LANES), F32)
    one_mid = jnp.where(lane == HEAD_DIM, 1.0, 0.0)
    one_first = jnp.where(lane == 0, 1.0, 0.0)
    cos_t, slo, shi, e2 = cos_ref[...], slo_ref[...], shi_ref[...], e2_ref[...]
    k_f = _head_norm_rope(qkv_ref[:, ATTN_WIDTH:ATTN_WIDTH + KV_WIDTH].astype(F32), gk_ref[...], e2, cos_t, slo, shi)
    v_f = qkv_ref[:, ATTN_WIDTH + KV_WIDTH:].astype(F32)
    q_n = []
    for j in range(ATTN_WIDTH // LANES):
        q_slab = _head_norm_rope(qkv_ref[:, j * LANES:(j + 1) * LANES].astype(F32), gq_ref[...], e2, cos_t, slo, shi)
        q_n.append((q_slab * (HEAD_DIM ** -0.5)).astype(BF16))
    k_blocks = [[kvar[i] for i in range(4)]]
    v_blocks = [[vvar[i] for i in range(4)]]
    for qb in range(2):
        k_blocks.append(_place_kv(k_f[qb * blk:(qb + 1) * blk], zero, zero))
        v_blocks.append(_place_kv(v_f[qb * blk:(qb + 1) * blk], one_mid, one_first))

    row2 = lax.broadcasted_iota(I32, (2 * blk, 1), 0)
    left2 = lax.broadcasted_iota(I32, (2 * blk, LANES), 1) < HEAD_DIM
    for qb in range(2):
        bias = bias0_ref[...] if qb == 0 else bias1_ref[...]
        rows = slice(qb * blk, (qb + 1) * blk)
        for g in range(N_KV_HEADS):
            q2 = jnp.concatenate([q_n[2 * g][rows], q_n[2 * g + 1][rows]], axis=0)
            acc = None
            for side in range(2):
                var = 2 * g + side
                keys = jnp.concatenate([k_blocks[qb][var], k_blocks[qb + 1][var]], axis=0)
                vals = jnp.concatenate([v_blocks[qb][var], v_blocks[qb + 1][var]], axis=0)
                s = lax.dot_general(q2, keys, (((1,), (1,)), ((), ())), preferred_element_type=F32) + bias
                h_top = GQA_GROUP * g + side
                sink = jnp.where(row2 < blk, sink_ref[layer, h_top], sink_ref[layer, h_top + 2])
                m = jnp.maximum(jnp.max(s, axis=-1, keepdims=True), sink)
                p = jnp.exp(s - m).astype(BF16)
                o = jnp.dot(p, vals, preferred_element_type=F32)
                ones_lane = HEAD_DIM if side == 0 else 0
                den = o[:, ones_lane:ones_lane + 1] + jnp.exp(sink - m)
                o = o * (1.0 / den)
                o = jnp.where(left2, o, 0.0) if side == 0 else jnp.where(left2, 0.0, o)
                acc = o if acc is None else acc + o
            o_ref[rows, (2 * g) * LANES:(2 * g + 1) * LANES] = acc[:blk].astype(BF16)
            o_ref[rows, (2 * g + 1) * LANES:(2 * g + 2) * LANES] = acc[blk:].astype(BF16)

    for i in range(4):
        kvar[i] = k_blocks[2][i]
        vvar[i] = v_blocks[2][i]
    kwin_ref[...] = k_f[blk:]
    vwin_ref[...] = v_f[blk:]


def _attn_bias():
    r = np.arange(2 * ATTN_BLOCK)[:, None] % ATTN_BLOCK
    c = np.arange(2 * ATTN_BLOCK)[None, :]
    band = (c >= r) & (c <= r + WINDOW)
    first = band & (c >= ATTN_BLOCK)
    to_bias = lambda ok: np.where(ok, 0.0, -np.inf).astype(np.float32)
    return jnp.asarray(np.stack([to_bias(first), to_bias(band)]))


def _attention_prompt(qkv, layer, sinks_all, bias, tables, gq_all, gk_all, e2, batch, seq_len):
    tq = 2 * ATTN_BLOCK
    nb = seq_len // tq
    win = pl.BlockSpec((None, ATTN_BLOCK, KV_WIDTH), lambda b, n: (b, 0, 0))
    tab = pl.BlockSpec((tq, LANES), lambda b, n: (n, 0))
    return pl.pallas_call(
        functools.partial(_attn_kernel, layer=layer),
        grid=(batch, nb),
        in_specs=[pl.BlockSpec(memory_space=pltpu.SMEM),
                  pl.BlockSpec((tq, QKV_W), lambda b, n: (b * nb + n, 0)),
                  pl.BlockSpec((None, tq, tq), lambda b, n: (jnp.minimum(n, 1), 0, 0)),
                  pl.BlockSpec((None, tq, tq), lambda b, n: (1, 0, 0)),
                  tab, tab, tab, _layer_spec((1, LANES), layer, 2), _layer_spec((1, LANES), layer, 2),
                  pl.BlockSpec((LANES, LANES), lambda b, n: (0, 0))],
        out_specs=[pl.BlockSpec((tq, ATTN_WIDTH), lambda b, n: (b * nb + n, 0)), win, win],
        out_shape=[jax.ShapeDtypeStruct((batch * seq_len, ATTN_WIDTH), BF16),
                   jax.ShapeDtypeStruct((batch, ATTN_BLOCK, KV_WIDTH), F32),
                   jax.ShapeDtypeStruct((batch, ATTN_BLOCK, KV_WIDTH), F32)],
        scratch_shapes=[pltpu.VMEM((4, ATTN_BLOCK, LANES), BF16), pltpu.VMEM((4, ATTN_BLOCK, LANES), BF16)],
        compiler_params=_params(("arbitrary", "arbitrary")),
        name="attn_prompt",
    )(sinks_all, qkv, bias, bias, *tables, gq_all, gk_all, e2)


def _conv_chunk(tail, x, w_ref, b_ref, rows):
    width = x.shape[1]
    x3 = x.reshape(rows // SUBLANES, SUBLANES, width)
    full = jnp.concatenate([tail[...][None], x3], axis=0)
    t = lax.broadcasted_iota(I32, x3.shape, 1)
    y = x3 * w_ref[CONV_K - 1:CONV_K, :] + b_ref[...]
    for j in range(1, CONV_K):
        rot = pltpu.roll(full, j, 1)
        y = y + jnp.where(t >= j, rot[1:], rot[:-1]) * w_ref[CONV_K - 1 - j:CONV_K - j, :]
    return y.reshape(rows, width)


def _lru_gates(xc, wa_ref, ba_ref, wx_ref, bx_ref, negsp_ref):
    xb = xc.astype(BF16)
    ra, ri = [], []
    for s in range(LRU_WIDTH // LANES):
        sl = slice(s * LANES, (s + 1) * LANES)
        ra.append(jnp.dot(xb[:, sl], wa_ref[s], preferred_element_type=F32))
        ri.append(jnp.dot(xb[:, sl], wx_ref[s], preferred_element_type=F32))
    r = _sigmoid(jnp.concatenate(ra, axis=1) + ba_ref[...])
    i = _sigmoid(jnp.concatenate(ri, axis=1) + bx_ref[...])
    log_a = r * negsp_ref[...]
    a = jnp.exp(log_a)
    u = jnp.sqrt(1.0 - a * a) * (i * xc)
    return a, u


def _lru_kernel(x_ref, gate_ref, cw_ref, cb_ref, wa_ref, ba_ref, wx_ref, bx_ref, negsp_ref,
                o_ref, h_ref, buf_ref, xbuf, hcar, *, rows):
    n = pl.program_id(1)

    @pl.when(n == 0)
    def _():
        xbuf[...] = jnp.zeros_like(xbuf)
        hcar[...] = jnp.zeros_like(hcar)

    x = x_ref[...]
    xc = _conv_chunk(xbuf, x, cw_ref, cb_ref, rows)
    a, u = _lru_gates(xc, wa_ref, ba_ref, wx_ref, bx_ref, negsp_ref)

    ng = rows // SUBLANES
    a = a.reshape(ng, SUBLANES, LRU_WIDTH)
    u = u.reshape(ng, SUBLANES, LRU_WIDTH)
    t = lax.broadcasted_iota(I32, a.shape, 1)
    d = 1
    while d < SUBLANES:
        keep = t >= d
        a_s = jnp.where(keep, pltpu.roll(a, d, 1), 1.0)
        u_s = jnp.where(keep, pltpu.roll(u, d, 1), 0.0)
        u = a * u_s + u
        a = a * a_s
        d *= 2
    carry = hcar[...]
    groups = []
    for r in range(ng):
        hg = a[r] * carry + u[r]
        groups.append(hg)
        carry = hg[SUBLANES - 1:SUBLANES, :]
    h = jnp.concatenate(groups, axis=0)
    o_ref[...] = (h * _gelu_tanh(gate_ref[...].astype(F32))).astype(BF16)
    hcar[...] = carry
    h_ref[...] = carry
    buf_ref[...] = x[rows - (CONV_K - 1):rows, :]
    xbuf[...] = x[rows - SUBLANES:rows, :]


def _lru_weight_stack(lru_w_a, lru_w_x, lru_lambda):
    def pair(w):
        w = w.astype(BF16).reshape(DEPTH, LRU_BLOCKS // 2, 2, LRU_BLOCK_DIM, LRU_BLOCK_DIM)
        z = jnp.zeros((DEPTH, LRU_BLOCKS // 2, LRU_BLOCK_DIM, LRU_BLOCK_DIM), BF16)
        top = jnp.concatenate([w[:, :, 0], z], axis=3)
        bot = jnp.concatenate([z, w[:, :, 1]], axis=3)
        return jnp.concatenate([top, bot], axis=2)
    return pair(lru_w_a), pair(lru_w_x), (-LRU_C * jax.nn.softplus(-lru_lambda.astype(F32)))[:, None, :]


def _lru_param_specs(layer, ngrid):
    ls = lambda shape: _layer_spec(shape, layer, ngrid)
    nb = LRU_WIDTH // LANES
    return [ls((CONV_K, LRU_WIDTH)), ls((1, LRU_WIDTH)), ls((nb, LANES, LANES)), ls((1, LRU_WIDTH)),
            ls((nb, LANES, LANES)), ls((1, LRU_WIDTH)), ls((1, LRU_WIDTH))]


def _lru_prompt(lx, lgate, layer, lru_params, batch, seq_len, rows):
    nc = seq_len // rows
    tok = lambda w: pl.BlockSpec((rows, w), lambda b, n: (b * nc + n, 0))
    return pl.pallas_call(
        functools.partial(_lru_kernel, rows=rows),
        grid=(batch, nc),
        in_specs=[tok(LRU_WIDTH), tok(LRU_WIDTH)] + _lru_param_specs(layer, 2),
        out_specs=[tok(LRU_WIDTH),
                   pl.BlockSpec((None, 1, LRU_WIDTH), lambda b, n: (b, 0, 0)),
                   pl.BlockSpec((None, CONV_K - 1, LRU_WIDTH), lambda b, n: (b, 0, 0))],
        out_shape=[jax.ShapeDtypeStruct((batch * seq_len, LRU_WIDTH), BF16),
                   jax.ShapeDtypeStruct((batch, 1, LRU_WIDTH), F32),
                   jax.ShapeDtypeStruct((batch, CONV_K - 1, LRU_WIDTH), F32)],
        scratch_shapes=[pltpu.VMEM((SUBLANES, LRU_WIDTH), F32), pltpu.VMEM((1, LRU_WIDTH), F32)],
        compiler_params=_params(("arbitrary", "arbitrary")),
        name="lru_prompt",
    )(lx, lgate, *lru_params)


def _ssd_gate_norm(y, z, gn):
    u = y * _silu(z)
    gw = SSD_INNER // SSD_GROUPS
    parts = []
    for g in range(SSD_GROUPS):
        ug = u[:, g * gw:(g + 1) * gw]
        parts.append(ug * lax.rsqrt(jnp.mean(ug * ug, axis=-1, keepdims=True) + EPS))
    return jnp.concatenate(parts, axis=1) * gn


def _ssd_kernel(xbc_ref, dt_ref, z_ref, cw_ref, cb_ref, dtb_ref, aneg_ref, d_ref, gn_ref, rexp_ref,
                o_ref, st_ref, buf_ref, xbuf, state):
    n = pl.program_id(1)
    q = SSD_CHUNK
    gw = SSD_INNER // SSD_GROUPS

    @pl.when(n == 0)
    def _():
        xbuf[...] = jnp.zeros_like(xbuf)
        state[...] = jnp.zeros_like(state)

    xbc = xbc_ref[...]
    xc = _silu(_conv_chunk(xbuf, xbc, cw_ref, cb_ref, q))
    xs = xc[:, :SSD_INNER]
    bm = xc[:, SSD_INNER:SSD_INNER + LANES]
    cm = xc[:, SSD_INNER + LANES:]

    rexp = rexp_ref[...]
    dt = _softplus(dt_ref[...] + dtb_ref[...])
    dta = dt * aneg_ref[...]
    li = lax.broadcasted_iota(I32, (q, q), 0)
    si = lax.broadcasted_iota(I32, (q, q), 1)
    causal = li >= si
    tri = jnp.where(causal, 1.0, 0.0).astype(BF16)
    a_cs = _dot_exact_lhs(tri, dta)
    a_cs_t = a_cs.T
    dt_e = _dot_exact_rhs(dt, rexp)
    a_e = _dot_exact_rhs(a_cs, rexp)
    a_last = a_e[q - 1:q, :]

    xdt = xs * dt_e
    lane = lax.broadcasted_iota(I32, (q, LANES), 1)
    left = lane < SSD_STATE
    zero = jnp.zeros((q, LANES), F32)
    cmb = cm.astype(BF16)
    bmb = bm.astype(BF16)
    cm_g = (jnp.where(left, cm, zero).astype(BF16), jnp.where(left, zero, cm).astype(BF16))
    scores = [lax.dot_general(cm_g[g], bmb, (((1,), (1,)), ((), ())), preferred_element_type=F32)
              for g in range(SSD_GROUPS)]

    y_off = jnp.dot(cmb, state[...].astype(BF16), preferred_element_type=F32) * jnp.exp(a_e)

    ys = []
    for j in range(SSD_INNER // LANES):
        xs_slab = xdt[:, j * LANES:(j + 1) * LANES]
        x_side = (jnp.where(left, xs_slab, zero).astype(BF16), jnp.where(left, zero, xs_slab).astype(BF16))
        acc = y_off[:, j * LANES:(j + 1) * LANES]
        for side in range(2):
            h = 2 * j + side
            g = h // (SSD_HEADS // SSD_GROUPS)
            col = a_e[:, h * SSD_HEAD_DIM:h * SSD_HEAD_DIM + 1]
            row = a_cs_t[h:h + 1, :]
            decay = jnp.exp(jnp.where(causal, col - row, -jnp.inf))
            acc = acc + jnp.dot((scores[g] * decay).astype(BF16), x_side[side], preferred_element_type=F32)
        ys.append(acc + d_ref[:, j * LANES:(j + 1) * LANES] * xs[:, j * LANES:(j + 1) * LANES])
    y = jnp.concatenate(ys, axis=1)
    o_ref[...] = _ssd_gate_norm(y, z_ref[...].astype(F32), gn_ref[...]).astype(BF16)

    x_end = (xdt * jnp.exp(a_last - a_e)).astype(BF16)
    upd = jnp.dot(bm.T.astype(BF16), x_end, preferred_element_type=F32)
    rg = lax.broadcasted_iota(I32, (LANES, SSD_INNER), 0) // SSD_STATE
    cg = lax.broadcasted_iota(I32, (LANES, SSD_INNER), 1) // gw
    new_state = jnp.where(rg == cg, jnp.exp(a_last) * state[...] + upd, 0.0)
    state[...] = new_state
    st_ref[...] = new_state
    buf_ref[...] = xbc[q - (CONV_K - 1):q, :]
    xbuf[...] = xbc[q - SUBLANES:q, :]


def _ssd_param_stack(ssd_conv_w, ssd_conv_b, ssd_dt_bias, ssd_a_log, ssd_d, g_ssd_norm):
    pad = lambda v: jnp.pad(v.astype(F32), ((0, 0), (0, LANES - SSD_HEADS)))[:, None, :]
    return (ssd_conv_w.astype(F32), ssd_conv_b.astype(F32)[:, None, :], pad(ssd_dt_bias),
            pad(-jnp.exp(ssd_a_log.astype(F32))),
            jnp.repeat(ssd_d.astype(F32), SSD_HEAD_DIM, axis=1)[:, None, :],
            g_ssd_norm.astype(F32)[:, None, :])


def _ssd_param_specs(layer, ngrid):
    ls = lambda shape: _layer_spec(shape, layer, ngrid)
    return [ls((CONV_K, SSD_CONV_DIM)), ls((1, SSD_CONV_DIM)), ls((1, LANES)), ls((1, LANES)),
            ls((1, SSD_INNER)), ls((1, SSD_INNER))]


def _head_expand_matrix():
    h = np.arange(LANES)[:, None]
    c = np.arange(SSD_INNER)[None, :]
    return jnp.asarray((h == c // SSD_HEAD_DIM).astype(np.float32), dtype=BF16)


def _ssd_prompt(xbc, dtr, z, layer, ssd_params, rexp, batch, seq_len):
    nc = seq_len // SSD_CHUNK
    tok = lambda w: pl.BlockSpec((SSD_CHUNK, w), lambda b, n: (b * nc + n, 0))
    return pl.pallas_call(
        _ssd_kernel,
        grid=(batch, nc),
        in_specs=[tok(SSD_CONV_DIM), tok(LANES), tok(SSD_INNER)] + _ssd_param_specs(layer, 2)
                 + [pl.BlockSpec((LANES, SSD_INNER), lambda b, n: (0, 0))],
        out_specs=[tok(SSD_INNER),
                   pl.BlockSpec((None, LANES, SSD_INNER), lambda b, n: (b, 0, 0)),
                   pl.BlockSpec((None, CONV_K - 1, SSD_CONV_DIM), lambda b, n: (b, 0, 0))],
        out_shape=[jax.ShapeDtypeStruct((batch * seq_len, SSD_INNER), BF16),
                   jax.ShapeDtypeStruct((batch, LANES, SSD_INNER), F32),
                   jax.ShapeDtypeStruct((batch, CONV_K - 1, SSD_CONV_DIM), F32)],
        scratch_shapes=[pltpu.VMEM((SUBLANES, SSD_CONV_DIM), F32), pltpu.VMEM((LANES, SSD_INNER), F32)],
        compiler_params=_params(("arbitrary", "arbitrary")),
        name="ssd_prompt",
    )(xbc, dtr, z, *ssd_params, rexp)


def _ssd_state_to_hpn(st):
    lead = st.shape[:-2]
    hpg = SSD_HEADS // SSD_GROUPS
    gw = SSD_INNER // SSD_GROUPS
    parts = []
    for g in range(SSD_GROUPS):
        blk = st[..., g * SSD_STATE:(g + 1) * SSD_STATE, g * gw:(g + 1) * gw]
        blk = blk.reshape(lead + (SSD_STATE, hpg, SSD_HEAD_DIM))
        parts.append(jnp.moveaxis(blk, -3, -1))
    return jnp.concatenate(parts, axis=-3)


def _merge_kernel(*refs, with_router):
    (oa_ref, ol_ref, os_ref, gt_ref, x_ref, g1_ref, sc2_ref, sh2_ref, gn_ref,
     wa_ref, wl_ref, ws_ref, wo_ref) = refs[:13]
    if with_router:
        wr_ref, br_ref, x1_ref, h2_ref, lg_ref = refs[13:]
    else:
        x1_ref, h2_ref = refs[13:]
    d = D_MODEL
    mm = lambda a, w: jnp.dot(a[...], w[...], preferred_element_type=F32)
    merged = (_sigmoid(gt_ref[:, 0:d].astype(F32)) * mm(oa_ref, wa_ref)
              + _sigmoid(gt_ref[:, d:2 * d].astype(F32)) * mm(ol_ref, wl_ref)
              + _sigmoid(gt_ref[:, 2 * d:3 * d].astype(F32)) * mm(os_ref, ws_ref))
    mix = jnp.dot(merged.astype(BF16), wo_ref[...], preferred_element_type=F32)
    x1 = x_ref[...] + g1_ref[...] * mix
    x1_ref[...] = x1
    h2 = _rms(x1, gn_ref[...]) * (1.0 + sc2_ref[...]) + sh2_ref[...]
    h2_ref[...] = h2.astype(h2_ref.dtype)
    if with_router:
        hi = h2.astype(BF16)
        lo = (h2 - hi.astype(F32)).astype(BF16)
        wr = wr_ref[...]
        t = jnp.dot(hi, wr, preferred_element_type=F32) + jnp.dot(lo, wr, preferred_element_type=F32)
        lg_ref[...] = t + pltpu.roll(t, LANES - N_EXPERTS, 1) + br_ref[...]


def _router_stack(moe_w_router, moe_b_router):
    w = moe_w_router.astype(F32)
    hi = w.astype(BF16)
    lo = (w - hi.astype(F32)).astype(BF16)
    n = w.shape[0]
    wr = jnp.concatenate([hi, lo, jnp.zeros((n, D_MODEL, LANES - 2 * N_EXPERTS), BF16)], axis=2)
    br = jnp.pad(moe_b_router.astype(F32), ((0, 0), (0, LANES - N_EXPERTS)))[:, None, :]
    return wr, br


def _merge(oa, ol, os_, gt, x, layer, mod, gn_all, wa_all, wl_all, ws_all, wo_all, router, tm):
    m = x.shape[0]
    row = lambda w: pl.BlockSpec((tm, w), lambda i: (i, 0))
    ls = lambda shape: _layer_spec(shape, layer, 1)
    in_specs = [row(ATTN_WIDTH), row(LRU_WIDTH), row(SSD_INNER), row(GATES_W), row(D_MODEL),
                mod.spec(layer, 2, tm), mod.spec(layer, 4, tm), mod.spec(layer, 3, tm),
                ls((1, D_MODEL)), ls((ATTN_WIDTH, D_MODEL)), ls((LRU_WIDTH, D_MODEL)),
                ls((SSD_INNER, D_MODEL)), ls((D_MODEL, D_MODEL))]
    args = [oa, ol, os_, gt, x, mod.arr, mod.arr, mod.arr, gn_all, wa_all, wl_all, ws_all, wo_all]
    out_specs = [row(D_MODEL), row(D_MODEL)]
    out_shape = [jax.ShapeDtypeStruct((m, D_MODEL), F32),
                 jax.ShapeDtypeStruct((m, D_MODEL), F32 if router is not None else BF16)]
    if router is not None:
        wr_all, br_all, j = router
        in_specs += [_layer_spec((D_MODEL, LANES), j, 1), _layer_spec((1, LANES), j, 1)]
        args += [wr_all, br_all]
        out_specs.append(row(LANES))
        out_shape.append(jax.ShapeDtypeStruct((m, LANES), F32))
    return pl.pallas_call(
        functools.partial(_merge_kernel, with_router=router is not None),
        grid=(m // tm,),
        in_specs=in_specs, out_specs=out_specs, out_shape=out_shape,
        compiler_params=_params(("arbitrary",)),
        name="merge",
    )(*args)


def _ffn_kernel(te_ref, nv_ref, *refs, fused_residual):
    if fused_residual:
        x_ref, wg_ref, wu_ref, wd_ref, res_ref, g2_ref, o_ref = refs
    else:
        x_ref, wg_ref, wu_ref, wd_ref, o_ref = refs
    i = pl.program_id(0)
    j = pl.program_id(1)

    @pl.when(j == 0)
    def _():
        o_ref[...] = jnp.zeros_like(o_ref)

    @pl.when(i < nv_ref[0])
    def _():
        h = x_ref[...].astype(BF16)
        a = jnp.dot(h, wg_ref[...], preferred_element_type=F32)
        b = jnp.dot(h, wu_ref[...], preferred_element_type=F32)
        t = (_silu(a) * b).astype(BF16)
        o_ref[...] += jnp.dot(t, wd_ref[...], preferred_element_type=F32)

    if fused_residual:
        @pl.when(j == pl.num_programs(1) - 1)
        def _():
            o_ref[...] = res_ref[...] + g2_ref[...] * o_ref[...]


def _ffn(x, wg, wu, wd, tile_expert, n_valid, tm, expert_base=0, residual=None, mod=None, layer=None):
    m = x.shape[0]
    nj = D_FF // FF_CHUNK
    last = nj - 1

    def jj(i, j, nv):
        return jnp.where(i < nv[0], j, last)

    in_specs = [pl.BlockSpec((tm, D_MODEL), lambda i, j, te, nv: (jnp.minimum(i, nv[0] - 1), 0)),
                pl.BlockSpec((None, D_MODEL, FF_CHUNK), lambda i, j, te, nv: (expert_base + te[i], 0, jj(i, j, nv))),
                pl.BlockSpec((None, D_MODEL, FF_CHUNK), lambda i, j, te, nv: (expert_base + te[i], 0, jj(i, j, nv))),
                pl.BlockSpec((None, FF_CHUNK, D_MODEL), lambda i, j, te, nv: (expert_base + te[i], jj(i, j, nv), 0))]
    args = [x, wg, wu, wd]
    fused = residual is not None
    if fused:
        in_specs += [pl.BlockSpec((tm, D_MODEL), lambda i, j, te, nv: (i, 0)), mod.spec(layer, 5, tm)]
        args += [residual, mod.arr]
    return pl.pallas_call(
        functools.partial(_ffn_kernel, fused_residual=fused),
        grid_spec=pltpu.PrefetchScalarGridSpec(
            num_scalar_prefetch=2, grid=(m // tm, nj), in_specs=in_specs,
            out_specs=pl.BlockSpec((tm, D_MODEL), lambda i, j, te, nv: (i, 0))),
        out_shape=jax.ShapeDtypeStruct((m, D_MODEL), F32),
        compiler_params=_params(("arbitrary", "arbitrary")),
        name="ffn",
    )(tile_expert, n_valid, *args)


def _route_kernel(lg_ref, p1_ref, p2_ref, w1_ref, w2_ref, te_ref, nv_ref, carry, e1s, e2s, r1s, r2s,
                  *, tb, tm, nblk):
    step = pl.program_id(0)

    @pl.when(step == 0)
    def _():
        carry[...] = jnp.zeros_like(carry)

    l8 = lg_ref[...].T[0:N_EXPERTS, :]
    e = lax.broadcasted_iota(I32, l8.shape, 0)
    m1 = jnp.max(l8, axis=0, keepdims=True)
    i1 = jnp.min(jnp.where(l8 == m1, e, N_EXPERTS), axis=0, keepdims=True)
    rest = jnp.where(e == i1, -jnp.inf, l8)
    m2 = jnp.max(rest, axis=0, keepdims=True)
    i2 = jnp.min(jnp.where(rest == m2, e, N_EXPERTS), axis=0, keepdims=True)
    t = jnp.exp(m2 - m1)
    w1 = 1.0 / (1.0 + t)
    sel1 = jnp.where(e == i1, 1.0, 0.0)
    sel2 = jnp.where(e == i2, 1.0, 0.0)
    sel = sel1 + sel2
    a = lax.broadcasted_iota(I32, (tb, tb), 0)
    b = lax.broadcasted_iota(I32, (tb, tb), 1)
    upper = jnp.where(a <= b, 1.0, 0.0).astype(BF16)
    incl = jnp.dot(sel.astype(BF16), upper, preferred_element_type=F32)
    rank = carry[:, 0:1] + incl - sel
    e1s[pl.ds(step, 1), :] = i1
    e2s[pl.ds(step, 1), :] = i2
    r1s[pl.ds(step, 1), :] = jnp.sum(sel1 * rank, axis=0, keepdims=True).astype(I32)
    r2s[pl.ds(step, 1), :] = jnp.sum(sel2 * rank, axis=0, keepdims=True).astype(I32)
    w1_ref[...] = w1
    w2_ref[...] = t * w1
    carry[...] = carry[...] + incl[:, tb - 1:tb]

    @pl.when(step == nblk - 1)
    def _():
        counts = carry[...].astype(I32)
        shift = tm.bit_length() - 1
        padded = lax.shift_right_logical(counts + (tm - 1), shift) << shift
        ex = lax.broadcasted_iota(I32, padded.shape, 0)
        ends = padded
        d = 1
        while d < N_EXPERTS:
            ends = ends + jnp.where(ex >= d, pltpu.roll(ends, d, 0), 0)
            d *= 2
        starts = ends - padded
        total = ends[N_EXPERTS - 1:N_EXPERTS, :]
        nvalid = lax.shift_right_logical(total, shift)
        tile0 = lax.broadcasted_iota(I32, padded.shape, 1) * tm
        te = jnp.sum(jnp.where(ends <= tile0, 1, 0), axis=0, keepdims=True)
        te_last = jnp.sum(jnp.where(ends <= total - tm, 1, 0), axis=0, keepdims=True)
        te_ref[...] = jnp.where(tile0[0:1, :] < total, te, te_last)
        nv_ref[...] = nvalid
        reps = tb // LANES
        e1, e2 = e1s[...], e2s[...]
        off1 = jnp.zeros(e1.shape, I32)
        off2 = jnp.zeros(e2.shape, I32)
        for k in range(N_EXPERTS):
            row = starts[k:k + 1, :]
            row = jnp.concatenate([row] * reps, axis=1) if reps > 1 else row
            off1 = jnp.where(e1 == k, row, off1)
            off2 = jnp.where(e2 == k, row, off2)
        p1_ref[...] = off1 + r1s[...]
        p2_ref[...] = off2 + r2s[...]


def _route(logits, tb, tm):
    m = logits.shape[0]
    nblk = m // tb
    assert tb % LANES == 0 and (m * TOP_K) // tm + N_EXPERTS <= LANES and tm & (tm - 1) == 0
    vec = pl.BlockSpec((1, tb), lambda i: (0, i))
    res = pl.BlockSpec((nblk, tb), lambda i: (0, 0))
    one = pl.BlockSpec((1, LANES), lambda i: (0, 0))
    return pl.pallas_call(
        functools.partial(_route_kernel, tb=tb, tm=tm, nblk=nblk),
        grid=(nblk,),
        in_specs=[pl.BlockSpec((tb, LANES), lambda i: (i, 0))],
        out_specs=[res, res, vec, vec, one, one],
        out_shape=[jax.ShapeDtypeStruct((nblk, tb), I32), jax.ShapeDtypeStruct((nblk, tb), I32),
                   jax.ShapeDtypeStruct((1, m), F32), jax.ShapeDtypeStruct((1, m), F32),
                   jax.ShapeDtypeStruct((1, LANES), I32), jax.ShapeDtypeStruct((1, LANES), I32)],
        scratch_shapes=[pltpu.VMEM((N_EXPERTS, LANES), F32)] + [pltpu.VMEM((nblk, tb), I32)] * 4,
        compiler_params=_params(("arbitrary",)),
        name="route",
    )(logits)


def _dispatch_kernel(p1_ref, p2_ref, x_ref, dst_in_ref, dst_ref, sem, *, tb):
    del dst_in_ref

    def start(r, c):
        src = x_ref.at[pl.ds(r, 1), :]
        pltpu.make_async_copy(src, dst_ref.at[pl.ds(p1_ref[0, r], 1), :], sem.at[0]).start()
        pltpu.make_async_copy(src, dst_ref.at[pl.ds(p2_ref[0, r], 1), :], sem.at[1]).start()
        return c

    lax.fori_loop(0, tb, start, 0, unroll=8)
    row0 = x_ref.at[pl.ds(0, 1), :]
    for _ in range(tb):
        pltpu.make_async_copy(row0, dst_ref.at[pl.ds(0, 1), :], sem.at[0]).wait()
        pltpu.make_async_copy(row0, dst_ref.at[pl.ds(0, 1), :], sem.at[1]).wait()


def _dispatch(x, pos1, pos2, dst, tb):
    m = x.shape[0]
    smem = pl.BlockSpec((None, 1, tb), lambda i: (i, 0, 0), memory_space=pltpu.SMEM)
    return pl.pallas_call(
        functools.partial(_dispatch_kernel, tb=tb),
        grid=(m // tb,),
        in_specs=[smem, smem, pl.BlockSpec((tb, D_MODEL), lambda i: (i, 0)),
                  pl.BlockSpec(memory_space=pl.ANY)],
        out_specs=pl.BlockSpec(memory_space=pl.ANY),
        out_shape=jax.ShapeDtypeStruct(dst.shape, F32),
        scratch_shapes=[pltpu.SemaphoreType.DMA((2,))],
        input_output_aliases={3: 0},
        compiler_params=_params(("arbitrary",)),
        name="dispatch",
    )(pos1.reshape(m // tb, 1, tb), pos2.reshape(m // tb, 1, tb), x, dst)


def _combine_kernel(p1_ref, p2_ref, y_ref, x1_ref, g2_ref, w1_ref, w2_ref, o_ref, buf1, buf2, sem, *, tb):
    def start(r, c):
        pltpu.make_async_copy(y_ref.at[pl.ds(p1_ref[0, r], 1), :], buf1.at[pl.ds(r, 1), :], sem.at[0]).start()
        pltpu.make_async_copy(y_ref.at[pl.ds(p2_ref[0, r], 1), :], buf2.at[pl.ds(r, 1), :], sem.at[1]).start()
        return c

    lax.fori_loop(0, tb, start, 0, unroll=8)
    for _ in range(tb):
        pltpu.make_async_copy(y_ref.at[pl.ds(0, 1), :], buf1.at[pl.ds(0, 1), :], sem.at[0]).wait()
        pltpu.make_async_copy(y_ref.at[pl.ds(0, 1), :], buf2.at[pl.ds(0, 1), :], sem.at[1]).wait()
    f = w1_ref[...] * buf1[...] + w2_ref[...] * buf2[...]
    o_ref[...] = x1_ref[...] + g2_ref[...] * f


def _combine(y, pos1, pos2, w1, w2, x1, layer, mod, tb):
    m = x1.shape[0]
    smem = pl.BlockSpec((None, 1, tb), lambda i: (i, 0, 0), memory_space=pltpu.SMEM)
    row = lambda w: pl.BlockSpec((tb, w), lambda i: (i, 0))
    return pl.pallas_call(
        functools.partial(_combine_kernel, tb=tb),
        grid=(m // tb,),
        in_specs=[smem, smem, pl.BlockSpec(memory_space=pl.ANY), row(D_MODEL),
                  mod.spec(layer, 5, tb), row(1), row(1)],
        out_specs=row(D_MODEL),
        out_shape=jax.ShapeDtypeStruct((m, D_MODEL), F32),
        scratch_shapes=[pltpu.VMEM((tb, D_MODEL), F32), pltpu.VMEM((tb, D_MODEL), F32),
                        pltpu.SemaphoreType.DMA((2,))],
        compiler_params=_params(("arbitrary",)),
        name="combine",
    )(pos1.reshape(m // tb, 1, tb), pos2.reshape(m // tb, 1, tb), y, x1, mod.arr,
      w1.reshape(m, 1), w2.reshape(m, 1))


def _moe(h2, logits, x1, layer, mod, wg, wu, wd, expert_base, sorted_buf, tm, tb):
    m = h2.shape[0]
    n_tiles = (m * TOP_K) // tm + N_EXPERTS
    pos1, pos2, w1, w2, te, nv = _route(logits, tb, tm)
    hs = _dispatch(h2, pos1, pos2, sorted_buf, tb)
    y = _ffn(hs, wg, wu, wd, te[0, :n_tiles], nv[0, :1], tm, expert_base=expert_base)
    return _combine(y, pos1, pos2, w1, w2, x1, layer, mod, tb), hs


def _attn_decode_kernel(q_ref, ck_ref, cv_ref, kn_ref, vn_ref, sink_ref, o_ref, kw_ref, vw_ref):
    q = q_ref[...].astype(BF16)
    ck, cv = ck_ref[...], cv_ref[...]
    kn, vn = kn_ref[...], vn_ref[...]
    s = jnp.einsum('bhd,bwd->bhw', q, ck.astype(BF16), preferred_element_type=F32)
    knb = kn.astype(BF16).astype(F32)
    s_new = jnp.sum(q.astype(F32) * knb[:, None, :], axis=-1, keepdims=True)
    sink = sink_ref[...]
    m = jnp.maximum(jnp.maximum(jnp.max(s, axis=-1, keepdims=True), s_new), sink)
    p = jnp.exp(s - m)
    pn = jnp.exp(s_new - m)
    inv = 1.0 / (jnp.sum(p, axis=-1, keepdims=True) + pn + jnp.exp(sink - m))
    o = jnp.einsum('bhw,bwd->bhd', (p * inv).astype(BF16), cv.astype(BF16), preferred_element_type=F32)
    o_ref[...] = o + (pn * inv).astype(BF16).astype(F32) * vn.astype(BF16).astype(F32)[:, None, :]
    w = lax.broadcasted_iota(I32, ck.shape, 1)
    last = w == ck.shape[1] - 1
    kw_ref[...] = jnp.where(last, kn[:, None, :], pltpu.roll(ck, ck.shape[1] - 1, 1))
    vw_ref[...] = jnp.where(last, vn[:, None, :], pltpu.roll(cv, cv.shape[1] - 1, 1))


def _attention_decode(qkv, layer, ck_all, cv_all, sinks_col, tb):
    n = qkv.shape[0]
    win = ck_all.shape[2]
    qh = qkv[:, :ATTN_WIDTH].reshape(n, N_KV_HEADS, GQA_GROUP, HEAD_DIM)
    z = jnp.zeros_like(qh[:, 0])
    q_pl = jnp.concatenate([jnp.concatenate([qh[:, 0], z], axis=-1),
                            jnp.concatenate([z, qh[:, 1]], axis=-1)], axis=1)
    kn = qkv[:, ATTN_WIDTH:ATTN_WIDTH + KV_WIDTH].astype(F32)
    vn = qkv[:, ATTN_WIDTH + KV_WIDTH:].astype(F32)
    blk3 = lambda a, b: pl.BlockSpec((tb, a, b), lambda i: (i, 0, 0))
    cache = pl.BlockSpec((None, tb, win, KV_WIDTH), lambda i: (layer, i, 0, 0))
    blk2 = pl.BlockSpec((tb, KV_WIDTH), lambda i: (i, 0))
    o, ck_all, cv_all = pl.pallas_call(
        _attn_decode_kernel,
        grid=(n // tb,),
        in_specs=[blk3(N_Q_HEADS, LANES), cache, cache, blk2, blk2,
                  pl.BlockSpec((None, N_Q_HEADS, 1), lambda i: (layer, 0, 0))],
        out_specs=[blk3(N_Q_HEADS, LANES), cache, cache],
        out_shape=[jax.ShapeDtypeStruct((n, N_Q_HEADS, LANES), F32),
                   jax.ShapeDtypeStruct(ck_all.shape, F32),
                   jax.ShapeDtypeStruct(cv_all.shape, F32)],
        input_output_aliases={1: 1, 2: 2},
        compiler_params=_params(("arbitrary",)),
        name="attn_decode",
    )(q_pl, ck_all, cv_all, kn, vn, sinks_col)
    o = o.reshape(n, N_KV_HEADS, GQA_GROUP, N_KV_HEADS, HEAD_DIM)
    o = jnp.concatenate([o[:, 0, :, 0], o[:, 1, :, 1]], axis=1).reshape(n, ATTN_WIDTH)
    return o.astype(BF16), ck_all, cv_all


def _lru_decode_kernel(x_ref, gate_ref, buf_ref, h0_ref, cw_ref, cb_ref, wa_ref, ba_ref, wx_ref, bx_ref,
                       negsp_ref, o_ref, h_ref, nbuf_ref):
    x = x_ref[...]
    xc = x * cw_ref[CONV_K - 1:CONV_K, :] + cb_ref[...]
    for j in range(CONV_K - 1):
        xc = xc + buf_ref[:, j, :] * cw_ref[j:j + 1, :]
    a, u = _lru_gates(xc, wa_ref, ba_ref, wx_ref, bx_ref, negsp_ref)
    h = a * h0_ref[...] + u
    o_ref[...] = (h * _gelu_tanh(gate_ref[...].astype(F32))).astype(BF16)
    h_ref[...] = h
    for j in range(CONV_K - 2):
        nbuf_ref[:, j, :] = buf_ref[:, j + 1, :]
    nbuf_ref[:, CONV_K - 2, :] = x


def _lru_decode(lx, lgate, layer, buf_all, h0_all, lru_params):
    n = lx.shape[0]
    one = lambda shape: pl.BlockSpec(shape, lambda i: (0,) * len(shape))
    lay = lambda shape: pl.BlockSpec((None,) + shape, lambda i: (layer,) + (0,) * len(shape))
    nb = LRU_WIDTH // LANES
    return pl.pallas_call(
        _lru_decode_kernel,
        in_specs=[one((n, LRU_WIDTH)), one((n, LRU_WIDTH)), lay((n, CONV_K - 1, LRU_WIDTH)), lay((n, LRU_WIDTH)),
                  lay((CONV_K, LRU_WIDTH)), lay((1, LRU_WIDTH)), lay((nb, LANES, LANES)), lay((1, LRU_WIDTH)),
                  lay((nb, LANES, LANES)), lay((1, LRU_WIDTH)), lay((1, LRU_WIDTH))],
        out_shape=[jax.ShapeDtypeStruct((n, LRU_WIDTH), BF16),
                   jax.ShapeDtypeStruct((n, LRU_WIDTH), F32),
                   jax.ShapeDtypeStruct((n, CONV_K - 1, LRU_WIDTH), F32)],
        grid=(1,),
        compiler_params=_params(("arbitrary",)),
        name="lru_decode",
    )(lx, lgate, buf_all, h0_all, *lru_params)


def _ssd_pre_decode_kernel(xbc_ref, dt_ref, buf_ref, cw_ref, cb_ref, dtb_ref, aneg_ref, rexp_ref,
                           xs_ref, xdt_ref, dec_ref, bm_ref, cm_ref, nbuf_ref):
    xbc = xbc_ref[...]
    xc = xbc * cw_ref[CONV_K - 1:CONV_K, :] + cb_ref[...]
    for j in range(CONV_K - 1):
        xc = xc + buf_ref[:, j, :] * cw_ref[j:j + 1, :]
    xc = _silu(xc)
    xs = xc[:, :SSD_INNER]
    rexp = rexp_ref[...]
    dt = _softplus(dt_ref[...] + dtb_ref[...])
    xs_ref[...] = xs
    xdt_ref[...] = xs * _dot_exact_rhs(dt, rexp)
    dec_ref[...] = jnp.exp(_dot_exact_rhs(dt * aneg_ref[...], rexp))
    bm_ref[...] = xc[:, SSD_INNER:SSD_INNER + LANES]
    cm_ref[...] = xc[:, SSD_INNER + LANES:]
    for j in range(CONV_K - 2):
        nbuf_ref[:, j, :] = buf_ref[:, j + 1, :]
    nbuf_ref[:, CONV_K - 2, :] = xbc


def _ssd_state_decode_kernel(h0_ref, x_ref, dec_ref, bm_ref, cm_ref, rep_ref, tile_ref, sum_ref,
                             hn_ref, y_ref, *, tb):
    pn = SSD_HEAD_DIM * SSD_STATE
    x = x_ref[...].reshape(tb * SSD_HEADS, SSD_HEAD_DIM).astype(BF16)
    xrep = jnp.dot(x, rep_ref[...], preferred_element_type=F32).reshape(tb, SSD_HEADS, pn)
    bmb, cmb = bm_ref[...].astype(BF16), cm_ref[...].astype(BF16)
    first = lax.broadcasted_iota(I32, (tb, SSD_HEADS, pn), 1) < SSD_HEADS // SSD_GROUPS

    def tiled(v):
        t0 = jnp.dot(v, tile_ref[0], preferred_element_type=F32)
        t1 = jnp.dot(v, tile_ref[1], preferred_element_type=F32)
        return jnp.where(first, t0[:, None, :], t1[:, None, :])

    h_new = dec_ref[...][:, :, 0:1] * h0_ref[...] + xrep * tiled(bmb)
    hn_ref[...] = h_new
    hc = (h_new * tiled(cmb)).reshape(tb * SSD_HEADS, pn).astype(BF16)
    y_ref[...] = jnp.dot(hc, sum_ref[...], preferred_element_type=F32).reshape(tb, SSD_HEADS, SSD_HEAD_DIM)


def _ssd_post_decode_kernel(y_ref, xs_ref, z_ref, d_ref, gn_ref, o_ref):
    y = y_ref[...] + d_ref[...] * xs_ref[...]
    o_ref[...] = _ssd_gate_norm(y, z_ref[...].astype(F32), gn_ref[...]).astype(BF16)


def _ssd_decode_consts():
    pn = SSD_HEAD_DIM * SSD_STATE
    col_p = np.arange(pn) // SSD_STATE
    col_n = np.arange(pn) % SSD_STATE
    rep = (np.arange(SSD_HEAD_DIM)[:, None] == col_p[None, :]).astype(np.float32)
    row_g = np.arange(LANES) // SSD_STATE
    row_n = np.arange(LANES) % SSD_STATE
    tile = np.stack([((row_g[:, None] == g) & (row_n[:, None] == col_n[None, :])).astype(np.float32)
                     for g in range(SSD_GROUPS)])
    summ = rep.T
    return jnp.asarray(rep, BF16), jnp.asarray(tile, BF16), jnp.asarray(summ, BF16)


def _ssd_decode(xbc, dtr, z, layer, buf_all, h0_all, ssd_params, rexp, dec_consts, tb):
    n = xbc.shape[0]
    pn = SSD_HEAD_DIM * SSD_STATE
    cw_all, cb_all, dtb_all, aneg_all, d_all, gn_all = ssd_params
    f = lambda w: jax.ShapeDtypeStruct((n, w), F32)
    one = lambda shape: pl.BlockSpec(shape, lambda i: (0,) * len(shape))
    lay = lambda shape: pl.BlockSpec((None,) + shape, lambda i: (layer,) + (0,) * len(shape))
    xs, xdt, dec, bm, cm, nbuf = pl.pallas_call(
        _ssd_pre_decode_kernel,
        in_specs=[one((n, SSD_CONV_DIM)), one((n, LANES)), lay((n, CONV_K - 1, SSD_CONV_DIM)),
                  lay((CONV_K, SSD_CONV_DIM)), lay((1, SSD_CONV_DIM)), lay((1, LANES)), lay((1, LANES)),
                  one((LANES, SSD_INNER))],
        out_shape=[f(SSD_INNER), f(SSD_INNER), f(SSD_INNER), f(LANES), f(LANES),
                   jax.ShapeDtypeStruct((n, CONV_K - 1, SSD_CONV_DIM), F32)],
        grid=(1,),
        compiler_params=_params(("arbitrary",)),
        name="ssd_pre_decode",
    )(xbc, dtr, buf_all, cw_all, cb_all, dtb_all, aneg_all, rexp)
    rep, tile, summ = dec_consts
    b3 = lambda a, b: pl.BlockSpec((tb, a, b), lambda i: (i, 0, 0))
    b2 = pl.BlockSpec((tb, LANES), lambda i: (i, 0))
    full = lambda shape: pl.BlockSpec(shape, lambda i: (0,) * len(shape))
    state = pl.BlockSpec((None, tb, SSD_HEADS, pn), lambda i: (layer, i, 0, 0))
    h0_all, y = pl.pallas_call(
        functools.partial(_ssd_state_decode_kernel, tb=tb),
        grid=(n // tb,),
        in_specs=[state, b3(SSD_HEADS, SSD_HEAD_DIM), b3(SSD_HEADS, SSD_HEAD_DIM), b2, b2,
                  full((SSD_HEAD_DIM, pn)), full((SSD_GROUPS, LANES, pn)), full((pn, SSD_HEAD_DIM))],
        out_specs=[state, b3(SSD_HEADS, SSD_HEAD_DIM)],
        out_shape=[jax.ShapeDtypeStruct(h0_all.shape, F32),
                   jax.ShapeDtypeStruct((n, SSD_HEADS, SSD_HEAD_DIM), F32)],
        input_output_aliases={0: 0},
        compiler_params=_params(("arbitrary",)),
        name="ssd_state_decode",
    )(h0_all, xdt.reshape(n, SSD_HEADS, SSD_HEAD_DIM), dec.reshape(n, SSD_HEADS, SSD_HEAD_DIM), bm, cm,
      rep, tile, summ)
    o = pl.pallas_call(
        _ssd_post_decode_kernel,
        in_specs=[one((n, SSD_INNER)), one((n, SSD_INNER)), one((n, SSD_INNER)),
                  lay((1, SSD_INNER)), lay((1, SSD_INNER))],
        out_shape=jax.ShapeDtypeStruct((n, SSD_INNER), BF16),
        grid=(1,),
        compiler_params=_params(("arbitrary",)),
        name="ssd_post_decode",
    )(y.reshape(n, SSD_INNER), xs, z, d_all, gn_all)
    return o, h0_all, nbuf


def _pack_w_in(w):
    dt0 = C_GT
    dt = jnp.pad(w[..., dt0:dt0 + SSD_HEADS], ((0, 0), (0, 0), (0, LANES - SSD_HEADS)))
    return jnp.concatenate([w[..., :dt0], w[..., dt0 + SSD_HEADS:], dt], axis=-1).astype(BF16)


def kernel(x_prompt, x_sample, c_prompt, c_sample, cache_k, cache_v, state_lru_h, state_lru_conv, state_ssd_h, state_ssd_conv, w_mod, b_mod, g_norm_mix, g_norm_ffn, w_in, g_q, g_k, attn_sinks, lru_conv_w, lru_conv_b, lru_w_a, lru_b_a, lru_w_x, lru_b_x, lru_lambda, ssd_conv_w, ssd_conv_b, ssd_dt_bias, ssd_a_log, ssd_d, g_ssd_norm, w_br_attn, w_br_lru, w_br_ssd, w_out, ffn_w_gate, ffn_w_up, ffn_w_down, moe_w_router, moe_b_router, moe_w_gate, moe_w_up, moe_w_down):
    bp, seq = x_prompt.shape[:2]
    ns = x_sample.shape[0]
    win = cache_k.shape[2]
    assert x_sample.shape[1] == 1
    assert win == WINDOW and PAST_LEN >= WINDOW
    mp = bp * seq
    tm_p, tm_s = 512, ns
    lru_rows = 256
    pn = SSD_HEAD_DIM * SSD_STATE

    mod = _modulation(jnp.concatenate([c_sample, c_prompt], axis=0), w_mod, b_mod)
    mod_s = _Mod(mod, True, ns, 1)
    mod_p = _Mod(mod[:, ns:].reshape(DEPTH * bp * 6, 1, D_MODEL), False, bp, seq)
    row3 = lambda v: v.astype(F32)[:, None, :]
    w_packed = _pack_w_in(w_in)
    g_mix, g_ffn = row3(g_norm_mix), row3(g_norm_ffn)
    gq2, gk2 = row3(jnp.tile(g_q, (1, 2))), row3(jnp.tile(g_k, (1, 2)))
    sinks = attn_sinks.astype(F32)
    sinks_col = sinks[:, :, None]
    e2 = _head_avg_matrix()
    tab_p = _rope_tables(jnp.arange(seq))
    tab_s = _rope_tables(jnp.full((1,), PAST_LEN))
    wa, wx, negsp = _lru_weight_stack(lru_w_a, lru_w_x, lru_lambda)
    lru_params = (lru_conv_w.astype(F32), row3(lru_conv_b), wa, row3(lru_b_a), wx, row3(lru_b_x), negsp)
    ssd_params = _ssd_param_stack(ssd_conv_w, ssd_conv_b, ssd_dt_bias, ssd_a_log, ssd_d, g_ssd_norm)
    rexp = _head_expand_matrix()
    dec_consts = _ssd_decode_consts()
    br = (w_br_attn.astype(BF16), w_br_lru.astype(BF16), w_br_ssd.astype(BF16), w_out.astype(BF16))
    wr_all, br_all = _router_stack(moe_w_router, moe_b_router)
    n_moe = moe_w_gate.shape[0]
    flat = lambda w: w.astype(BF16).reshape((n_moe * N_EXPERTS,) + w.shape[2:])
    moe_w = (flat(moe_w_gate), flat(moe_w_up), flat(moe_w_down))
    ffn_w = (ffn_w_gate.astype(BF16), ffn_w_up.astype(BF16), ffn_w_down.astype(BF16))
    ck_all = cache_k.reshape(DEPTH, ns, win, KV_WIDTH)
    cv_all = cache_v.reshape(DEPTH, ns, win, KV_WIDTH)
    ssd_h0_all = state_ssd_h.reshape(DEPTH, ns, SSD_HEADS, pn)
    sorted_p = jnp.zeros((((mp * TOP_K) // tm_p + N_EXPERTS) * tm_p, D_MODEL), F32)
    sorted_s = jnp.zeros((((ns * TOP_K) // tm_s + N_EXPERTS) * tm_s, D_MODEL), F32)

    xp = x_prompt.reshape(mp, D_MODEL)
    xs = x_sample.reshape(ns, D_MODEL)
    outs_p = [[] for _ in range(6)]
    outs_s = [[] for _ in range(3)]
    attn_bias = _attn_bias()

    for l in range(DEPTH):
        is_moe = l % 2 == 1
        j = l // 2
        router = (wr_all, br_all, j) if is_moe else None

        qkv, lx, lg, z, xbc, gt, dtr = _in_proj(xp, l, g_mix, mod_p, w_packed, tm_p)
        o_attn, k_win, v_win = _attention_prompt(qkv, l, sinks, attn_bias, tab_p, gq2, gk2, e2, bp, seq)
        o_lru, lru_h, lru_buf = _lru_prompt(lx, lg, l, lru_params, bp, seq, lru_rows)
        o_ssd, ssd_st, ssd_buf = _ssd_prompt(xbc, dtr, z, l, ssd_params, rexp, bp, seq)
        merged = _merge(o_attn, o_lru, o_ssd, gt, xp, l, mod_p, g_ffn, *br, router, tm_p)
        if is_moe:
            x1, h2, logits = merged
            xp, sorted_p = _moe(h2, logits, x1, l, mod_p, *moe_w, j * N_EXPERTS, sorted_p, tm_p, 512)
        else:
            x1, h2 = merged
            nt = mp // tm_p
            xp = _ffn(h2, *ffn_w, jnp.zeros((nt,), I32), jnp.full((1,), nt, I32), tm_p,
                      expert_base=j, residual=x1, mod=mod_p, layer=l)
        for acc, t in zip(outs_p, (k_win, v_win, lru_h, lru_buf, ssd_st, ssd_buf)):
            acc.append(t)

        qkv, lx, lg, z, xbc, gt, dtr = _in_proj(xs, l, g_mix, mod_s, w_packed, tm_s, (tab_s, gq2, gk2, e2))
        o_attn, ck_all, cv_all = _attention_decode(qkv, l, ck_all, cv_all, sinks_col, 32)
        o_lru, lru_h, lru_buf = _lru_decode(lx, lg, l, state_lru_conv, state_lru_h, lru_params)
        o_ssd, ssd_h0_all, ssd_buf = _ssd_decode(xbc, dtr, z, l, state_ssd_conv, ssd_h0_all, ssd_params, rexp,
                                                 dec_consts, 16)
        merged = _merge(o_attn, o_lru, o_ssd, gt, xs, l, mod_s, g_ffn, *br, router, tm_s)
        if is_moe:
            x1, h2, logits = merged
            xs, sorted_s = _moe(h2, logits, x1, l, mod_s, *moe_w, j * N_EXPERTS, sorted_s, tm_s, ns)
        else:
            x1, h2 = merged
            xs = _ffn(h2, *ffn_w, jnp.zeros((1,), I32), jnp.ones((1,), I32), tm_s,
                      expert_base=j, residual=x1, mod=mod_s, layer=l)
        for acc, t in zip(outs_s, (lru_h, lru_buf, ssd_buf)):
            acc.append(t)

    st = [jnp.stack(a) for a in outs_p]
    ss = [jnp.stack(a) for a in outs_s]
    kv5 = lambda t, b: t.reshape(DEPTH, b, WINDOW, N_KV_HEADS, HEAD_DIM)
    return (xp.reshape(bp, seq, D_MODEL), xs.reshape(ns, 1, D_MODEL),
            kv5(st[0], bp), kv5(st[1], bp), st[2].reshape(DEPTH, bp, LRU_WIDTH), st[3],
            _ssd_state_to_hpn(st[4]), st[5],
            kv5(ck_all, ns), kv5(cv_all, ns), ss[0], ss[1],
            ssd_h0_all.reshape(DEPTH, ns, SSD_HEADS, SSD_HEAD_DIM, SSD_STATE), ss[2])
```

```python
import functools
import math

import numpy as np
import jax
import jax.numpy as jnp
from jax import lax
from jax.experimental import pallas as pl
from jax.experimental.pallas import tpu as pltpu

F32 = jnp.float32
BF16 = jnp.bfloat16
I32 = jnp.int32

D_MODEL = 1024
DEPTH = 4
PAST_LEN = 8192
N_Q_HEADS = 8
N_KV_HEADS = 2
HEAD_DIM = 64
GQA_GROUP = N_Q_HEADS // N_KV_HEADS
WINDOW = 128
ATTN_BLOCK = 128
ROT_DIM = HEAD_DIM // 4
ROPE_THETA = 500000.0
ATTN_WIDTH = N_Q_HEADS * HEAD_DIM
KV_WIDTH = N_KV_HEADS * HEAD_DIM
LRU_WIDTH = 512
LRU_BLOCKS = 8
LRU_BLOCK_DIM = LRU_WIDTH // LRU_BLOCKS
LRU_C = 8.0
CONV_K = 4
SSD_HEADS = 8
SSD_HEAD_DIM = 64
SSD_INNER = SSD_HEADS * SSD_HEAD_DIM
SSD_GROUPS = 2
SSD_STATE = 64
SSD_CHUNK = 128
SSD_CONV_DIM = SSD_INNER + 2 * SSD_GROUPS * SSD_STATE
D_FF = 2816
N_EXPERTS = 8
TOP_K = 2
EPS = 1e-6

LANES = 128
SUBLANES = 8
QKV_W = ATTN_WIDTH + 2 * KV_WIDTH
GATES_W = 3 * D_MODEL
C_QKV = 0
C_LX = C_QKV + QKV_W
C_LG = C_LX + LRU_WIDTH
C_Z = C_LG + LRU_WIDTH
C_XBC = C_Z + SSD_INNER
C_GT = C_XBC + SSD_CONV_DIM
C_DT = C_GT + GATES_W
IN_PACKED = C_DT + LANES
FF_CHUNK = D_FF // 2
VMEM_LIMIT = 56 * 1024 * 1024


def _params(sem=None):
    if sem is None:
        return pltpu.CompilerParams(vmem_limit_bytes=VMEM_LIMIT)
    return pltpu.CompilerParams(dimension_semantics=sem, vmem_limit_bytes=VMEM_LIMIT)


def _layer_spec(shape, layer, ngrid):
    zeros = (0,) * len(shape)
    return pl.BlockSpec((None,) + tuple(shape), lambda *g: (layer,) + zeros)


def _sigmoid(x):
    return 0.5 * jnp.tanh(0.5 * x) + 0.5


def _silu(x):
    return x * _sigmoid(x)


def _softplus(x):
    return jnp.maximum(x, 0.0) + jnp.log(1.0 + jnp.exp(-jnp.abs(x)))


def _gelu_tanh(x):
    return 0.5 * x * (1.0 + jnp.tanh(math.sqrt(2.0 / math.pi) * (x + 0.044715 * (x * x * x))))


def _split3(a):
    a1 = a.astype(BF16)
    r1 = a - a1.astype(F32)
    a2 = r1.astype(BF16)
    a3 = (r1 - a2.astype(F32)).astype(BF16)
    return a1, a2, a3


def _dot_exact_rhs(a, m):
    a1, a2, a3 = _split3(a)
    d = lambda t: jnp.dot(t, m, preferred_element_type=F32)
    return d(a1) + d(a2) + d(a3)


def _dot_exact_lhs(m, a):
    a1, a2, a3 = _split3(a)
    d = lambda t: jnp.dot(m, t, preferred_element_type=F32)
    return d(a1) + d(a2) + d(a3)


def _rms(x, g):
    return x * lax.rsqrt(jnp.mean(x * x, axis=-1, keepdims=True) + EPS) * g


def _mod_kernel(c_ref, w_ref, b_ref, o_ref):
    s = _silu(c_ref[...]).astype(BF16)
    o_ref[...] = jnp.dot(s, w_ref[...].astype(BF16), preferred_element_type=F32) + b_ref[...]


def _modulation(c, w_mod, b_mod):
    rows = c.shape[0]
    tn = 1536
    nt = (6 * D_MODEL) // tn
    return pl.pallas_call(
        _mod_kernel,
        grid=(DEPTH, nt),
        in_specs=[
            pl.BlockSpec((rows, D_MODEL), lambda l, j: (0, 0)),
            pl.BlockSpec((None, D_MODEL, tn), lambda l, j: (l, 0, j)),
            pl.BlockSpec((None, 1, tn), lambda l, j: (l, 0, j)),
        ],
        out_specs=pl.BlockSpec((None, rows, tn), lambda l, j: (l, 0, j)),
        out_shape=jax.ShapeDtypeStruct((DEPTH, rows, 6 * D_MODEL), F32),
        compiler_params=_params(("arbitrary", "arbitrary")),
        name="modulation",
    )(c, w_mod, b_mod.reshape(DEPTH, 1, 6 * D_MODEL))


class _Mod:
    def __init__(self, arr, per_row, batch, seq_len):
        self.arr, self.per_row, self.batch, self.seq_len = arr, per_row, batch, seq_len

    def spec(self, layer, k, tm, nprefetch=0):
        if self.per_row:
            return pl.BlockSpec((None, tm, D_MODEL), lambda i, *_: (layer, i, k))
        b, s = self.batch, self.seq_len
        return pl.BlockSpec((None, 1, D_MODEL), lambda i, *_: ((layer * b + (i * tm) // s) * 6 + k, 0, 0))


def _in_kernel(*refs, qk_norm):
    x_ref, g_ref, sc_ref, sh_ref, w_ref = refs[:5]
    qkv_ref, lx_ref, lg_ref, z_ref, xbc_ref, gt_ref, dt_ref = refs[-7:]
    h = (_rms(x_ref[...], g_ref[...]) * (1.0 + sc_ref[...]) + sh_ref[...]).astype(BF16)

    def mm(lo, hi):
        return jnp.dot(h, w_ref[:, lo:hi], preferred_element_type=F32)

    if qk_norm:
        cos_ref, slo_ref, shi_ref, gq_ref, gk_ref, e2_ref = refs[5:11]
        cos_t, slo, shi, e2 = cos_ref[...], slo_ref[...], shi_ref[...], e2_ref[...]
        for j in range(ATTN_WIDTH // LANES):
            q_slab = _head_norm_rope(mm(j * LANES, (j + 1) * LANES), gq_ref[...], e2, cos_t, slo, shi)
            qkv_ref[:, j * LANES:(j + 1) * LANES] = (q_slab * (HEAD_DIM ** -0.5)).astype(BF16)
        k_slab = _head_norm_rope(mm(ATTN_WIDTH, ATTN_WIDTH + KV_WIDTH), gk_ref[...], e2, cos_t, slo, shi)
        qkv_ref[:, ATTN_WIDTH:ATTN_WIDTH + KV_WIDTH] = k_slab.astype(BF16)
        qkv_ref[:, ATTN_WIDTH + KV_WIDTH:] = mm(ATTN_WIDTH + KV_WIDTH, C_LX).astype(BF16)
    else:
        qkv_ref[...] = mm(C_QKV, C_LX).astype(BF16)
    lx_ref[...] = mm(C_LX, C_LG)
    lg_ref[...] = mm(C_LG, C_Z).astype(BF16)
    z_ref[...] = mm(C_Z, C_XBC).astype(BF16)
    xbc_ref[...] = mm(C_XBC, C_GT)
    gt_ref[...] = mm(C_GT, C_DT).astype(BF16)
    dt_ref[...] = mm(C_DT, IN_PACKED)


def _in_proj(x, layer, g_all, mod, w_all, tm, qk_norm=None):
    m = x.shape[0]
    row = lambda w: pl.BlockSpec((tm, w), lambda i: (i, 0))
    in_specs = [row(D_MODEL), _layer_spec((1, D_MODEL), layer, 1), mod.spec(layer, 1, tm), mod.spec(layer, 0, tm),
                _layer_spec((D_MODEL, IN_PACKED), layer, 1)]
    args = [x, g_all, mod.arr, mod.arr, w_all]
    if qk_norm is not None:
        tables, gq_all, gk_all, e2 = qk_norm
        tab = pl.BlockSpec((1, LANES), lambda i: (0, 0))
        in_specs += [tab, tab, tab, _layer_spec((1, LANES), layer, 1), _layer_spec((1, LANES), layer, 1),
                     pl.BlockSpec((LANES, LANES), lambda i: (0, 0))]
        args += [*tables, gq_all, gk_all, e2]
    return pl.pallas_call(
        functools.partial(_in_kernel, qk_norm=qk_norm is not None),
        grid=(m // tm,),
        in_specs=in_specs,
        out_specs=[row(QKV_W), row(LRU_WIDTH), row(LRU_WIDTH), row(SSD_INNER),
                   row(SSD_CONV_DIM), row(GATES_W), row(LANES)],
        out_shape=[jax.ShapeDtypeStruct((m, QKV_W), BF16),
                   jax.ShapeDtypeStruct((m, LRU_WIDTH), F32),
                   jax.ShapeDtypeStruct((m, LRU_WIDTH), BF16),
                   jax.ShapeDtypeStruct((m, SSD_INNER), BF16),
                   jax.ShapeDtypeStruct((m, SSD_CONV_DIM), F32),
                   jax.ShapeDtypeStruct((m, GATES_W), BF16),
                   jax.ShapeDtypeStruct((m, LANES), F32)],
        compiler_params=_params(("arbitrary",)),
        name="in_proj",
    )(*args)


def _head_norm_rope(t, g2, e2, cos_t, sin_lo, sin_hi):
    sq = t * t
    hi = sq.astype(BF16)
    lo = (sq - hi.astype(F32)).astype(BF16)
    ms = jnp.dot(hi, e2, preferred_element_type=F32) + jnp.dot(lo, e2, preferred_element_type=F32)
    tn = t * lax.rsqrt(ms + EPS) * g2
    up = pltpu.roll(tn, LANES - ROT_DIM // 2, 1)
    dn = pltpu.roll(tn, ROT_DIM // 2, 1)
    return tn * cos_t + up * sin_lo + dn * sin_hi


def _rope_tables(pos):
    half = ROT_DIM // 2
    inv_freq = ROPE_THETA ** (-jnp.arange(0, ROT_DIM, 2, dtype=F32) / ROT_DIM)
    ang = pos.astype(F32)[:, None] * inv_freq[None, :]
    cos, sin = jnp.cos(ang), jnp.sin(ang)
    n = pos.shape[0]
    ones = jnp.ones((n, HEAD_DIM - ROT_DIM), F32)
    zeros = jnp.zeros((n, HEAD_DIM - ROT_DIM), F32)
    zh = jnp.zeros((n, half), F32)
    cos_t = jnp.concatenate([cos, cos, ones], axis=1)
    sin_lo = jnp.concatenate([-sin, zh, zeros], axis=1)
    sin_hi = jnp.concatenate([zh, sin, zeros], axis=1)
    rep = lambda t: jnp.concatenate([t, t], axis=1)
    return rep(cos_t), rep(sin_lo), rep(sin_hi)


def _head_avg_matrix():
    r = np.arange(LANES)
    return jnp.asarray((r[:, None] // HEAD_DIM == r[None, :] // HEAD_DIM).astype(np.float32) / HEAD_DIM, dtype=BF16)


def _place_kv(t, fill_left, fill_right):
    lane = lax.broadcasted_iota(I32, t.shape, 1)
    left = lane < HEAD_DIM
    sw = pltpu.roll(t, HEAD_DIM, 1)
    return [jnp.where(left, t, fill_left).astype(BF16), jnp.where(left, fill_right, sw).astype(BF16),
            jnp.where(left, sw, fill_left).astype(BF16), jnp.where(left, fill_right, t).astype(BF16)]


def _attn_kernel(sink_ref, qkv_ref, bias0_ref, bias1_ref, cos_ref, slo_ref, shi_ref, gq_ref, gk_ref, e2_ref,
                 o_ref, kwin_ref, vwin_ref, kvar, vvar, *, layer):
    @pl.when(pl.program_id(1) == 0)
    def _():
        kvar[...] = jnp.zeros_like(kvar)
        vvar[...] = jnp.zeros_like(vvar)

    k_f, v_f = _attn_rows(sink_ref, qkv_ref, 0, bias0_ref[...], bias1_ref[...], cos_ref[...], slo_ref[...],
                          shi_ref[...], gq_ref[...], gk_ref[...], e2_ref[...], o_ref, kvar, vvar, layer)
    kwin_ref[...] = k_f[ATTN_BLOCK:]
    vwin_ref[...] = v_f[ATTN_BLOCK:]


def _attn_rows(sink_ref, qkv_ref, r0, bias0, bias1, cos_t, slo, shi, gq, gk, e2, o_ref, kvar, vvar, layer):
    blk = ATTN_BLOCK
    tile = slice(r0, r0 + 2 * blk)
    lane = lax.broadcasted_iota(I32, (blk, LANES), 1)
    zero = jnp.zeros((blk, LANES), F32)
    one_mid = jnp.where(lane == HEAD_DIM, 1.0, 0.0)
    one_first = jnp.where(lane == 0, 1.0, 0.0)
    k_f = _head_norm_rope(qkv_ref[tile, ATTN_WIDTH:ATTN_WIDTH + KV_WIDTH].astype(F32), gk, e2, cos_t, slo, shi)
    v_f = qkv_ref[tile, ATTN_WIDTH + KV_WIDTH:].astype(F32)
    q_n = []
    for j in range(ATTN_WIDTH // LANES):
        q_slab = _head_norm_rope(qkv_ref[tile, j * LANES:(j + 1) * LANES].astype(F32), gq, e2, cos_t, slo, shi)
        q_n.append((q_slab * (HEAD_DIM ** -0.5)).astype(BF16))
    k_blocks = [[kvar[i] for i in range(4)]]
    v_blocks = [[vvar[i] for i in range(4)]]
    for qb in range(2):
        k_blocks.append(_place_kv(k_f[qb * blk:(qb + 1) * blk], zero, zero))
        v_blocks.append(_place_kv(v_f[qb * blk:(qb + 1) * blk], one_mid, one_first))

    row2 = lax.broadcasted_iota(I32, (2 * blk, 1), 0)
    left2 = lax.broadcasted_iota(I32, (2 * blk, LANES), 1) < HEAD_DIM
    for qb in range(2):
        bias = bias0 if qb == 0 else bias1
        rows = slice(qb * blk, (qb + 1) * blk)
        orow = slice(r0 + qb * blk, r0 + (qb + 1) * blk)
        for g in range(N_KV_HEADS):
            q2 = jnp.concatenate([q_n[2 * g][rows], q_n[2 * g + 1][rows]], axis=0)
            acc = None
            for side in range(2):
                var = 2 * g + side
                keys = jnp.concatenate([k_blocks[qb][var], k_blocks[qb + 1][var]], axis=0)
                vals = jnp.concatenate([v_blocks[qb][var], v_blocks[qb + 1][var]], axis=0)
                s = lax.dot_general(q2, keys, (((1,), (1,)), ((), ())), preferred_element_type=F32) + bias
                h_top = GQA_GROUP * g + side
                sink = jnp.where(row2 < blk, sink_ref[layer, h_top], sink_ref[layer, h_top + 2])
                m = jnp.maximum(jnp.max(s, axis=-1, keepdims=True), sink)
                p = jnp.exp(s - m).astype(BF16)
                o = jnp.dot(p, vals, preferred_element_type=F32)
                ones_lane = HEAD_DIM if side == 0 else 0
                den = o[:, ones_lane:ones_lane + 1] + jnp.exp(sink - m)
                o = o * (1.0 / den)
                o = jnp.where(left2, o, 0.0) if side == 0 else jnp.where(left2, 0.0, o)
                acc = o if acc is None else acc + o
            o_ref[orow, (2 * g) * LANES:(2 * g + 1) * LANES] = acc[:blk].astype(BF16)
            o_ref[orow, (2 * g + 1) * LANES:(2 * g + 2) * LANES] = acc[blk:].astype(BF16)

    for i in range(4):
        kvar[i] = k_blocks[2][i]
        vvar[i] = v_blocks[2][i]
    return k_f, v_f


def _attn_bias():
    r = np.arange(2 * ATTN_BLOCK)[:, None] % ATTN_BLOCK
    c = np.arange(2 * ATTN_BLOCK)[None, :]
    band = (c >= r) & (c <= r + WINDOW)
    first = band & (c >= ATTN_BLOCK)
    to_bias = lambda ok: np.where(ok, 0.0, -np.inf).astype(np.float32)
    return jnp.asarray(np.stack([to_bias(first), to_bias(band)]))


def _attention_prompt(qkv, layer, sinks_all, bias, tables, gq_all, gk_all, e2, batch, seq_len):
    tq = 2 * ATTN_BLOCK
    nb = seq_len // tq
    win = pl.BlockSpec((None, ATTN_BLOCK, KV_WIDTH), lambda b, n: (b, 0, 0))
    tab = pl.BlockSpec((tq, LANES), lambda b, n: (n, 0))
    return pl.pallas_call(
        functools.partial(_attn_kernel, layer=layer),
        grid=(batch, nb),
        in_specs=[pl.BlockSpec(memory_space=pltpu.SMEM),
                  pl.BlockSpec((tq, QKV_W), lambda b, n: (b * nb + n, 0)),
                  pl.BlockSpec((None, tq, tq), lambda b, n: (jnp.minimum(n, 1), 0, 0)),
                  pl.BlockSpec((None, tq, tq), lambda b, n: (1, 0, 0)),
                  tab, tab, tab, _layer_spec((1, LANES), layer, 2), _layer_spec((1, LANES), layer, 2),
                  pl.BlockSpec((LANES, LANES), lambda b, n: (0, 0))],
        out_specs=[pl.BlockSpec((tq, ATTN_WIDTH), lambda b, n: (b * nb + n, 0)), win, win],
        out_shape=[jax.ShapeDtypeStruct((batch * seq_len, ATTN_WIDTH), BF16),
                   jax.ShapeDtypeStruct((batch, ATTN_BLOCK, KV_WIDTH), F32),
                   jax.ShapeDtypeStruct((batch, ATTN_BLOCK, KV_WIDTH), F32)],
        scratch_shapes=[pltpu.VMEM((4, ATTN_BLOCK, LANES), BF16), pltpu.VMEM((4, ATTN_BLOCK, LANES), BF16)],
        compiler_params=_params(("arbitrary", "arbitrary")),
        name="attn_prompt",
    )(sinks_all, qkv, bias, bias, *tables, gq_all, gk_all, e2)


def _conv_chunk(tail, x, w_ref, b_ref, rows):
    width = x.shape[1]
    x3 = x.reshape(rows // SUBLANES, SUBLANES, width)
    full = jnp.concatenate([tail[...][None], x3], axis=0)
    t = lax.broadcasted_iota(I32, x3.shape, 1)
    y = x3 * w_ref[CONV_K - 1:CONV_K, :] + b_ref[...]
    for j in range(1, CONV_K):
        rot = pltpu.roll(full, j, 1)
        y = y + jnp.where(t >= j, rot[1:], rot[:-1]) * w_ref[CONV_K - 1 - j:CONV_K - j, :]
    return y.reshape(rows, width)


def _lru_gates(xc, wa_ref, ba_ref, wx_ref, bx_ref, negsp_ref):
    xb = xc.astype(BF16)
    ra, ri = [], []
    for s in range(LRU_WIDTH // LANES):
        sl = slice(s * LANES, (s + 1) * LANES)
        ra.append(jnp.dot(xb[:, sl], wa_ref[s], preferred_element_type=F32))
        ri.append(jnp.dot(xb[:, sl], wx_ref[s], preferred_element_type=F32))
    r = _sigmoid(jnp.concatenate(ra, axis=1) + ba_ref[...])
    i = _sigmoid(jnp.concatenate(ri, axis=1) + bx_ref[...])
    log_a = r * negsp_ref[...]
    a = jnp.exp(log_a)
    u = jnp.sqrt(1.0 - a * a) * (i * xc)
    return a, u


def _lru_scan(x, gate, cw_ref, cb_ref, wa_ref, ba_ref, wx_ref, bx_ref, negsp_ref, xbuf, hcar, rows):
    xc = _conv_chunk(xbuf, x, cw_ref, cb_ref, rows)
    a, u = _lru_gates(xc, wa_ref, ba_ref, wx_ref, bx_ref, negsp_ref)

    ng = rows // SUBLANES
    a = a.reshape(ng, SUBLANES, LRU_WIDTH)
    u = u.reshape(ng, SUBLANES, LRU_WIDTH)
    t = lax.broadcasted_iota(I32, a.shape, 1)
    d = 1
    while d < SUBLANES:
        keep = t >= d
        a_s = jnp.where(keep, pltpu.roll(a, d, 1), 1.0)
        u_s = jnp.where(keep, pltpu.roll(u, d, 1), 0.0)
        u = a * u_s + u
        a = a * a_s
        d *= 2
    carry = hcar[...]
    groups = []
    for r in range(ng):
        hg = a[r] * carry + u[r]
        groups.append(hg)
        carry = hg[SUBLANES - 1:SUBLANES, :]
    h = jnp.concatenate(groups, axis=0)
    hcar[...] = carry
    xbuf[...] = x[rows - SUBLANES:rows, :]
    return h * _gelu_tanh(gate), carry


def _lru_kernel(x_ref, gate_ref, cw_ref, cb_ref, wa_ref, ba_ref, wx_ref, bx_ref, negsp_ref,
                o_ref, h_ref, buf_ref, xbuf, hcar, *, rows):
    @pl.when(pl.program_id(1) == 0)
    def _():
        xbuf[...] = jnp.zeros_like(xbuf)
        hcar[...] = jnp.zeros_like(hcar)

    x = x_ref[...]
    o, carry = _lru_scan(x, gate_ref[...].astype(F32), cw_ref, cb_ref, wa_ref, ba_ref, wx_ref, bx_ref, negsp_ref,
                         xbuf, hcar, rows)
    o_ref[...] = o.astype(BF16)
    h_ref[...] = carry
    buf_ref[...] = x[rows - (CONV_K - 1):rows, :]


def _lru_weight_stack(lru_w_a, lru_w_x, lru_lambda):
    def pair(w):
        w = w.astype(BF16).reshape(DEPTH, LRU_BLOCKS // 2, 2, LRU_BLOCK_DIM, LRU_BLOCK_DIM)
        z = jnp.zeros((DEPTH, LRU_BLOCKS // 2, LRU_BLOCK_DIM, LRU_BLOCK_DIM), BF16)
        top = jnp.concatenate([w[:, :, 0], z], axis=3)
        bot = jnp.concatenate([z, w[:, :, 1]], axis=3)
        return jnp.concatenate([top, bot], axis=2)
    return pair(lru_w_a), pair(lru_w_x), (-LRU_C * jax.nn.softplus(-lru_lambda.astype(F32)))[:, None, :]


def _lru_param_specs(layer, ngrid):
    ls = lambda shape: _layer_spec(shape, layer, ngrid)
    nb = LRU_WIDTH // LANES
    return [ls((CONV_K, LRU_WIDTH)), ls((1, LRU_WIDTH)), ls((nb, LANES, LANES)), ls((1, LRU_WIDTH)),
            ls((nb, LANES, LANES)), ls((1, LRU_WIDTH)), ls((1, LRU_WIDTH))]


def _lru_prompt(lx, lgate, layer, lru_params, batch, seq_len, rows):
    nc = seq_len // rows
    tok = lambda w: pl.BlockSpec((rows, w), lambda b, n: (b * nc + n, 0))
    return pl.pallas_call(
        functools.partial(_lru_kernel, rows=rows),
        grid=(batch, nc),
        in_specs=[tok(LRU_WIDTH), tok(LRU_WIDTH)] + _lru_param_specs(layer, 2),
        out_specs=[tok(LRU_WIDTH),
                   pl.BlockSpec((None, 1, LRU_WIDTH), lambda b, n: (b, 0, 0)),
                   pl.BlockSpec((None, CONV_K - 1, LRU_WIDTH), lambda b, n: (b, 0, 0))],
        out_shape=[jax.ShapeDtypeStruct((batch * seq_len, LRU_WIDTH), BF16),
                   jax.ShapeDtypeStruct((batch, 1, LRU_WIDTH), F32),
                   jax.ShapeDtypeStruct((batch, CONV_K - 1, LRU_WIDTH), F32)],
        scratch_shapes=[pltpu.VMEM((SUBLANES, LRU_WIDTH), F32), pltpu.VMEM((1, LRU_WIDTH), F32)],
        compiler_params=_params(("arbitrary", "arbitrary")),
        name="lru_prompt",
    )(lx, lgate, *lru_params)


def _in_lru_kernel(x_ref, g_ref, sc_ref, sh_ref, w_ref, cw_ref, cb_ref, wa_ref, ba_ref, wx_ref, bx_ref, negsp_ref,
                   qkv_ref, z_ref, xbc_ref, gt_ref, dt_ref, ol_ref, lh_ref, lbuf_ref, xbuf, hcar,
                   *, rows, tiles_per_seq):
    @pl.when(pl.program_id(0) % tiles_per_seq == 0)
    def _():
        xbuf[...] = jnp.zeros_like(xbuf)
        hcar[...] = jnp.zeros_like(hcar)

    h = (_rms(x_ref[...], g_ref[...]) * (1.0 + sc_ref[...]) + sh_ref[...]).astype(BF16)

    def mm(lo, hi):
        return jnp.dot(h, w_ref[:, lo:hi], preferred_element_type=F32)

    lx = mm(C_LX, C_LG)
    gate = mm(C_LG, C_Z)
    qkv_ref[...] = mm(C_QKV, C_LX).astype(BF16)
    z_ref[...] = mm(C_Z, C_XBC).astype(BF16)
    xbc_ref[...] = mm(C_XBC, C_GT)
    gt_ref[...] = mm(C_GT, C_DT).astype(BF16)
    dt_ref[...] = mm(C_DT, IN_PACKED)
    o, carry = _lru_scan(lx, gate, cw_ref, cb_ref, wa_ref, ba_ref, wx_ref, bx_ref, negsp_ref, xbuf, hcar, rows)
    ol_ref[...] = o.astype(BF16)
    lh_ref[...] = carry
    lbuf_ref[...] = lx[rows - (CONV_K - 1):rows, :]


def _in_proj_lru(x, layer, g_all, mod, w_all, lru_params, tm, batch, seq_len):
    m = x.shape[0]
    tps = seq_len // tm
    row = lambda w: pl.BlockSpec((tm, w), lambda i: (i, 0))
    per_seq = lambda r, w: pl.BlockSpec((None, r, w), lambda i: (i // tps, 0, 0))
    return pl.pallas_call(
        functools.partial(_in_lru_kernel, rows=tm, tiles_per_seq=tps),
        grid=(m // tm,),
        in_specs=[row(D_MODEL), _layer_spec((1, D_MODEL), layer, 1), mod.spec(layer, 1, tm), mod.spec(layer, 0, tm),
                  _layer_spec((D_MODEL, IN_PACKED), layer, 1)] + _lru_param_specs(layer, 1),
        out_specs=[row(QKV_W), row(SSD_INNER), row(SSD_CONV_DIM), row(GATES_W), row(LANES), row(LRU_WIDTH),
                   per_seq(1, LRU_WIDTH), per_seq(CONV_K - 1, LRU_WIDTH)],
        out_shape=[jax.ShapeDtypeStruct((m, QKV_W), BF16),
                   jax.ShapeDtypeStruct((m, SSD_INNER), BF16),
                   jax.ShapeDtypeStruct((m, SSD_CONV_DIM), F32),
                   jax.ShapeDtypeStruct((m, GATES_W), BF16),
                   jax.ShapeDtypeStruct((m, LANES), F32),
                   jax.ShapeDtypeStruct((m, LRU_WIDTH), BF16),
                   jax.ShapeDtypeStruct((batch, 1, LRU_WIDTH), F32),
                   jax.ShapeDtypeStruct((batch, CONV_K - 1, LRU_WIDTH), F32)],
        scratch_shapes=[pltpu.VMEM((SUBLANES, LRU_WIDTH), F32), pltpu.VMEM((1, LRU_WIDTH), F32)],
        compiler_params=_params(("arbitrary",)),
        name="in_proj_lru",
    )(x, g_all, mod.arr, mod.arr, w_all, *lru_params)


def _ssd_gate_norm(y, z, gn):
    u = y * _silu(z)
    gw = SSD_INNER // SSD_GROUPS
    parts = []
    for g in range(SSD_GROUPS):
        ug = u[:, g * gw:(g + 1) * gw]
        parts.append(ug * lax.rsqrt(jnp.mean(ug * ug, axis=-1, keepdims=True) + EPS))
    return jnp.concatenate(parts, axis=1) * gn


def _ssd_chunk(xbc, dt_raw, z, cw_ref, cb_ref, dtb_ref, aneg_ref, d_ref, gn_ref, rexp_ref, xbuf, state):
    q = SSD_CHUNK
    gw = SSD_INNER // SSD_GROUPS
    xc = _silu(_conv_chunk(xbuf, xbc, cw_ref, cb_ref, q))
    xs = xc[:, :SSD_INNER]
    bm = xc[:, SSD_INNER:SSD_INNER + LANES]
    cm = xc[:, SSD_INNER + LANES:]

    rexp = rexp_ref[...]
    dt = _softplus(dt_raw + dtb_ref[...])
    dta = dt * aneg_ref[...]
    li = lax.broadcasted_iota(I32, (q, q), 0)
    si = lax.broadcasted_iota(I32, (q, q), 1)
    causal = li >= si
    tri = jnp.where(causal, 1.0, 0.0).astype(BF16)
    a_cs = _dot_exact_lhs(tri, dta)
    a_cs_t = a_cs.T
    dt_e = _dot_exact_rhs(dt, rexp)
    a_e = _dot_exact_rhs(a_cs, rexp)
    a_last = a_e[q - 1:q, :]

    xdt = xs * dt_e
    lane = lax.broadcasted_iota(I32, (q, LANES), 1)
    left = lane < SSD_STATE
    zero = jnp.zeros((q, LANES), F32)
    cmb = cm.astype(BF16)
    bmb = bm.astype(BF16)
    cm_g = (jnp.where(left, cm, zero).astype(BF16), jnp.where(left, zero, cm).astype(BF16))
    scores = [lax.dot_general(cm_g[g], bmb, (((1,), (1,)), ((), ())), preferred_element_type=F32)
              for g in range(SSD_GROUPS)]

    y_off = jnp.dot(cmb, state[...].astype(BF16), preferred_element_type=F32) * jnp.exp(a_e)

    ys = []
    for j in range(SSD_INNER // LANES):
        xs_slab = xdt[:, j * LANES:(j + 1) * LANES]
        x_side = (jnp.where(left, xs_slab, zero).astype(BF16), jnp.where(left, zero, xs_slab).astype(BF16))
        acc = y_off[:, j * LANES:(j + 1) * LANES]
        for side in range(2):
            h = 2 * j + side
            g = h // (SSD_HEADS // SSD_GROUPS)
            col = a_e[:, h * SSD_HEAD_DIM:h * SSD_HEAD_DIM + 1]
            row = a_cs_t[h:h + 1, :]
            decay = jnp.exp(jnp.where(causal, col - row, -jnp.inf))
            acc = acc + jnp.dot((scores[g] * decay).astype(BF16), x_side[side], preferred_element_type=F32)
        ys.append(acc + d_ref[:, j * LANES:(j + 1) * LANES] * xs[:, j * LANES:(j + 1) * LANES])
    y = jnp.concatenate(ys, axis=1)
    out = _ssd_gate_norm(y, z, gn_ref[...])

    x_end = (xdt * jnp.exp(a_last - a_e)).astype(BF16)
    upd = jnp.dot(bm.T.astype(BF16), x_end, preferred_element_type=F32)
    rg = lax.broadcasted_iota(I32, (LANES, SSD_INNER), 0) // SSD_STATE
    cg = lax.broadcasted_iota(I32, (LANES, SSD_INNER), 1) // gw
    state[...] = jnp.where(rg == cg, jnp.exp(a_last) * state[...] + upd, 0.0)
    xbuf[...] = xbc[q - SUBLANES:q, :]
    return out


def _ssd_kernel(xbc_ref, dt_ref, z_ref, cw_ref, cb_ref, dtb_ref, aneg_ref, d_ref, gn_ref, rexp_ref,
                o_ref, st_ref, buf_ref, xbuf, state):
    @pl.when(pl.program_id(1) == 0)
    def _():
        xbuf[...] = jnp.zeros_like(xbuf)
        state[...] = jnp.zeros_like(state)

    xbc = xbc_ref[...]
    out = _ssd_chunk(xbc, dt_ref[...], z_ref[...].astype(F32), cw_ref, cb_ref, dtb_ref, aneg_ref, d_ref, gn_ref,
                     rexp_ref, xbuf, state)
    o_ref[...] = out.astype(BF16)
    st_ref[...] = state[...]
    buf_ref[...] = xbc[SSD_CHUNK - (CONV_K - 1):SSD_CHUNK, :]


def _in_mix_kernel(*refs, rows, tiles_per_seq):
    x_ref, g_ref, sc_ref, sh_ref, w_ref = refs[:5]
    lru_p = refs[5:12]
    ssd_p = refs[12:19]
    qkv_ref, gt_ref, ol_ref, lh_ref, lbuf_ref, os_ref, st_ref, sbuf_ref = refs[19:27]
    xbuf_l, hcar, xbuf_s, state = refs[27:]

    @pl.when(pl.program_id(0) % tiles_per_seq == 0)
    def _():
        for s in (xbuf_l, hcar, xbuf_s, state):
            s[...] = jnp.zeros_like(s)

    h = (_rms(x_ref[...], g_ref[...]) * (1.0 + sc_ref[...]) + sh_ref[...]).astype(BF16)

    def mm(lo, hi):
        return jnp.dot(h, w_ref[:, lo:hi], preferred_element_type=F32)

    lx = mm(C_LX, C_LG)
    gate = mm(C_LG, C_Z)
    z = mm(C_Z, C_XBC)
    xbc = mm(C_XBC, C_GT)
    dt = mm(C_DT, IN_PACKED)
    qkv_ref[...] = mm(C_QKV, C_LX).astype(BF16)
    gt_ref[...] = mm(C_GT, C_DT).astype(BF16)
    o, carry = _lru_scan(lx, gate, *lru_p, xbuf_l, hcar, rows)
    ol_ref[...] = o.astype(BF16)
    lh_ref[...] = carry
    lbuf_ref[...] = lx[rows - (CONV_K - 1):rows, :]
    for c in range(rows // SSD_CHUNK):
        sl = slice(c * SSD_CHUNK, (c + 1) * SSD_CHUNK)
        out = _ssd_chunk(xbc[sl], dt[sl], z[sl], *ssd_p, xbuf_s, state)
        os_ref[sl, :] = out.astype(BF16)
    st_ref[...] = state[...]
    sbuf_ref[...] = xbc[rows - (CONV_K - 1):rows, :]


def _in_proj_mix(x, layer, g_all, mod, w_all, lru_params, ssd_params, rexp, tm, batch, seq_len):
    m = x.shape[0]
    tps = seq_len // tm
    row = lambda w: pl.BlockSpec((tm, w), lambda i: (i, 0))
    per_seq = lambda r, w: pl.BlockSpec((None, r, w), lambda i: (i // tps, 0, 0))
    return pl.pallas_call(
        functools.partial(_in_mix_kernel, rows=tm, tiles_per_seq=tps),
        grid=(m // tm,),
        in_specs=[row(D_MODEL), _layer_spec((1, D_MODEL), layer, 1), mod.spec(layer, 1, tm), mod.spec(layer, 0, tm),
                  _layer_spec((D_MODEL, IN_PACKED), layer, 1)] + _lru_param_specs(layer, 1)
                 + _ssd_param_specs(layer, 1) + [pl.BlockSpec((LANES, SSD_INNER), lambda i: (0, 0))],
        out_specs=[row(QKV_W), row(GATES_W), row(LRU_WIDTH), per_seq(1, LRU_WIDTH), per_seq(CONV_K - 1, LRU_WIDTH),
                   row(SSD_INNER), per_seq(LANES, SSD_INNER), per_seq(CONV_K - 1, SSD_CONV_DIM)],
        out_shape=[jax.ShapeDtypeStruct((m, QKV_W), BF16),
                   jax.ShapeDtypeStruct((m, GATES_W), BF16),
                   jax.ShapeDtypeStruct((m, LRU_WIDTH), BF16),
                   jax.ShapeDtypeStruct((batch, 1, LRU_WIDTH), F32),
                   jax.ShapeDtypeStruct((batch, CONV_K - 1, LRU_WIDTH), F32),
                   jax.ShapeDtypeStruct((m, SSD_INNER), BF16),
                   jax.ShapeDtypeStruct((batch, LANES, SSD_INNER), F32),
                   jax.ShapeDtypeStruct((batch, CONV_K - 1, SSD_CONV_DIM), F32)],
        scratch_shapes=[pltpu.VMEM((SUBLANES, LRU_WIDTH), F32), pltpu.VMEM((1, LRU_WIDTH), F32),
                        pltpu.VMEM((SUBLANES, SSD_CONV_DIM), F32), pltpu.VMEM((LANES, SSD_INNER), F32)],
        compiler_params=_params(("arbitrary",)),
        name="in_proj_mix",
    )(x, g_all, mod.arr, mod.arr, w_all, *lru_params, *ssd_params, rexp)


def _ssd_param_stack(ssd_conv_w, ssd_conv_b, ssd_dt_bias, ssd_a_log, ssd_d, g_ssd_norm):
    pad = lambda v: jnp.pad(v.astype(F32), ((0, 0), (0, LANES - SSD_HEADS)))[:, None, :]
    return (ssd_conv_w.astype(F32), ssd_conv_b.astype(F32)[:, None, :], pad(ssd_dt_bias),
            pad(-jnp.exp(ssd_a_log.astype(F32))),
            jnp.repeat(ssd_d.astype(F32), SSD_HEAD_DIM, axis=1)[:, None, :],
            g_ssd_norm.astype(F32)[:, None, :])


def _ssd_param_specs(layer, ngrid):
    ls = lambda shape: _layer_spec(shape, layer, ngrid)
    return [ls((CONV_K, SSD_CONV_DIM)), ls((1, SSD_CONV_DIM)), ls((1, LANES)), ls((1, LANES)),
            ls((1, SSD_INNER)), ls((1, SSD_INNER))]


def _head_expand_matrix():
    h = np.arange(LANES)[:, None]
    c = np.arange(SSD_INNER)[None, :]
    return jnp.asarray((h == c // SSD_HEAD_DIM).astype(np.float32), dtype=BF16)


def _ssd_prompt(xbc, dtr, z, layer, ssd_params, rexp, batch, seq_len):
    nc = seq_len // SSD_CHUNK
    tok = lambda w: pl.BlockSpec((SSD_CHUNK, w), lambda b, n: (b * nc + n, 0))
    return pl.pallas_call(
        _ssd_kernel,
        grid=(batch, nc),
        in_specs=[tok(SSD_CONV_DIM), tok(LANES), tok(SSD_INNER)] + _ssd_param_specs(layer, 2)
                 + [pl.BlockSpec((LANES, SSD_INNER), lambda b, n: (0, 0))],
        out_specs=[tok(SSD_INNER),
                   pl.BlockSpec((None, LANES, SSD_INNER), lambda b, n: (b, 0, 0)),
                   pl.BlockSpec((None, CONV_K - 1, SSD_CONV_DIM), lambda b, n: (b, 0, 0))],
        out_shape=[jax.ShapeDtypeStruct((batch * seq_len, SSD_INNER), BF16),
                   jax.ShapeDtypeStruct((batch, LANES, SSD_INNER), F32),
                   jax.ShapeDtypeStruct((batch, CONV_K - 1, SSD_CONV_DIM), F32)],
        scratch_shapes=[pltpu.VMEM((SUBLANES, SSD_CONV_DIM), F32), pltpu.VMEM((LANES, SSD_INNER), F32)],
        compiler_params=_params(("arbitrary", "arbitrary")),
        name="ssd_prompt",
    )(xbc, dtr, z, *ssd_params, rexp)


def _ssd_state_to_hpn(st):
    lead = st.shape[:-2]
    hpg = SSD_HEADS // SSD_GROUPS
    gw = SSD_INNER // SSD_GROUPS
    parts = []
    for g in range(SSD_GROUPS):
        blk = st[..., g * SSD_STATE:(g + 1) * SSD_STATE, g * gw:(g + 1) * gw]
        blk = blk.reshape(lead + (SSD_STATE, hpg, SSD_HEAD_DIM))
        parts.append(jnp.moveaxis(blk, -3, -1))
    return jnp.concatenate(parts, axis=-3)


def _merge_kernel(*refs, with_router):
    (oa_ref, ol_ref, os_ref, gt_ref, x_ref, g1_ref, sc2_ref, sh2_ref, gn_ref,
     wa_ref, wl_ref, ws_ref, wo_ref) = refs[:13]
    if with_router:
        wr_ref, br_ref, x1_ref, h2_ref, lg_ref = refs[13:]
    else:
        x1_ref, h2_ref = refs[13:]
    d = D_MODEL
    mm = lambda a, w: jnp.dot(a[...], w[...], preferred_element_type=F32)
    merged = (_sigmoid(gt_ref[:, 0:d].astype(F32)) * mm(oa_ref, wa_ref)
              + _sigmoid(gt_ref[:, d:2 * d].astype(F32)) * mm(ol_ref, wl_ref)
              + _sigmoid(gt_ref[:, 2 * d:3 * d].astype(F32)) * mm(os_ref, ws_ref))
    mix = jnp.dot(merged.astype(BF16), wo_ref[...], preferred_element_type=F32)
    x1 = x_ref[...] + g1_ref[...] * mix
    x1_ref[...] = x1
    h2 = _rms(x1, gn_ref[...]) * (1.0 + sc2_ref[...]) + sh2_ref[...]
    h2_ref[...] = h2.astype(h2_ref.dtype)
    if with_router:
        hi = h2.astype(BF16)
        lo = (h2 - hi.astype(F32)).astype(BF16)
        wr = wr_ref[...]
        t = jnp.dot(hi, wr, preferred_element_type=F32) + jnp.dot(lo, wr, preferred_element_type=F32)
        lg_ref[...] = t + pltpu.roll(t, LANES - N_EXPERTS, 1) + br_ref[...]


def _router_stack(moe_w_router, moe_b_router):
    w = moe_w_router.astype(F32)
    hi = w.astype(BF16)
    lo = (w - hi.astype(F32)).astype(BF16)
    n = w.shape[0]
    wr = jnp.concatenate([hi, lo, jnp.zeros((n, D_MODEL, LANES - 2 * N_EXPERTS), BF16)], axis=2)
    br = jnp.pad(moe_b_router.astype(F32), ((0, 0), (0, LANES - N_EXPERTS)))[:, None, :]
    return wr, br


def _merge(oa, ol, os_, gt, x, layer, mod, gn_all, wa_all, wl_all, ws_all, wo_all, router, tm):
    m = x.shape[0]
    row = lambda w: pl.BlockSpec((tm, w), lambda i: (i, 0))
    ls = lambda shape: _layer_spec(shape, layer, 1)
    in_specs = [row(ATTN_WIDTH), row(LRU_WIDTH), row(SSD_INNER), row(GATES_W), row(D_MODEL),
                mod.spec(layer, 2, tm), mod.spec(layer, 4, tm), mod.spec(layer, 3, tm),
                ls((1, D_MODEL)), ls((ATTN_WIDTH, D_MODEL)), ls((LRU_WIDTH, D_MODEL)),
                ls((SSD_INNER, D_MODEL)), ls((D_MODEL, D_MODEL))]
    args = [oa, ol, os_, gt, x, mod.arr, mod.arr, mod.arr, gn_all, wa_all, wl_all, ws_all, wo_all]
    out_specs = [row(D_MODEL), row(D_MODEL)]
    out_shape = [jax.ShapeDtypeStruct((m, D_MODEL), F32),
                 jax.ShapeDtypeStruct((m, D_MODEL), F32 if router is not None else BF16)]
    if router is not None:
        wr_all, br_all, j = router
        in_specs += [_layer_spec((D_MODEL, LANES), j, 1), _layer_spec((1, LANES), j, 1)]
        args += [wr_all, br_all]
        out_specs.append(row(LANES))
        out_shape.append(jax.ShapeDtypeStruct((m, LANES), F32))
    return pl.pallas_call(
        functools.partial(_merge_kernel, with_router=router is not None),
        grid=(m // tm,),
        in_specs=in_specs, out_specs=out_specs, out_shape=out_shape,
        compiler_params=_params(("arbitrary",)),
        name="merge",
    )(*args)


def _ffn_kernel(te_ref, nv_ref, *refs, fused_residual):
    if fused_residual:
        x_ref, wg_ref, wu_ref, wd_ref, res_ref, g2_ref, o_ref = refs
    else:
        x_ref, wg_ref, wu_ref, wd_ref, o_ref = refs
    i = pl.program_id(0)
    j = pl.program_id(1)

    @pl.when(j == 0)
    def _():
        o_ref[...] = jnp.zeros_like(o_ref)

    @pl.when(i < nv_ref[0])
    def _():
        h = x_ref[...].astype(BF16)
        a = jnp.dot(h, wg_ref[...], preferred_element_type=F32)
        b = jnp.dot(h, wu_ref[...], preferred_element_type=F32)
        t = (_silu(a) * b).astype(BF16)
        o_ref[...] += jnp.dot(t, wd_ref[...], preferred_element_type=F32)

    if fused_residual:
        @pl.when(j == pl.num_programs(1) - 1)
        def _():
            o_ref[...] = res_ref[...] + g2_ref[...] * o_ref[...]


def _ffn(x, wg, wu, wd, tile_expert, n_valid, tm, expert_base=0, residual=None, mod=None, layer=None):
    m = x.shape[0]
    nj = D_FF // FF_CHUNK
    last = nj - 1

    def jj(i, j, nv):
        return jnp.where(i < nv[0], j, last)

    in_specs = [pl.BlockSpec((tm, D_MODEL), lambda i, j, te, nv: (jnp.minimum(i, nv[0] - 1), 0)),
                pl.BlockSpec((None, D_MODEL, FF_CHUNK), lambda i, j, te, nv: (expert_base + te[i], 0, jj(i, j, nv))),
                pl.BlockSpec((None, D_MODEL, FF_CHUNK), lambda i, j, te, nv: (expert_base + te[i], 0, jj(i, j, nv))),
                pl.BlockSpec((None, FF_CHUNK, D_MODEL), lambda i, j, te, nv: (expert_base + te[i], jj(i, j, nv), 0))]
    args = [x, wg, wu, wd]
    fused = residual is not None
    if fused:
        in_specs += [pl.BlockSpec((tm, D_MODEL), lambda i, j, te, nv: (i, 0)), mod.spec(layer, 5, tm)]
        args += [residual, mod.arr]
    return pl.pallas_call(
        functools.partial(_ffn_kernel, fused_residual=fused),
        grid_spec=pltpu.PrefetchScalarGridSpec(
            num_scalar_prefetch=2, grid=(m // tm, nj), in_specs=in_specs,
            out_specs=pl.BlockSpec((tm, D_MODEL), lambda i, j, te, nv: (i, 0))),
        out_shape=jax.ShapeDtypeStruct((m, D_MODEL), F32),
        compiler_params=_params(("arbitrary", "arbitrary")),
        name="ffn",
    )(tile_expert, n_valid, *args)


def _route_kernel(lg_ref, p1_ref, p2_ref, w1_ref, w2_ref, te_ref, nv_ref, carry, e1s, e2s, r1s, r2s,
                  *, tb, tm, nblk):
    step = pl.program_id(0)

    @pl.when(step == 0)
    def _():
        carry[...] = jnp.zeros_like(carry)

    l8 = lg_ref[...].T[0:N_EXPERTS, :]
    e = lax.broadcasted_iota(I32, l8.shape, 0)
    m1 = jnp.max(l8, axis=0, keepdims=True)
    i1 = jnp.min(jnp.where(l8 == m1, e, N_EXPERTS), axis=0, keepdims=True)
    rest = jnp.where(e == i1, -jnp.inf, l8)
    m2 = jnp.max(rest, axis=0, keepdims=True)
    i2 = jnp.min(jnp.where(rest == m2, e, N_EXPERTS), axis=0, keepdims=True)
    t = jnp.exp(m2 - m1)
    w1 = 1.0 / (1.0 + t)
    sel1 = jnp.where(e == i1, 1.0, 0.0)
    sel2 = jnp.where(e == i2, 1.0, 0.0)
    sel = sel1 + sel2
    a = lax.broadcasted_iota(I32, (tb, tb), 0)
    b = lax.broadcasted_iota(I32, (tb, tb), 1)
    upper = jnp.where(a <= b, 1.0, 0.0).astype(BF16)
    incl = jnp.dot(sel.astype(BF16), upper, preferred_element_type=F32)
    rank = carry[:, 0:1] + incl - sel
    e1s[pl.ds(step, 1), :] = i1
    e2s[pl.ds(step, 1), :] = i2
    r1s[pl.ds(step, 1), :] = jnp.sum(sel1 * rank, axis=0, keepdims=True).astype(I32)
    r2s[pl.ds(step, 1), :] = jnp.sum(sel2 * rank, axis=0, keepdims=True).astype(I32)
    w1_ref[...] = w1
    w2_ref[...] = t * w1
    carry[...] = carry[...] + incl[:, tb - 1:tb]

    @pl.when(step == nblk - 1)
    def _():
        counts = carry[...].astype(I32)
        shift = tm.bit_length() - 1
        padded = lax.shift_right_logical(counts + (tm - 1), shift) << shift
        ex = lax.broadcasted_iota(I32, padded.shape, 0)
        ends = padded
        d = 1
        while d < N_EXPERTS:
            ends = ends + jnp.where(ex >= d, pltpu.roll(ends, d, 0), 0)
            d *= 2
        starts = ends - padded
        total = ends[N_EXPERTS - 1:N_EXPERTS, :]
        nvalid = lax.shift_right_logical(total, shift)
        tile0 = lax.broadcasted_iota(I32, padded.shape, 1) * tm
        te = jnp.sum(jnp.where(ends <= tile0, 1, 0), axis=0, keepdims=True)
        te_last = jnp.sum(jnp.where(ends <= total - tm, 1, 0), axis=0, keepdims=True)
        te_ref[...] = jnp.where(tile0[0:1, :] < total, te, te_last)
        nv_ref[...] = nvalid
        reps = tb // LANES
        e1, e2 = e1s[...], e2s[...]
        off1 = jnp.zeros(e1.shape, I32)
        off2 = jnp.zeros(e2.shape, I32)
        for k in range(N_EXPERTS):
            row = starts[k:k + 1, :]
            row = jnp.concatenate([row] * reps, axis=1) if reps > 1 else row
            off1 = jnp.where(e1 == k, row, off1)
            off2 = jnp.where(e2 == k, row, off2)
        p1_ref[...] = off1 + r1s[...]
        p2_ref[...] = off2 + r2s[...]


def _route(logits, tb, tm):
    m = logits.shape[0]
    nblk = m // tb
    assert tb % LANES == 0 and (m * TOP_K) // tm + N_EXPERTS <= LANES and tm & (tm - 1) == 0
    vec = pl.BlockSpec((1, tb), lambda i: (0, i))
    res = pl.BlockSpec((nblk, tb), lambda i: (0, 0))
    one = pl.BlockSpec((1, LANES), lambda i: (0, 0))
    return pl.pallas_call(
        functools.partial(_route_kernel, tb=tb, tm=tm, nblk=nblk),
        grid=(nblk,),
        in_specs=[pl.BlockSpec((tb, LANES), lambda i: (i, 0))],
        out_specs=[res, res, vec, vec, one, one],
        out_shape=[jax.ShapeDtypeStruct((nblk, tb), I32), jax.ShapeDtypeStruct((nblk, tb), I32),
                   jax.ShapeDtypeStruct((1, m), F32), jax.ShapeDtypeStruct((1, m), F32),
                   jax.ShapeDtypeStruct((1, LANES), I32), jax.ShapeDtypeStruct((1, LANES), I32)],
        scratch_shapes=[pltpu.VMEM((N_EXPERTS, LANES), F32)] + [pltpu.VMEM((nblk, tb), I32)] * 4,
        compiler_params=_params(("arbitrary",)),
        name="route",
    )(logits)


def _dispatch_kernel(p1_ref, p2_ref, x_ref, dst_in_ref, dst_ref, sem, *, tb):
    del dst_in_ref

    def start(r, c):
        src = x_ref.at[pl.ds(r, 1), :]
        pltpu.make_async_copy(src, dst_ref.at[pl.ds(p1_ref[0, r], 1), :], sem.at[0]).start()
        pltpu.make_async_copy(src, dst_ref.at[pl.ds(p2_ref[0, r], 1), :], sem.at[1]).start()
        return c

    lax.fori_loop(0, tb, start, 0, unroll=8)
    row0 = x_ref.at[pl.ds(0, 1), :]
    for _ in range(tb):
        pltpu.make_async_copy(row0, dst_ref.at[pl.ds(0, 1), :], sem.at[0]).wait()
        pltpu.make_async_copy(row0, dst_ref.at[pl.ds(0, 1), :], sem.at[1]).wait()


def _dispatch(x, pos1, pos2, dst, tb):
    m = x.shape[0]
    smem = pl.BlockSpec((None, 1, tb), lambda i: (i, 0, 0), memory_space=pltpu.SMEM)
    return pl.pallas_call(
        functools.partial(_dispatch_kernel, tb=tb),
        grid=(m // tb,),
        in_specs=[smem, smem, pl.BlockSpec((tb, D_MODEL), lambda i: (i, 0)),
                  pl.BlockSpec(memory_space=pl.ANY)],
        out_specs=pl.BlockSpec(memory_space=pl.ANY),
        out_shape=jax.ShapeDtypeStruct(dst.shape, F32),
        scratch_shapes=[pltpu.SemaphoreType.DMA((2,))],
        input_output_aliases={3: 0},
        compiler_params=_params(("arbitrary",)),
        name="dispatch",
    )(pos1.reshape(m // tb, 1, tb), pos2.reshape(m // tb, 1, tb), x, dst)


def _combine_kernel(p1_ref, p2_ref, y_ref, x1_ref, g2_ref, w1_ref, w2_ref, o_ref, buf1, buf2, sem, *, tb):
    def start(r, c):
        pltpu.make_async_copy(y_ref.at[pl.ds(p1_ref[0, r], 1), :], buf1.at[pl.ds(r, 1), :], sem.at[0]).start()
        pltpu.make_async_copy(y_ref.at[pl.ds(p2_ref[0, r], 1), :], buf2.at[pl.ds(r, 1), :], sem.at[1]).start()
        return c

    lax.fori_loop(0, tb, start, 0, unroll=8)
    for _ in range(tb):
        pltpu.make_async_copy(y_ref.at[pl.ds(0, 1), :], buf1.at[pl.ds(0, 1), :], sem.at[0]).wait()
        pltpu.make_async_copy(y_ref.at[pl.ds(0, 1), :], buf2.at[pl.ds(0, 1), :], sem.at[1]).wait()
    f = w1_ref[...] * buf1[...] + w2_ref[...] * buf2[...]
    o_ref[...] = x1_ref[...] + g2_ref[...] * f


def _combine(y, pos1, pos2, w1, w2, x1, layer, mod, tb):
    m = x1.shape[0]
    smem = pl.BlockSpec((None, 1, tb), lambda i: (i, 0, 0), memory_space=pltpu.SMEM)
    row = lambda w: pl.BlockSpec((tb, w), lambda i: (i, 0))
    return pl.pallas_call(
        functools.partial(_combine_kernel, tb=tb),
        grid=(m // tb,),
        in_specs=[smem, smem, pl.BlockSpec(memory_space=pl.ANY), row(D_MODEL),
                  mod.spec(layer, 5, tb), row(1), row(1)],
        out_specs=row(D_MODEL),
        out_shape=jax.ShapeDtypeStruct((m, D_MODEL), F32),
        scratch_shapes=[pltpu.VMEM((tb, D_MODEL), F32), pltpu.VMEM((tb, D_MODEL), F32),
                        pltpu.SemaphoreType.DMA((2,))],
        compiler_params=_params(("arbitrary",)),
        name="combine",
    )(pos1.reshape(m // tb, 1, tb), pos2.reshape(m // tb, 1, tb), y, x1, mod.arr,
      w1.reshape(m, 1), w2.reshape(m, 1))


def _moe(h2, logits, x1, layer, mod, wg, wu, wd, expert_base, sorted_buf, tm, tb):
    m = h2.shape[0]
    n_tiles = (m * TOP_K) // tm + N_EXPERTS
    pos1, pos2, w1, w2, te, nv = _route(logits, tb, tm)
    hs = _dispatch(h2, pos1, pos2, sorted_buf, tb)
    y = _ffn(hs, wg, wu, wd, te[0, :n_tiles], nv[0, :1], tm, expert_base=expert_base)
    return _combine(y, pos1, pos2, w1, w2, x1, layer, mod, tb), hs


def _attn_decode_kernel(q_ref, ck_ref, cv_ref, kn_ref, vn_ref, sink_ref, o_ref, kw_ref, vw_ref):
    q = q_ref[...].astype(BF16)
    ck, cv = ck_ref[...], cv_ref[...]
    kn, vn = kn_ref[...], vn_ref[...]
    s = jnp.einsum('bhd,bwd->bhw', q, ck.astype(BF16), preferred_element_type=F32)
    knb = kn.astype(BF16).astype(F32)
    s_new = jnp.sum(q.astype(F32) * knb[:, None, :], axis=-1, keepdims=True)
    sink = sink_ref[...]
    m = jnp.maximum(jnp.maximum(jnp.max(s, axis=-1, keepdims=True), s_new), sink)
    p = jnp.exp(s - m)
    pn = jnp.exp(s_new - m)
    inv = 1.0 / (jnp.sum(p, axis=-1, keepdims=True) + pn + jnp.exp(sink - m))
    o = jnp.einsum('bhw,bwd->bhd', (p * inv).astype(BF16), cv.astype(BF16), preferred_element_type=F32)
    o_ref[...] = o + (pn * inv).astype(BF16).astype(F32) * vn.astype(BF16).astype(F32)[:, None, :]
    w = lax.broadcasted_iota(I32, ck.shape, 1)
    last = w == ck.shape[1] - 1
    kw_ref[...] = jnp.where(last, kn[:, None, :], pltpu.roll(ck, ck.shape[1] - 1, 1))
    vw_ref[...] = jnp.where(last, vn[:, None, :], pltpu.roll(cv, cv.shape[1] - 1, 1))


def _attention_decode(qkv, layer, ck_all, cv_all, sinks_col, tb):
    n = qkv.shape[0]
    win = ck_all.shape[2]
    qh = qkv[:, :ATTN_WIDTH].reshape(n, N_KV_HEADS, GQA_GROUP, HEAD_DIM)
    z = jnp.zeros_like(qh[:, 0])
    q_pl = jnp.concatenate([jnp.concatenate([qh[:, 0], z], axis=-1),
                            jnp.concatenate([z, qh[:, 1]], axis=-1)], axis=1)
    kn = qkv[:, ATTN_WIDTH:ATTN_WIDTH + KV_WIDTH].astype(F32)
    vn = qkv[:, ATTN_WIDTH + KV_WIDTH:].astype(F32)
    blk3 = lambda a, b: pl.BlockSpec((tb, a, b), lambda i: (i, 0, 0))
    cache = pl.BlockSpec((None, tb, win, KV_WIDTH), lambda i: (layer, i, 0, 0))
    blk2 = pl.BlockSpec((tb, KV_WIDTH), lambda i: (i, 0))
    o, ck_all, cv_all = pl.pallas_call(
        _attn_decode_kernel,
        grid=(n // tb,),
        in_specs=[blk3(N_Q_HEADS, LANES), cache, cache, blk2, blk2,
                  pl.BlockSpec((None, N_Q_HEADS, 1), lambda i: (layer, 0, 0))],
        out_specs=[blk3(N_Q_HEADS, LANES), cache, cache],
        out_shape=[jax.ShapeDtypeStruct((n, N_Q_HEADS, LANES), F32),
                   jax.ShapeDtypeStruct(ck_all.shape, F32),
                   jax.ShapeDtypeStruct(cv_all.shape, F32)],
        input_output_aliases={1: 1, 2: 2},
        compiler_params=_params(("arbitrary",)),
        name="attn_decode",
    )(q_pl, ck_all, cv_all, kn, vn, sinks_col)
    o = o.reshape(n, N_KV_HEADS, GQA_GROUP, N_KV_HEADS, HEAD_DIM)
    o = jnp.concatenate([o[:, 0, :, 0], o[:, 1, :, 1]], axis=1).reshape(n, ATTN_WIDTH)
    return o.astype(BF16), ck_all, cv_all


def _lru_decode_kernel(x_ref, gate_ref, buf_ref, h0_ref, cw_ref, cb_ref, wa_ref, ba_ref, wx_ref, bx_ref,
                       negsp_ref, o_ref, h_ref, nbuf_ref):
    x = x_ref[...]
    xc = x * cw_ref[CONV_K - 1:CONV_K, :] + cb_ref[...]
    for j in range(CONV_K - 1):
        xc = xc + buf_ref[:, j, :] * cw_ref[j:j + 1, :]
    a, u = _lru_gates(xc, wa_ref, ba_ref, wx_ref, bx_ref, negsp_ref)
    h = a * h0_ref[...] + u
    o_ref[...] = (h * _gelu_tanh(gate_ref[...].astype(F32))).astype(BF16)
    h_ref[...] = h
    for j in range(CONV_K - 2):
        nbuf_ref[:, j, :] = buf_ref[:, j + 1, :]
    nbuf_ref[:, CONV_K - 2, :] = x


def _lru_decode(lx, lgate, layer, buf_all, h0_all, lru_params):
    n = lx.shape[0]
    one = lambda shape: pl.BlockSpec(shape, lambda i: (0,) * len(shape))
    lay = lambda shape: pl.BlockSpec((None,) + shape, lambda i: (layer,) + (0,) * len(shape))
    nb = LRU_WIDTH // LANES
    return pl.pallas_call(
        _lru_decode_kernel,
        in_specs=[one((n, LRU_WIDTH)), one((n, LRU_WIDTH)), lay((n, CONV_K - 1, LRU_WIDTH)), lay((n, LRU_WIDTH)),
                  lay((CONV_K, LRU_WIDTH)), lay((1, LRU_WIDTH)), lay((nb, LANES, LANES)), lay((1, LRU_WIDTH)),
                  lay((nb, LANES, LANES)), lay((1, LRU_WIDTH)), lay((1, LRU_WIDTH))],
        out_shape=[jax.ShapeDtypeStruct((n, LRU_WIDTH), BF16),
                   jax.ShapeDtypeStruct((n, LRU_WIDTH), F32),
                   jax.ShapeDtypeStruct((n, CONV_K - 1, LRU_WIDTH), F32)],
        grid=(1,),
        compiler_params=_params(("arbitrary",)),
        name="lru_decode",
    )(lx, lgate, buf_all, h0_all, *lru_params)


def _ssd_pre_decode_kernel(xbc_ref, dt_ref, buf_ref, cw_ref, cb_ref, dtb_ref, aneg_ref, rexp_ref,
                           xs_ref, xdt_ref, dec_ref, bm_ref, cm_ref, nbuf_ref):
    xbc = xbc_ref[...]
    xc = xbc * cw_ref[CONV_K - 1:CONV_K, :] + cb_ref[...]
    for j in range(CONV_K - 1):
        xc = xc + buf_ref[:, j, :] * cw_ref[j:j + 1, :]
    xc = _silu(xc)
    xs = xc[:, :SSD_INNER]
    rexp = rexp_ref[...]
    dt = _softplus(dt_ref[...] + dtb_ref[...])
    xs_ref[...] = xs
    xdt_ref[...] = xs * _dot_exact_rhs(dt, rexp)
    dec_ref[...] = jnp.exp(_dot_exact_rhs(dt * aneg_ref[...], rexp))
    bm_ref[...] = xc[:, SSD_INNER:SSD_INNER + LANES]
    cm_ref[...] = xc[:, SSD_INNER + LANES:]
    for j in range(CONV_K - 2):
        nbuf_ref[:, j, :] = buf_ref[:, j + 1, :]
    nbuf_ref[:, CONV_K - 2, :] = xbc


def _ssd_state_decode_kernel(h0_ref, x_ref, dec_ref, bm_ref, cm_ref, rep_ref, tile_ref, sum_ref,
                             hn_ref, y_ref, *, tb):
    pn = SSD_HEAD_DIM * SSD_STATE
    x = x_ref[...].reshape(tb * SSD_HEADS, SSD_HEAD_DIM).astype(BF16)
    xrep = jnp.dot(x, rep_ref[...], preferred_element_type=F32).reshape(tb, SSD_HEADS, pn)
    bmb, cmb = bm_ref[...].astype(BF16), cm_ref[...].astype(BF16)
    first = lax.broadcasted_iota(I32, (tb, SSD_HEADS, pn), 1) < SSD_HEADS // SSD_GROUPS

    def tiled(v):
        t0 = jnp.dot(v, tile_ref[0], preferred_element_type=F32)
        t1 = jnp.dot(v, tile_ref[1], preferred_element_type=F32)
        return jnp.where(first, t0[:, None, :], t1[:, None, :])

    h_new = dec_ref[...][:, :, 0:1] * h0_ref[...] + xrep * tiled(bmb)
    hn_ref[...] = h_new
    hc = (h_new * tiled(cmb)).reshape(tb * SSD_HEADS, pn).astype(BF16)
    y_ref[...] = jnp.dot(hc, sum_ref[...], preferred_element_type=F32).reshape(tb, SSD_HEADS, SSD_HEAD_DIM)


def _ssd_post_decode_kernel(y_ref, xs_ref, z_ref, d_ref, gn_ref, o_ref):
    y = y_ref[...] + d_ref[...] * xs_ref[...]
    o_ref[...] = _ssd_gate_norm(y, z_ref[...].astype(F32), gn_ref[...]).astype(BF16)


def _ssd_decode_consts():
    pn = SSD_HEAD_DIM * SSD_STATE
    col_p = np.arange(pn) // SSD_STATE
    col_n = np.arange(pn) % SSD_STATE
    rep = (np.arange(SSD_HEAD_DIM)[:, None] == col_p[None, :]).astype(np.float32)
    row_g = np.arange(LANES) // SSD_STATE
    row_n = np.arange(LANES) % SSD_STATE
    tile = np.stack([((row_g[:, None] == g) & (row_n[:, None] == col_n[None, :])).astype(np.float32)
                     for g in range(SSD_GROUPS)])
    summ = rep.T
    return jnp.asarray(rep, BF16), jnp.asarray(tile, BF16), jnp.asarray(summ, BF16)


def _ssd_decode(xbc, dtr, z, layer, buf_all, h0_all, ssd_params, rexp, dec_consts, tb):
    n = xbc.shape[0]
    pn = SSD_HEAD_DIM * SSD_STATE
    cw_all, cb_all, dtb_all, aneg_all, d_all, gn_all = ssd_params
    f = lambda w: jax.ShapeDtypeStruct((n, w), F32)
    one = lambda shape: pl.BlockSpec(shape, lambda i: (0,) * len(shape))
    lay = lambda shape: pl.BlockSpec((None,) + shape, lambda i: (layer,) + (0,) * len(shape))
    xs, xdt, dec, bm, cm, nbuf = pl.pallas_call(
        _ssd_pre_decode_kernel,
        in_specs=[one((n, SSD_CONV_DIM)), one((n, LANES)), lay((n, CONV_K - 1, SSD_CONV_DIM)),
                  lay((CONV_K, SSD_CONV_DIM)), lay((1, SSD_CONV_DIM)), lay((1, LANES)), lay((1, LANES)),
                  one((LANES, SSD_INNER))],
        out_shape=[f(SSD_INNER), f(SSD_INNER), f(SSD_INNER), f(LANES), f(LANES),
                   jax.ShapeDtypeStruct((n, CONV_K - 1, SSD_CONV_DIM), F32)],
        grid=(1,),
        compiler_params=_params(("arbitrary",)),
        name="ssd_pre_decode",
    )(xbc, dtr, buf_all, cw_all, cb_all, dtb_all, aneg_all, rexp)
    rep, tile, summ = dec_consts
    b3 = lambda a, b: pl.BlockSpec((tb, a, b), lambda i: (i, 0, 0))
    b2 = pl.BlockSpec((tb, LANES), lambda i: (i, 0))
    full = lambda shape: pl.BlockSpec(shape, lambda i: (0,) * len(shape))
    state = pl.BlockSpec((None, tb, SSD_HEADS, pn), lambda i: (layer, i, 0, 0))
    h0_all, y = pl.pallas_call(
        functools.partial(_ssd_state_decode_kernel, tb=tb),
        grid=(n // tb,),
        in_specs=[state, b3(SSD_HEADS, SSD_HEAD_DIM), b3(SSD_HEADS, SSD_HEAD_DIM), b2, b2,
                  full((SSD_HEAD_DIM, pn)), full((SSD_GROUPS, LANES, pn)), full((pn, SSD_HEAD_DIM))],
        out_specs=[state, b3(SSD_HEADS, SSD_HEAD_DIM)],
        out_shape=[jax.ShapeDtypeStruct(h0_all.shape, F32),
                   jax.ShapeDtypeStruct((n, SSD_HEADS, SSD_HEAD_DIM), F32)],
        input_output_aliases={0: 0},
        compiler_params=_params(("arbitrary",)),
        name="ssd_state_decode",
    )(h0_all, xdt.reshape(n, SSD_HEADS, SSD_HEAD_DIM), dec.reshape(n, SSD_HEADS, SSD_HEAD_DIM), bm, cm,
      rep, tile, summ)
    o = pl.pallas_call(
        _ssd_post_decode_kernel,
        in_specs=[one((n, SSD_INNER)), one((n, SSD_INNER)), one((n, SSD_INNER)),
                  lay((1, SSD_INNER)), lay((1, SSD_INNER))],
        out_shape=jax.ShapeDtypeStruct((n, SSD_INNER), BF16),
        grid=(1,),
        compiler_params=_params(("arbitrary",)),
        name="ssd_post_decode",
    )(y.reshape(n, SSD_INNER), xs, z, d_all, gn_all)
    return o, h0_all, nbuf


def _pack_w_in(w):
    dt0 = C_GT
    dt = jnp.pad(w[..., dt0:dt0 + SSD_HEADS].astype(BF16), ((0, 0), (0, 0), (0, LANES - SSD_HEADS)))
    return jnp.concatenate([w[..., :dt0].astype(BF16), w[..., dt0 + SSD_HEADS:].astype(BF16), dt], axis=-1)


def kernel(x_prompt, x_sample, c_prompt, c_sample, cache_k, cache_v, state_lru_h, state_lru_conv, state_ssd_h, state_ssd_conv, w_mod, b_mod, g_norm_mix, g_norm_ffn, w_in, g_q, g_k, attn_sinks, lru_conv_w, lru_conv_b, lru_w_a, lru_b_a, lru_w_x, lru_b_x, lru_lambda, ssd_conv_w, ssd_conv_b, ssd_dt_bias, ssd_a_log, ssd_d, g_ssd_norm, w_br_attn, w_br_lru, w_br_ssd, w_out, ffn_w_gate, ffn_w_up, ffn_w_down, moe_w_router, moe_b_router, moe_w_gate, moe_w_up, moe_w_down):
    bp, seq = x_prompt.shape[:2]
    ns = x_sample.shape[0]
    win = cache_k.shape[2]
    assert x_sample.shape[1] == 1
    assert win == WINDOW and PAST_LEN >= WINDOW
    mp = bp * seq
    tm_p, tm_s = 512, ns
    lru_rows = 256
    pn = SSD_HEAD_DIM * SSD_STATE

    mod = _modulation(jnp.concatenate([c_sample, c_prompt], axis=0), w_mod, b_mod)
    mod_s = _Mod(mod, True, ns, 1)
    mod_p = _Mod(mod[:, ns:].reshape(DEPTH * bp * 6, 1, D_MODEL), False, bp, seq)
    row3 = lambda v: v.astype(F32)[:, None, :]
    w_packed = _pack_w_in(w_in)
    g_mix, g_ffn = row3(g_norm_mix), row3(g_norm_ffn)
    gq2, gk2 = row3(jnp.tile(g_q, (1, 2))), row3(jnp.tile(g_k, (1, 2)))
    sinks = attn_sinks.astype(F32)
    sinks_col = sinks[:, :, None]
    e2 = _head_avg_matrix()
    tab_p = _rope_tables(jnp.arange(seq))
    tab_s = _rope_tables(jnp.full((1,), PAST_LEN))
    wa, wx, negsp = _lru_weight_stack(lru_w_a, lru_w_x, lru_lambda)
    lru_params = (lru_conv_w.astype(F32), row3(lru_conv_b), wa, row3(lru_b_a), wx, row3(lru_b_x), negsp)
    ssd_params = _ssd_param_stack(ssd_conv_w, ssd_conv_b, ssd_dt_bias, ssd_a_log, ssd_d, g_ssd_norm)
    rexp = _head_expand_matrix()
    dec_consts = _ssd_decode_consts()
    br = (w_br_attn.astype(BF16), w_br_lru.astype(BF16), w_br_ssd.astype(BF16), w_out.astype(BF16))
    wr_all, br_all = _router_stack(moe_w_router, moe_b_router)
    n_moe = moe_w_gate.shape[0]
    flat = lambda w: w.astype(BF16).reshape((n_moe * N_EXPERTS,) + w.shape[2:])
    moe_w = (flat(moe_w_gate), flat(moe_w_up), flat(moe_w_down))
    ffn_w = (ffn_w_gate.astype(BF16), ffn_w_up.astype(BF16), ffn_w_down.astype(BF16))
    ck_all = cache_k.reshape(DEPTH, ns, win, KV_WIDTH)
    cv_all = cache_v.reshape(DEPTH, ns, win, KV_WIDTH)
    ssd_h0_all = state_ssd_h.reshape(DEPTH, ns, SSD_HEADS, pn)
    sorted_p = jnp.zeros((((mp * TOP_K) // tm_p + N_EXPERTS) * tm_p, D_MODEL), F32)
    sorted_s = jnp.zeros((((ns * TOP_K) // tm_s + N_EXPERTS) * tm_s, D_MODEL), F32)

    xp = x_prompt.reshape(mp, D_MODEL)
    xs = x_sample.reshape(ns, D_MODEL)
    outs_p = [[] for _ in range(6)]
    outs_s = [[] for _ in range(3)]
    attn_bias = _attn_bias()

    for l in range(DEPTH):
        is_moe = l % 2 == 1
        j = l // 2
        router = (wr_all, br_all, j) if is_moe else None

        qkv, gt, o_lru, lru_h, lru_buf, o_ssd, ssd_st, ssd_buf = _in_proj_mix(
            xp, l, g_mix, mod_p, w_packed, lru_params, ssd_params, rexp, tm_p, bp, seq)
        o_attn, k_win, v_win = _attention_prompt(qkv, l, sinks, attn_bias, tab_p, gq2, gk2, e2, bp, seq)
        merged = _merge(o_attn, o_lru, o_ssd, gt, xp, l, mod_p, g_ffn, *br, router, tm_p)
        if is_moe:
            x1, h2, logits = merged
            xp, sorted_p = _moe(h2, logits, x1, l, mod_p, *moe_w, j * N_EXPERTS, sorted_p, tm_p, 512)
        else:
            x1, h2 = merged
            nt = mp // tm_p
            xp = _ffn(h2, *ffn_w, jnp.zeros((nt,), I32), jnp.full((1,), nt, I32), tm_p,
                      expert_base=j, residual=x1, mod=mod_p, layer=l)
        for acc, t in zip(outs_p, (k_win, v_win, lru_h, lru_buf, ssd_st, ssd_buf)):
            acc.append(t)

        qkv, lx, lg, z, xbc, gt, dtr = _in_proj(xs, l, g_mix, mod_s, w_packed, tm_s, (tab_s, gq2, gk2, e2))
        o_attn, ck_all, cv_all = _attention_decode(qkv, l, ck_all, cv_all, sinks_col, 32)
        o_lru, lru_h, lru_buf = _lru_decode(lx, lg, l, state_lru_conv, state_lru_h, lru_params)
        o_ssd, ssd_h0_all, ssd_buf = _ssd_decode(xbc, dtr, z, l, state_ssd_conv, ssd_h0_all, ssd_params, rexp,
                                                 dec_consts, 16)
        merged = _merge(o_attn, o_lru, o_ssd, gt, xs, l, mod_s, g_ffn, *br, router, tm_s)
        if is_moe:
            x1, h2, logits = merged
            xs, sorted_s = _moe(h2, logits, x1, l, mod_s, *moe_w, j * N_EXPERTS, sorted_s, tm_s, ns)
        else:
            x1, h2 = merged
            xs = _ffn(h2, *ffn_w, jnp.zeros((1,), I32), jnp.ones((1,), I32), tm_s,
                      expert_base=j, residual=x1, mod=mod_s, layer=l)
        for acc, t in zip(outs_s, (lru_h, lru_buf, ssd_buf)):
            acc.append(t)

    st = [jnp.stack(a) for a in outs_p]
    ss = [jnp.stack(a) for a in outs_s]
    kv5 = lambda t, b: t.reshape(DEPTH, b, WINDOW, N_KV_HEADS, HEAD_DIM)
    return (xp.reshape(bp, seq, D_MODEL), xs.reshape(ns, 1, D_MODEL),
            kv5(st[0], bp), kv5(st[1], bp), st[2].reshape(DEPTH, bp, LRU_WIDTH), st[3],
            _ssd_state_to_hpn(st[4]), st[5],
            kv5(ck_all, ns), kv5(cv_all, ns), ss[0], ss[1],
            ssd_h0_all.reshape(DEPTH, ns, SSD_HEADS, SSD_HEAD_DIM, SSD_STATE), ss[2])
```

```python
import functools
import math

import numpy as np
import jax
import jax.numpy as jnp
from jax import lax
from jax.experimental import pallas as pl
from jax.experimental.pallas import tpu as pltpu

F32 = jnp.float32
BF16 = jnp.bfloat16
I32 = jnp.int32

D_MODEL = 1024
DEPTH = 4
PAST_LEN = 8192
N_Q_HEADS = 8
N_KV_HEADS = 2
HEAD_DIM = 64
GQA_GROUP = N_Q_HEADS // N_KV_HEADS
WINDOW = 128
ATTN_BLOCK = 128
ROT_DIM = HEAD_DIM // 4
ROPE_THETA = 500000.0
ATTN_WIDTH = N_Q_HEADS * HEAD_DIM
KV_WIDTH = N_KV_HEADS * HEAD_DIM
LRU_WIDTH = 512
LRU_BLOCKS = 8
LRU_BLOCK_DIM = LRU_WIDTH // LRU_BLOCKS
LRU_C = 8.0
CONV_K = 4
SSD_HEADS = 8
SSD_HEAD_DIM = 64
SSD_INNER = SSD_HEADS * SSD_HEAD_DIM
SSD_GROUPS = 2
SSD_STATE = 64
SSD_CHUNK = 128
SSD_CONV_DIM = SSD_INNER + 2 * SSD_GROUPS * SSD_STATE
D_FF = 2816
N_EXPERTS = 8
TOP_K = 2
EPS = 1e-6

LANES = 128
SUBLANES = 8
QKV_W = ATTN_WIDTH + 2 * KV_WIDTH
GATES_W = 3 * D_MODEL
C_QKV = 0
C_LX = C_QKV + QKV_W
C_LG = C_LX + LRU_WIDTH
C_Z = C_LG + LRU_WIDTH
C_XBC = C_Z + SSD_INNER
C_GT = C_XBC + SSD_CONV_DIM
C_DT = C_GT + GATES_W
IN_PACKED = C_DT + LANES
FF_CHUNK = D_FF // 2
VMEM_LIMIT = 56 * 1024 * 1024


def _params(sem=None):
    if sem is None:
        return pltpu.CompilerParams(vmem_limit_bytes=VMEM_LIMIT)
    return pltpu.CompilerParams(dimension_semantics=sem, vmem_limit_bytes=VMEM_LIMIT)


def _layer_spec(shape, layer, ngrid):
    zeros = (0,) * len(shape)
    return pl.BlockSpec((None,) + tuple(shape), lambda *g: (layer,) + zeros)


def _sigmoid(x):
    return 0.5 * jnp.tanh(0.5 * x) + 0.5


def _silu(x):
    return x * _sigmoid(x)


def _softplus(x):
    return jnp.maximum(x, 0.0) + jnp.log(1.0 + jnp.exp(-jnp.abs(x)))


def _gelu_tanh(x):
    return 0.5 * x * (1.0 + jnp.tanh(math.sqrt(2.0 / math.pi) * (x + 0.044715 * (x * x * x))))


def _split3(a):
    a1 = a.astype(BF16)
    r1 = a - a1.astype(F32)
    a2 = r1.astype(BF16)
    a3 = (r1 - a2.astype(F32)).astype(BF16)
    return a1, a2, a3


def _dot_exact_rhs(a, m):
    a1, a2, a3 = _split3(a)
    d = lambda t: jnp.dot(t, m, preferred_element_type=F32)
    return d(a1) + d(a2) + d(a3)


def _dot_exact_lhs(m, a):
    a1, a2, a3 = _split3(a)
    d = lambda t: jnp.dot(m, t, preferred_element_type=F32)
    return d(a1) + d(a2) + d(a3)


def _rms(x, g):
    return x * lax.rsqrt(jnp.mean(x * x, axis=-1, keepdims=True) + EPS) * g


def _mod_kernel(c_ref, w_ref, b_ref, o_ref):
    s = _silu(c_ref[...]).astype(BF16)
    o_ref[...] = jnp.dot(s, w_ref[...].astype(BF16), preferred_element_type=F32) + b_ref[...]


def _modulation(c, w_mod, b_mod):
    rows = c.shape[0]
    tn = 1536
    nt = (6 * D_MODEL) // tn
    return pl.pallas_call(
        _mod_kernel,
        grid=(DEPTH, nt),
        in_specs=[
            pl.BlockSpec((rows, D_MODEL), lambda l, j: (0, 0)),
            pl.BlockSpec((None, D_MODEL, tn), lambda l, j: (l, 0, j)),
            pl.BlockSpec((None, 1, tn), lambda l, j: (l, 0, j)),
        ],
        out_specs=pl.BlockSpec((None, rows, tn), lambda l, j: (l, 0, j)),
        out_shape=jax.ShapeDtypeStruct((DEPTH, rows, 6 * D_MODEL), F32),
        compiler_params=_params(("arbitrary", "arbitrary")),
        name="modulation",
    )(c, w_mod, b_mod.reshape(DEPTH, 1, 6 * D_MODEL))


class _Mod:
    def __init__(self, arr, per_row, batch, seq_len):
        self.arr, self.per_row, self.batch, self.seq_len = arr, per_row, batch, seq_len

    def spec(self, layer, k, tm, nprefetch=0):
        if self.per_row:
            return pl.BlockSpec((None, tm, D_MODEL), lambda i, *_: (layer, i, k))
        b, s = self.batch, self.seq_len
        return pl.BlockSpec((None, 1, D_MODEL), lambda i, *_: ((layer * b + (i * tm) // s) * 6 + k, 0, 0))


def _in_kernel(*refs, qk_norm):
    x_ref, g_ref, sc_ref, sh_ref, w_ref = refs[:5]
    qkv_ref, lx_ref, lg_ref, z_ref, xbc_ref, gt_ref, dt_ref = refs[-7:]
    h = (_rms(x_ref[...], g_ref[...]) * (1.0 + sc_ref[...]) + sh_ref[...]).astype(BF16)

    def mm(lo, hi):
        return jnp.dot(h, w_ref[:, lo:hi], preferred_element_type=F32)

    if qk_norm:
        cos_ref, slo_ref, shi_ref, gq_ref, gk_ref, e2_ref = refs[5:11]
        cos_t, slo, shi, e2 = cos_ref[...], slo_ref[...], shi_ref[...], e2_ref[...]
        for j in range(ATTN_WIDTH // LANES):
            q_slab = _head_norm_rope(mm(j * LANES, (j + 1) * LANES), gq_ref[...], e2, cos_t, slo, shi)
            qkv_ref[:, j * LANES:(j + 1) * LANES] = (q_slab * (HEAD_DIM ** -0.5)).astype(BF16)
        k_slab = _head_norm_rope(mm(ATTN_WIDTH, ATTN_WIDTH + KV_WIDTH), gk_ref[...], e2, cos_t, slo, shi)
        qkv_ref[:, ATTN_WIDTH:ATTN_WIDTH + KV_WIDTH] = k_slab.astype(BF16)
        qkv_ref[:, ATTN_WIDTH + KV_WIDTH:] = mm(ATTN_WIDTH + KV_WIDTH, C_LX).astype(BF16)
    else:
        qkv_ref[...] = mm(C_QKV, C_LX).astype(BF16)
    lx_ref[...] = mm(C_LX, C_LG)
    lg_ref[...] = mm(C_LG, C_Z).astype(BF16)
    z_ref[...] = mm(C_Z, C_XBC).astype(BF16)
    xbc_ref[...] = mm(C_XBC, C_GT)
    gt_ref[...] = mm(C_GT, C_DT).astype(BF16)
    dt_ref[...] = mm(C_DT, IN_PACKED)


def _in_proj(x, layer, g_all, mod, w_all, tm, qk_norm=None):
    m = x.shape[0]
    row = lambda w: pl.BlockSpec((tm, w), lambda i: (i, 0))
    in_specs = [row(D_MODEL), _layer_spec((1, D_MODEL), layer, 1), mod.spec(layer, 1, tm), mod.spec(layer, 0, tm),
                _layer_spec((D_MODEL, IN_PACKED), layer, 1)]
    args = [x, g_all, mod.arr, mod.arr, w_all]
    if qk_norm is not None:
        tables, gq_all, gk_all, e2 = qk_norm
        tab = pl.BlockSpec((1, LANES), lambda i: (0, 0))
        in_specs += [tab, tab, tab, _layer_spec((1, LANES), layer, 1), _layer_spec((1, LANES), layer, 1),
                     pl.BlockSpec((LANES, LANES), lambda i: (0, 0))]
        args += [*tables, gq_all, gk_all, e2]
    return pl.pallas_call(
        functools.partial(_in_kernel, qk_norm=qk_norm is not None),
        grid=(m // tm,),
        in_specs=in_specs,
        out_specs=[row(QKV_W), row(LRU_WIDTH), row(LRU_WIDTH), row(SSD_INNER),
                   row(SSD_CONV_DIM), row(GATES_W), row(LANES)],
        out_shape=[jax.ShapeDtypeStruct((m, QKV_W), BF16),
                   jax.ShapeDtypeStruct((m, LRU_WIDTH), F32),
                   jax.ShapeDtypeStruct((m, LRU_WIDTH), BF16),
                   jax.ShapeDtypeStruct((m, SSD_INNER), BF16),
                   jax.ShapeDtypeStruct((m, SSD_CONV_DIM), F32),
                   jax.ShapeDtypeStruct((m, GATES_W), BF16),
                   jax.ShapeDtypeStruct((m, LANES), F32)],
        compiler_params=_params(("arbitrary",)),
        name="in_proj",
    )(*args)


def _head_norm_rope(t, g2, e2, cos_t, sin_lo, sin_hi):
    sq = t * t
    hi = sq.astype(BF16)
    lo = (sq - hi.astype(F32)).astype(BF16)
    ms = jnp.dot(hi, e2, preferred_element_type=F32) + jnp.dot(lo, e2, preferred_element_type=F32)
    tn = t * lax.rsqrt(ms + EPS) * g2
    up = pltpu.roll(tn, LANES - ROT_DIM // 2, 1)
    dn = pltpu.roll(tn, ROT_DIM // 2, 1)
    return tn * cos_t + up * sin_lo + dn * sin_hi


def _rope_tables(pos):
    half = ROT_DIM // 2
    inv_freq = ROPE_THETA ** (-jnp.arange(0, ROT_DIM, 2, dtype=F32) / ROT_DIM)
    ang = pos.astype(F32)[:, None] * inv_freq[None, :]
    cos, sin = jnp.cos(ang), jnp.sin(ang)
    n = pos.shape[0]
    ones = jnp.ones((n, HEAD_DIM - ROT_DIM), F32)
    zeros = jnp.zeros((n, HEAD_DIM - ROT_DIM), F32)
    zh = jnp.zeros((n, half), F32)
    cos_t = jnp.concatenate([cos, cos, ones], axis=1)
    sin_lo = jnp.concatenate([-sin, zh, zeros], axis=1)
    sin_hi = jnp.concatenate([zh, sin, zeros], axis=1)
    rep = lambda t: jnp.concatenate([t, t], axis=1)
    return rep(cos_t), rep(sin_lo), rep(sin_hi)


def _head_avg_matrix():
    r = np.arange(LANES)
    return jnp.asarray((r[:, None] // HEAD_DIM == r[None, :] // HEAD_DIM).astype(np.float32) / HEAD_DIM, dtype=BF16)


def _place_kv(t, fill_left, fill_right):
    lane = lax.broadcasted_iota(I32, t.shape, 1)
    left = lane < HEAD_DIM
    sw = pltpu.roll(t, HEAD_DIM, 1)
    return [jnp.where(left, t, fill_left).astype(BF16), jnp.where(left, fill_right, sw).astype(BF16),
            jnp.where(left, sw, fill_left).astype(BF16), jnp.where(left, fill_right, t).astype(BF16)]


def _attn_kernel(sink_ref, qkv_ref, bias0_ref, bias1_ref, cos_ref, slo_ref, shi_ref, gq_ref, gk_ref, e2_ref,
                 o_ref, kwin_ref, vwin_ref, kvar, vvar, *, layer):
    @pl.when(pl.program_id(1) == 0)
    def _():
        kvar[...] = jnp.zeros_like(kvar)
        vvar[...] = jnp.zeros_like(vvar)

    k_f, v_f = _attn_rows(sink_ref, qkv_ref, 0, bias0_ref[...], bias1_ref[...], cos_ref[...], slo_ref[...],
                          shi_ref[...], gq_ref[...], gk_ref[...], e2_ref[...], o_ref, kvar, vvar, layer)
    kwin_ref[...] = k_f[ATTN_BLOCK:]
    vwin_ref[...] = v_f[ATTN_BLOCK:]


def _attn_rows(sink_ref, qkv_ref, r0, bias0, bias1, cos_t, slo, shi, gq, gk, e2, o_ref, kvar, vvar, layer):
    blk = ATTN_BLOCK
    tile = slice(r0, r0 + 2 * blk)
    lane = lax.broadcasted_iota(I32, (blk, LANES), 1)
    zero = jnp.zeros((blk, LANES), F32)
    one_mid = jnp.where(lane == HEAD_DIM, 1.0, 0.0)
    one_first = jnp.where(lane == 0, 1.0, 0.0)
    k_f = _head_norm_rope(qkv_ref[tile, ATTN_WIDTH:ATTN_WIDTH + KV_WIDTH].astype(F32), gk, e2, cos_t, slo, shi)
    v_f = qkv_ref[tile, ATTN_WIDTH + KV_WIDTH:].astype(F32)
    q_n = []
    for j in range(ATTN_WIDTH // LANES):
        q_slab = _head_norm_rope(qkv_ref[tile, j * LANES:(j + 1) * LANES].astype(F32), gq, e2, cos_t, slo, shi)
        q_n.append((q_slab * (HEAD_DIM ** -0.5)).astype(BF16))
    k_blocks = [[kvar[i] for i in range(4)]]
    v_blocks = [[vvar[i] for i in range(4)]]
    for qb in range(2):
        k_blocks.append(_place_kv(k_f[qb * blk:(qb + 1) * blk], zero, zero))
        v_blocks.append(_place_kv(v_f[qb * blk:(qb + 1) * blk], one_mid, one_first))

    row2 = lax.broadcasted_iota(I32, (2 * blk, 1), 0)
    left2 = lax.broadcasted_iota(I32, (2 * blk, LANES), 1) < HEAD_DIM
    for qb in range(2):
        bias = bias0 if qb == 0 else bias1
        rows = slice(qb * blk, (qb + 1) * blk)
        orow = slice(r0 + qb * blk, r0 + (qb + 1) * blk)
        for g in range(N_KV_HEADS):
            q2 = jnp.concatenate([q_n[2 * g][rows], q_n[2 * g + 1][rows]], axis=0)
            acc = None
            for side in range(2):
                var = 2 * g + side
                keys = jnp.concatenate([k_blocks[qb][var], k_blocks[qb + 1][var]], axis=0)
                vals = jnp.concatenate([v_blocks[qb][var], v_blocks[qb + 1][var]], axis=0)
                s = lax.dot_general(q2, keys, (((1,), (1,)), ((), ())), preferred_element_type=F32) + bias
                h_top = GQA_GROUP * g + side
                sink = jnp.where(row2 < blk, sink_ref[layer, h_top], sink_ref[layer, h_top + 2])
                m = jnp.maximum(jnp.max(s, axis=-1, keepdims=True), sink)
                p = jnp.exp(s - m).astype(BF16)
                o = jnp.dot(p, vals, preferred_element_type=F32)
                ones_lane = HEAD_DIM if side == 0 else 0
                den = o[:, ones_lane:ones_lane + 1] + jnp.exp(sink - m)
                o = o * (1.0 / den)
                o = jnp.where(left2, o, 0.0) if side == 0 else jnp.where(left2, 0.0, o)
                acc = o if acc is None else acc + o
            o_ref[orow, (2 * g) * LANES:(2 * g + 1) * LANES] = acc[:blk].astype(BF16)
            o_ref[orow, (2 * g + 1) * LANES:(2 * g + 2) * LANES] = acc[blk:].astype(BF16)

    for i in range(4):
        kvar[i] = k_blocks[2][i]
        vvar[i] = v_blocks[2][i]
    return k_f, v_f


def _attn_bias():
    r = np.arange(2 * ATTN_BLOCK)[:, None] % ATTN_BLOCK
    c = np.arange(2 * ATTN_BLOCK)[None, :]
    band = (c >= r) & (c <= r + WINDOW)
    first = band & (c >= ATTN_BLOCK)
    to_bias = lambda ok: np.where(ok, 0.0, -np.inf).astype(np.float32)
    return jnp.asarray(np.stack([to_bias(first), to_bias(band)]))


def _attention_prompt(qkv, layer, sinks_all, bias, tables, gq_all, gk_all, e2, batch, seq_len):
    tq = 2 * ATTN_BLOCK
    nb = seq_len // tq
    win = pl.BlockSpec((None, ATTN_BLOCK, KV_WIDTH), lambda b, n: (b, 0, 0))
    tab = pl.BlockSpec((tq, LANES), lambda b, n: (n, 0))
    return pl.pallas_call(
        functools.partial(_attn_kernel, layer=layer),
        grid=(batch, nb),
        in_specs=[pl.BlockSpec(memory_space=pltpu.SMEM),
                  pl.BlockSpec((tq, QKV_W), lambda b, n: (b * nb + n, 0)),
                  pl.BlockSpec((None, tq, tq), lambda b, n: (jnp.minimum(n, 1), 0, 0)),
                  pl.BlockSpec((None, tq, tq), lambda b, n: (1, 0, 0)),
                  tab, tab, tab, _layer_spec((1, LANES), layer, 2), _layer_spec((1, LANES), layer, 2),
                  pl.BlockSpec((LANES, LANES), lambda b, n: (0, 0))],
        out_specs=[pl.BlockSpec((tq, ATTN_WIDTH), lambda b, n: (b * nb + n, 0)), win, win],
        out_shape=[jax.ShapeDtypeStruct((batch * seq_len, ATTN_WIDTH), BF16),
                   jax.ShapeDtypeStruct((batch, ATTN_BLOCK, KV_WIDTH), F32),
                   jax.ShapeDtypeStruct((batch, ATTN_BLOCK, KV_WIDTH), F32)],
        scratch_shapes=[pltpu.VMEM((4, ATTN_BLOCK, LANES), BF16), pltpu.VMEM((4, ATTN_BLOCK, LANES), BF16)],
        compiler_params=_params(("arbitrary", "arbitrary")),
        name="attn_prompt",
    )(sinks_all, qkv, bias, bias, *tables, gq_all, gk_all, e2)


def _conv_chunk(tail, x, w_ref, b_ref, rows):
    width = x.shape[1]
    x3 = x.reshape(rows // SUBLANES, SUBLANES, width)
    full = jnp.concatenate([tail[...][None], x3], axis=0)
    t = lax.broadcasted_iota(I32, x3.shape, 1)
    y = x3 * w_ref[CONV_K - 1:CONV_K, :] + b_ref[...]
    for j in range(1, CONV_K):
        rot = pltpu.roll(full, j, 1)
        y = y + jnp.where(t >= j, rot[1:], rot[:-1]) * w_ref[CONV_K - 1 - j:CONV_K - j, :]
    return y.reshape(rows, width)


def _lru_gates(xc, wa_ref, ba_ref, wx_ref, bx_ref, negsp_ref):
    xb = xc.astype(BF16)
    ra, ri = [], []
    for s in range(LRU_WIDTH // LANES):
        sl = slice(s * LANES, (s + 1) * LANES)
        ra.append(jnp.dot(xb[:, sl], wa_ref[s], preferred_element_type=F32))
        ri.append(jnp.dot(xb[:, sl], wx_ref[s], preferred_element_type=F32))
    r = _sigmoid(jnp.concatenate(ra, axis=1) + ba_ref[...])
    i = _sigmoid(jnp.concatenate(ri, axis=1) + bx_ref[...])
    log_a = r * negsp_ref[...]
    a = jnp.exp(log_a)
    u = jnp.sqrt(1.0 - a * a) * (i * xc)
    return a, u


def _lru_scan(x, gate, cw_ref, cb_ref, wa_ref, ba_ref, wx_ref, bx_ref, negsp_ref, xbuf, hcar, rows):
    xc = _conv_chunk(xbuf, x, cw_ref, cb_ref, rows)
    a, u = _lru_gates(xc, wa_ref, ba_ref, wx_ref, bx_ref, negsp_ref)

    ng = rows // SUBLANES
    a = a.reshape(ng, SUBLANES, LRU_WIDTH)
    u = u.reshape(ng, SUBLANES, LRU_WIDTH)
    t = lax.broadcasted_iota(I32, a.shape, 1)
    d = 1
    while d < SUBLANES:
        keep = t >= d
        a_s = jnp.where(keep, pltpu.roll(a, d, 1), 1.0)
        u_s = jnp.where(keep, pltpu.roll(u, d, 1), 0.0)
        u = a * u_s + u
        a = a * a_s
        d *= 2
    carry = hcar[...]
    groups = []
    for r in range(ng):
        hg = a[r] * carry + u[r]
        groups.append(hg)
        carry = hg[SUBLANES - 1:SUBLANES, :]
    h = jnp.concatenate(groups, axis=0)
    hcar[...] = carry
    xbuf[...] = x[rows - SUBLANES:rows, :]
    return h * _gelu_tanh(gate), carry


def _lru_kernel(x_ref, gate_ref, cw_ref, cb_ref, wa_ref, ba_ref, wx_ref, bx_ref, negsp_ref,
                o_ref, h_ref, buf_ref, xbuf, hcar, *, rows):
    @pl.when(pl.program_id(1) == 0)
    def _():
        xbuf[...] = jnp.zeros_like(xbuf)
        hcar[...] = jnp.zeros_like(hcar)

    x = x_ref[...]
    o, carry = _lru_scan(x, gate_ref[...].astype(F32), cw_ref, cb_ref, wa_ref, ba_ref, wx_ref, bx_ref, negsp_ref,
                         xbuf, hcar, rows)
    o_ref[...] = o.astype(BF16)
    h_ref[...] = carry
    buf_ref[...] = x[rows - (CONV_K - 1):rows, :]


def _lru_weight_stack(lru_w_a, lru_w_x, lru_lambda):
    def pair(w):
        w = w.astype(BF16).reshape(DEPTH, LRU_BLOCKS // 2, 2, LRU_BLOCK_DIM, LRU_BLOCK_DIM)
        z = jnp.zeros((DEPTH, LRU_BLOCKS // 2, LRU_BLOCK_DIM, LRU_BLOCK_DIM), BF16)
        top = jnp.concatenate([w[:, :, 0], z], axis=3)
        bot = jnp.concatenate([z, w[:, :, 1]], axis=3)
        return jnp.concatenate([top, bot], axis=2)
    return pair(lru_w_a), pair(lru_w_x), (-LRU_C * jax.nn.softplus(-lru_lambda.astype(F32)))[:, None, :]


def _lru_param_specs(layer, ngrid):
    ls = lambda shape: _layer_spec(shape, layer, ngrid)
    nb = LRU_WIDTH // LANES
    return [ls((CONV_K, LRU_WIDTH)), ls((1, LRU_WIDTH)), ls((nb, LANES, LANES)), ls((1, LRU_WIDTH)),
            ls((nb, LANES, LANES)), ls((1, LRU_WIDTH)), ls((1, LRU_WIDTH))]


def _lru_prompt(lx, lgate, layer, lru_params, batch, seq_len, rows):
    nc = seq_len // rows
    tok = lambda w: pl.BlockSpec((rows, w), lambda b, n: (b * nc + n, 0))
    return pl.pallas_call(
        functools.partial(_lru_kernel, rows=rows),
        grid=(batch, nc),
        in_specs=[tok(LRU_WIDTH), tok(LRU_WIDTH)] + _lru_param_specs(layer, 2),
        out_specs=[tok(LRU_WIDTH),
                   pl.BlockSpec((None, 1, LRU_WIDTH), lambda b, n: (b, 0, 0)),
                   pl.BlockSpec((None, CONV_K - 1, LRU_WIDTH), lambda b, n: (b, 0, 0))],
        out_shape=[jax.ShapeDtypeStruct((batch * seq_len, LRU_WIDTH), BF16),
                   jax.ShapeDtypeStruct((batch, 1, LRU_WIDTH), F32),
                   jax.ShapeDtypeStruct((batch, CONV_K - 1, LRU_WIDTH), F32)],
        scratch_shapes=[pltpu.VMEM((SUBLANES, LRU_WIDTH), F32), pltpu.VMEM((1, LRU_WIDTH), F32)],
        compiler_params=_params(("arbitrary", "arbitrary")),
        name="lru_prompt",
    )(lx, lgate, *lru_params)


def _in_lru_kernel(x_ref, g_ref, sc_ref, sh_ref, w_ref, cw_ref, cb_ref, wa_ref, ba_ref, wx_ref, bx_ref, negsp_ref,
                   qkv_ref, z_ref, xbc_ref, gt_ref, dt_ref, ol_ref, lh_ref, lbuf_ref, xbuf, hcar,
                   *, rows, tiles_per_seq):
    @pl.when(pl.program_id(0) % tiles_per_seq == 0)
    def _():
        xbuf[...] = jnp.zeros_like(xbuf)
        hcar[...] = jnp.zeros_like(hcar)

    h = (_rms(x_ref[...], g_ref[...]) * (1.0 + sc_ref[...]) + sh_ref[...]).astype(BF16)

    def mm(lo, hi):
        return jnp.dot(h, w_ref[:, lo:hi], preferred_element_type=F32)

    lx = mm(C_LX, C_LG)
    gate = mm(C_LG, C_Z)
    qkv_ref[...] = mm(C_QKV, C_LX).astype(BF16)
    z_ref[...] = mm(C_Z, C_XBC).astype(BF16)
    xbc_ref[...] = mm(C_XBC, C_GT)
    gt_ref[...] = mm(C_GT, C_DT).astype(BF16)
    dt_ref[...] = mm(C_DT, IN_PACKED)
    o, carry = _lru_scan(lx, gate, cw_ref, cb_ref, wa_ref, ba_ref, wx_ref, bx_ref, negsp_ref, xbuf, hcar, rows)
    ol_ref[...] = o.astype(BF16)
    lh_ref[...] = carry
    lbuf_ref[...] = lx[rows - (CONV_K - 1):rows, :]


def _in_proj_lru(x, layer, g_all, mod, w_all, lru_params, tm, batch, seq_len):
    m = x.shape[0]
    tps = seq_len // tm
    row = lambda w: pl.BlockSpec((tm, w), lambda i: (i, 0))
    per_seq = lambda r, w: pl.BlockSpec((None, r, w), lambda i: (i // tps, 0, 0))
    return pl.pallas_call(
        functools.partial(_in_lru_kernel, rows=tm, tiles_per_seq=tps),
        grid=(m // tm,),
        in_specs=[row(D_MODEL), _layer_spec((1, D_MODEL), layer, 1), mod.spec(layer, 1, tm), mod.spec(layer, 0, tm),
                  _layer_spec((D_MODEL, IN_PACKED), layer, 1)] + _lru_param_specs(layer, 1),
        out_specs=[row(QKV_W), row(SSD_INNER), row(SSD_CONV_DIM), row(GATES_W), row(LANES), row(LRU_WIDTH),
                   per_seq(1, LRU_WIDTH), per_seq(CONV_K - 1, LRU_WIDTH)],
        out_shape=[jax.ShapeDtypeStruct((m, QKV_W), BF16),
                   jax.ShapeDtypeStruct((m, SSD_INNER), BF16),
                   jax.ShapeDtypeStruct((m, SSD_CONV_DIM), F32),
                   jax.ShapeDtypeStruct((m, GATES_W), BF16),
                   jax.ShapeDtypeStruct((m, LANES), F32),
                   jax.ShapeDtypeStruct((m, LRU_WIDTH), BF16),
                   jax.ShapeDtypeStruct((batch, 1, LRU_WIDTH), F32),
                   jax.ShapeDtypeStruct((batch, CONV_K - 1, LRU_WIDTH), F32)],
        scratch_shapes=[pltpu.VMEM((SUBLANES, LRU_WIDTH), F32), pltpu.VMEM((1, LRU_WIDTH), F32)],
        compiler_params=_params(("arbitrary",)),
        name="in_proj_lru",
    )(x, g_all, mod.arr, mod.arr, w_all, *lru_params)


def _ssd_gate_norm(y, z, gn):
    u = y * _silu(z)
    gw = SSD_INNER // SSD_GROUPS
    parts = []
    for g in range(SSD_GROUPS):
        ug = u[:, g * gw:(g + 1) * gw]
        parts.append(ug * lax.rsqrt(jnp.mean(ug * ug, axis=-1, keepdims=True) + EPS))
    return jnp.concatenate(parts, axis=1) * gn


def _ssd_chunk(xbc, dt_raw, z, cw_ref, cb_ref, dtb_ref, aneg_ref, d_ref, gn_ref, rexp_ref, xbuf, state):
    q = SSD_CHUNK
    gw = SSD_INNER // SSD_GROUPS
    xc = _silu(_conv_chunk(xbuf, xbc, cw_ref, cb_ref, q))
    xs = xc[:, :SSD_INNER]
    bm = xc[:, SSD_INNER:SSD_INNER + LANES]
    cm = xc[:, SSD_INNER + LANES:]

    rexp = rexp_ref[...]
    dt = _softplus(dt_raw + dtb_ref[...])
    dta = dt * aneg_ref[...]
    li = lax.broadcasted_iota(I32, (q, q), 0)
    si = lax.broadcasted_iota(I32, (q, q), 1)
    causal = li >= si
    tri = jnp.where(causal, 1.0, 0.0).astype(BF16)
    a_cs = _dot_exact_lhs(tri, dta)
    a_cs_t = a_cs.T
    dt_e = _dot_exact_rhs(dt, rexp)
    a_e = _dot_exact_rhs(a_cs, rexp)
    a_last = a_e[q - 1:q, :]

    xdt = xs * dt_e
    lane = lax.broadcasted_iota(I32, (q, LANES), 1)
    left = lane < SSD_STATE
    zero = jnp.zeros((q, LANES), F32)
    cmb = cm.astype(BF16)
    bmb = bm.astype(BF16)
    cm_g = (jnp.where(left, cm, zero).astype(BF16), jnp.where(left, zero, cm).astype(BF16))
    scores = [lax.dot_general(cm_g[g], bmb, (((1,), (1,)), ((), ())), preferred_element_type=F32)
              for g in range(SSD_GROUPS)]

    y_off = jnp.dot(cmb, state[...].astype(BF16), preferred_element_type=F32) * jnp.exp(a_e)

    ys = []
    for j in range(SSD_INNER // LANES):
        xs_slab = xdt[:, j * LANES:(j + 1) * LANES]
        x_side = (jnp.where(left, xs_slab, zero).astype(BF16), jnp.where(left, zero, xs_slab).astype(BF16))
        acc = y_off[:, j * LANES:(j + 1) * LANES]
        for side in range(2):
            h = 2 * j + side
            g = h // (SSD_HEADS // SSD_GROUPS)
            col = a_e[:, h * SSD_HEAD_DIM:h * SSD_HEAD_DIM + 1]
            row = a_cs_t[h:h + 1, :]
            decay = jnp.exp(jnp.where(causal, col - row, -jnp.inf))
            acc = acc + jnp.dot((scores[g] * decay).astype(BF16), x_side[side], preferred_element_type=F32)
        ys.append(acc + d_ref[:, j * LANES:(j + 1) * LANES] * xs[:, j * LANES:(j + 1) * LANES])
    y = jnp.concatenate(ys, axis=1)
    out = _ssd_gate_norm(y, z, gn_ref[...])

    x_end = (xdt * jnp.exp(a_last - a_e)).astype(BF16)
    upd = jnp.dot(bm.T.astype(BF16), x_end, preferred_element_type=F32)
    rg = lax.broadcasted_iota(I32, (LANES, SSD_INNER), 0) // SSD_STATE
    cg = lax.broadcasted_iota(I32, (LANES, SSD_INNER), 1) // gw
    state[...] = jnp.where(rg == cg, jnp.exp(a_last) * state[...] + upd, 0.0)
    xbuf[...] = xbc[q - SUBLANES:q, :]
    return out


def _ssd_kernel(xbc_ref, dt_ref, z_ref, cw_ref, cb_ref, dtb_ref, aneg_ref, d_ref, gn_ref, rexp_ref,
                o_ref, st_ref, buf_ref, xbuf, state):
    @pl.when(pl.program_id(1) == 0)
    def _():
        xbuf[...] = jnp.zeros_like(xbuf)
        state[...] = jnp.zeros_like(state)

    xbc = xbc_ref[...]
    out = _ssd_chunk(xbc, dt_ref[...], z_ref[...].astype(F32), cw_ref, cb_ref, dtb_ref, aneg_ref, d_ref, gn_ref,
                     rexp_ref, xbuf, state)
    o_ref[...] = out.astype(BF16)
    st_ref[...] = state[...]
    buf_ref[...] = xbc[SSD_CHUNK - (CONV_K - 1):SSD_CHUNK, :]


def _in_mix_kernel(*refs, rows, tiles_per_seq):
    x_ref, g_ref, sc_ref, sh_ref, w_ref = refs[:5]
    lru_p = refs[5:12]
    ssd_p = refs[12:19]
    qkv_ref, gt_ref, ol_ref, lh_ref, lbuf_ref, os_ref, st_ref, sbuf_ref = refs[19:27]
    xbuf_l, hcar, xbuf_s, state = refs[27:]

    @pl.when(pl.program_id(0) % tiles_per_seq == 0)
    def _():
        for s in (xbuf_l, hcar, xbuf_s, state):
            s[...] = jnp.zeros_like(s)

    h = (_rms(x_ref[...], g_ref[...]) * (1.0 + sc_ref[...]) + sh_ref[...]).astype(BF16)

    def mm(lo, hi):
        return jnp.dot(h, w_ref[:, lo:hi], preferred_element_type=F32)

    lx = mm(C_LX, C_LG)
    gate = mm(C_LG, C_Z)
    z = mm(C_Z, C_XBC)
    xbc = mm(C_XBC, C_GT)
    dt = mm(C_DT, IN_PACKED)
    qkv_ref[...] = mm(C_QKV, C_LX).astype(BF16)
    gt_ref[...] = mm(C_GT, C_DT).astype(BF16)
    o, carry = _lru_scan(lx, gate, *lru_p, xbuf_l, hcar, rows)
    ol_ref[...] = o.astype(BF16)
    lh_ref[...] = carry
    lbuf_ref[...] = lx[rows - (CONV_K - 1):rows, :]
    for c in range(rows // SSD_CHUNK):
        sl = slice(c * SSD_CHUNK, (c + 1) * SSD_CHUNK)
        out = _ssd_chunk(xbc[sl], dt[sl], z[sl], *ssd_p, xbuf_s, state)
        os_ref[sl, :] = out.astype(BF16)
    st_ref[...] = state[...]
    sbuf_ref[...] = xbc[rows - (CONV_K - 1):rows, :]


def _in_proj_mix(x, layer, g_all, mod, w_all, lru_params, ssd_params, rexp, tm, batch, seq_len):
    m = x.shape[0]
    tps = seq_len // tm
    row = lambda w: pl.BlockSpec((tm, w), lambda i: (i, 0))
    per_seq = lambda r, w: pl.BlockSpec((None, r, w), lambda i: (i // tps, 0, 0))
    return pl.pallas_call(
        functools.partial(_in_mix_kernel, rows=tm, tiles_per_seq=tps),
        grid=(m // tm,),
        in_specs=[row(D_MODEL), _layer_spec((1, D_MODEL), layer, 1), mod.spec(layer, 1, tm), mod.spec(layer, 0, tm),
                  _layer_spec((D_MODEL, IN_PACKED), layer, 1)] + _lru_param_specs(layer, 1)
                 + _ssd_param_specs(layer, 1) + [pl.BlockSpec((LANES, SSD_INNER), lambda i: (0, 0))],
        out_specs=[row(QKV_W), row(GATES_W), row(LRU_WIDTH), per_seq(1, LRU_WIDTH), per_seq(CONV_K - 1, LRU_WIDTH),
                   row(SSD_INNER), per_seq(LANES, SSD_INNER), per_seq(CONV_K - 1, SSD_CONV_DIM)],
        out_shape=[jax.ShapeDtypeStruct((m, QKV_W), BF16),
                   jax.ShapeDtypeStruct((m, GATES_W), BF16),
                   jax.ShapeDtypeStruct((m, LRU_WIDTH), BF16),
                   jax.ShapeDtypeStruct((batch, 1, LRU_WIDTH), F32),
                   jax.ShapeDtypeStruct((batch, CONV_K - 1, LRU_WIDTH), F32),
                   jax.ShapeDtypeStruct((m, SSD_INNER), BF16),
                   jax.ShapeDtypeStruct((batch, LANES, SSD_INNER), F32),
                   jax.ShapeDtypeStruct((batch, CONV_K - 1, SSD_CONV_DIM), F32)],
        scratch_shapes=[pltpu.VMEM((SUBLANES, LRU_WIDTH), F32), pltpu.VMEM((1, LRU_WIDTH), F32),
                        pltpu.VMEM((SUBLANES, SSD_CONV_DIM), F32), pltpu.VMEM((LANES, SSD_INNER), F32)],
        compiler_params=_params(("arbitrary",)),
        name="in_proj_mix",
    )(x, g_all, mod.arr, mod.arr, w_all, *lru_params, *ssd_params, rexp)


def _ssd_param_stack(ssd_conv_w, ssd_conv_b, ssd_dt_bias, ssd_a_log, ssd_d, g_ssd_norm):
    pad = lambda v: jnp.pad(v.astype(F32), ((0, 0), (0, LANES - SSD_HEADS)))[:, None, :]
    return (ssd_conv_w.astype(F32), ssd_conv_b.astype(F32)[:, None, :], pad(ssd_dt_bias),
            pad(-jnp.exp(ssd_a_log.astype(F32))),
            jnp.repeat(ssd_d.astype(F32), SSD_HEAD_DIM, axis=1)[:, None, :],
            g_ssd_norm.astype(F32)[:, None, :])


def _ssd_param_specs(layer, ngrid):
    ls = lambda shape: _layer_spec(shape, layer, ngrid)
    return [ls((CONV_K, SSD_CONV_DIM)), ls((1, SSD_CONV_DIM)), ls((1, LANES)), ls((1, LANES)),
            ls((1, SSD_INNER)), ls((1, SSD_INNER))]


def _head_expand_matrix():
    h = np.arange(LANES)[:, None]
    c = np.arange(SSD_INNER)[None, :]
    return jnp.asarray((h == c // SSD_HEAD_DIM).astype(np.float32), dtype=BF16)


def _ssd_prompt(xbc, dtr, z, layer, ssd_params, rexp, batch, seq_len):
    nc = seq_len // SSD_CHUNK
    tok = lambda w: pl.BlockSpec((SSD_CHUNK, w), lambda b, n: (b * nc + n, 0))
    return pl.pallas_call(
        _ssd_kernel,
        grid=(batch, nc),
        in_specs=[tok(SSD_CONV_DIM), tok(LANES), tok(SSD_INNER)] + _ssd_param_specs(layer, 2)
                 + [pl.BlockSpec((LANES, SSD_INNER), lambda b, n: (0, 0))],
        out_specs=[tok(SSD_INNER),
                   pl.BlockSpec((None, LANES, SSD_INNER), lambda b, n: (b, 0, 0)),
                   pl.BlockSpec((None, CONV_K - 1, SSD_CONV_DIM), lambda b, n: (b, 0, 0))],
        out_shape=[jax.ShapeDtypeStruct((batch * seq_len, SSD_INNER), BF16),
                   jax.ShapeDtypeStruct((batch, LANES, SSD_INNER), F32),
                   jax.ShapeDtypeStruct((batch, CONV_K - 1, SSD_CONV_DIM), F32)],
        scratch_shapes=[pltpu.VMEM((SUBLANES, SSD_CONV_DIM), F32), pltpu.VMEM((LANES, SSD_INNER), F32)],
        compiler_params=_params(("arbitrary", "arbitrary")),
        name="ssd_prompt",
    )(xbc, dtr, z, *ssd_params, rexp)


def _ssd_state_to_hpn(st):
    lead = st.shape[:-2]
    hpg = SSD_HEADS // SSD_GROUPS
    gw = SSD_INNER // SSD_GROUPS
    parts = []
    for g in range(SSD_GROUPS):
        blk = st[..., g * SSD_STATE:(g + 1) * SSD_STATE, g * gw:(g + 1) * gw]
        blk = blk.reshape(lead + (SSD_STATE, hpg, SSD_HEAD_DIM))
        parts.append(jnp.moveaxis(blk, -3, -1))
    return jnp.concatenate(parts, axis=-3)


def _merge_kernel(*refs, with_router):
    (oa_ref, ol_ref, os_ref, gt_ref, x_ref, g1_ref, sc2_ref, sh2_ref, gn_ref,
     wa_ref, wl_ref, ws_ref, wo_ref) = refs[:13]
    if with_router:
        wr_ref, br_ref, x1_ref, h2_ref, lg_ref = refs[13:]
    else:
        x1_ref, h2_ref = refs[13:]
    d = D_MODEL
    mm = lambda a, w: jnp.dot(a[...], w[...], preferred_element_type=F32)
    gate = lambda lo: _sigmoid(gt_ref[:, lo:lo + d].astype(F32))
    merged = (gate(0) * mm(oa_ref, wa_ref) + gate(d) * mm(ol_ref, wl_ref) + gate(2 * d) * mm(os_ref, ws_ref))
    mix = jnp.dot(merged.astype(BF16), wo_ref[...], preferred_element_type=F32)
    x1 = x_ref[...] + g1_ref[...] * mix
    x1_ref[...] = x1
    h2 = _rms(x1, gn_ref[...]) * (1.0 + sc2_ref[...]) + sh2_ref[...]
    h2_ref[...] = h2.astype(h2_ref.dtype)
    if with_router:
        hi = h2.astype(BF16)
        lo = (h2 - hi.astype(F32)).astype(BF16)
        wr = wr_ref[...]
        t = jnp.dot(hi, wr, preferred_element_type=F32) + jnp.dot(lo, wr, preferred_element_type=F32)
        lg_ref[...] = t + pltpu.roll(t, LANES - N_EXPERTS, 1) + br_ref[...]


def _router_stack(moe_w_router, moe_b_router):
    w = moe_w_router.astype(F32)
    hi = w.astype(BF16)
    lo = (w - hi.astype(F32)).astype(BF16)
    n = w.shape[0]
    wr = jnp.concatenate([hi, lo, jnp.zeros((n, D_MODEL, LANES - 2 * N_EXPERTS), BF16)], axis=2)
    br = jnp.pad(moe_b_router.astype(F32), ((0, 0), (0, LANES - N_EXPERTS)))[:, None, :]
    return wr, br


def _merge(oa, ol, os_, gt, x, layer, mod, gn_all, wa_all, wl_all, ws_all, wo_all, router, tm):
    m = x.shape[0]
    row = lambda w: pl.BlockSpec((tm, w), lambda i: (i, 0))
    ls = lambda shape: _layer_spec(shape, layer, 1)
    in_specs = [row(ATTN_WIDTH), row(LRU_WIDTH), row(SSD_INNER), row(GATES_W), row(D_MODEL),
                mod.spec(layer, 2, tm), mod.spec(layer, 4, tm), mod.spec(layer, 3, tm),
                ls((1, D_MODEL)), ls((ATTN_WIDTH, D_MODEL)), ls((LRU_WIDTH, D_MODEL)),
                ls((SSD_INNER, D_MODEL)), ls((D_MODEL, D_MODEL))]
    args = [oa, ol, os_, gt, x, mod.arr, mod.arr, mod.arr, gn_all, wa_all, wl_all, ws_all, wo_all]
    out_specs = [row(D_MODEL), row(D_MODEL)]
    out_shape = [jax.ShapeDtypeStruct((m, D_MODEL), F32),
                 jax.ShapeDtypeStruct((m, D_MODEL), F32 if router is not None else BF16)]
    if router is not None:
        wr_all, br_all, j = router
        in_specs += [_layer_spec((D_MODEL, LANES), j, 1), _layer_spec((1, LANES), j, 1)]
        args += [wr_all, br_all]
        out_specs.append(row(LANES))
        out_shape.append(jax.ShapeDtypeStruct((m, LANES), F32))
    return pl.pallas_call(
        functools.partial(_merge_kernel, with_router=router is not None),
        grid=(m // tm,),
        in_specs=in_specs, out_specs=out_specs, out_shape=out_shape,
        compiler_params=_params(("arbitrary",)),
        name="merge",
    )(*args)


def _ffn_kernel(te_ref, nv_ref, *refs, fused_residual):
    if fused_residual:
        x_ref, wg_ref, wu_ref, wd_ref, res_ref, g2_ref, o_ref = refs
    else:
        x_ref, wg_ref, wu_ref, wd_ref, o_ref = refs
    i = pl.program_id(0)
    j = pl.program_id(1)

    @pl.when(j == 0)
    def _():
        o_ref[...] = jnp.zeros_like(o_ref)

    @pl.when(i < nv_ref[0])
    def _():
        h = x_ref[...].astype(BF16)
        a = jnp.dot(h, wg_ref[...], preferred_element_type=F32)
        b = jnp.dot(h, wu_ref[...], preferred_element_type=F32)
        t = (_silu(a) * b).astype(BF16)
        o_ref[...] += jnp.dot(t, wd_ref[...], preferred_element_type=F32)

    if fused_residual:
        @pl.when(j == pl.num_programs(1) - 1)
        def _():
            o_ref[...] = res_ref[...] + g2_ref[...] * o_ref[...]


def _ffn(x, wg, wu, wd, tile_expert, n_valid, tm, expert_base=0, residual=None, mod=None, layer=None):
    m = x.shape[0]
    nj = D_FF // FF_CHUNK
    last = nj - 1

    def jj(i, j, nv):
        return jnp.where(i < nv[0], j, last)

    in_specs = [pl.BlockSpec((tm, D_MODEL), lambda i, j, te, nv: (jnp.minimum(i, nv[0] - 1), 0)),
                pl.BlockSpec((None, D_MODEL, FF_CHUNK), lambda i, j, te, nv: (expert_base + te[i], 0, jj(i, j, nv))),
                pl.BlockSpec((None, D_MODEL, FF_CHUNK), lambda i, j, te, nv: (expert_base + te[i], 0, jj(i, j, nv))),
                pl.BlockSpec((None, FF_CHUNK, D_MODEL), lambda i, j, te, nv: (expert_base + te[i], jj(i, j, nv), 0))]
    args = [x, wg, wu, wd]
    fused = residual is not None
    if fused:
        in_specs += [pl.BlockSpec((tm, D_MODEL), lambda i, j, te, nv: (i, 0)), mod.spec(layer, 5, tm)]
        args += [residual, mod.arr]
    return pl.pallas_call(
        functools.partial(_ffn_kernel, fused_residual=fused),
        grid_spec=pltpu.PrefetchScalarGridSpec(
            num_scalar_prefetch=2, grid=(m // tm, nj), in_specs=in_specs,
            out_specs=pl.BlockSpec((tm, D_MODEL), lambda i, j, te, nv: (i, 0))),
        out_shape=jax.ShapeDtypeStruct((m, D_MODEL), F32),
        compiler_params=_params(("arbitrary", "arbitrary")),
        name="ffn",
    )(tile_expert, n_valid, *args)


def _route_kernel(lg_ref, p1_ref, p2_ref, w1_ref, w2_ref, te_ref, nv_ref, carry, e1s, e2s, r1s, r2s,
                  *, tb, tm, nblk, n_real):
    step = pl.program_id(0)

    @pl.when(step == 0)
    def _():
        carry[...] = jnp.zeros_like(carry)

    l8 = lg_ref[...].T[0:N_EXPERTS, :]
    e = lax.broadcasted_iota(I32, l8.shape, 0)
    m1 = jnp.max(l8, axis=0, keepdims=True)
    i1 = jnp.min(jnp.where(l8 == m1, e, N_EXPERTS), axis=0, keepdims=True)
    rest = jnp.where(e == i1, -jnp.inf, l8)
    m2 = jnp.max(rest, axis=0, keepdims=True)
    i2 = jnp.min(jnp.where(rest == m2, e, N_EXPERTS), axis=0, keepdims=True)
    t = jnp.exp(m2 - m1)
    w1 = 1.0 / (1.0 + t)
    real = step * tb + lax.broadcasted_iota(I32, l8.shape, 1) < n_real
    sel1 = jnp.where(real & (e == i1), 1.0, 0.0)
    sel2 = jnp.where(real & (e == i2), 1.0, 0.0)
    sel = sel1 + sel2
    a = lax.broadcasted_iota(I32, (tb, tb), 0)
    b = lax.broadcasted_iota(I32, (tb, tb), 1)
    upper = jnp.where(a <= b, 1.0, 0.0).astype(BF16)
    incl = jnp.dot(sel.astype(BF16), upper, preferred_element_type=F32)
    rank = carry[:, 0:1] + incl - sel
    e1s[pl.ds(step, 1), :] = i1
    e2s[pl.ds(step, 1), :] = i2
    r1s[pl.ds(step, 1), :] = jnp.sum(sel1 * rank, axis=0, keepdims=True).astype(I32)
    r2s[pl.ds(step, 1), :] = jnp.sum(sel2 * rank, axis=0, keepdims=True).astype(I32)
    w1_ref[...] = w1
    w2_ref[...] = t * w1
    carry[...] = carry[...] + incl[:, tb - 1:tb]

    @pl.when(step == nblk - 1)
    def _():
        counts = carry[...].astype(I32)
        shift = tm.bit_length() - 1
        padded = lax.shift_right_logical(counts + (tm - 1), shift) << shift
        ex = lax.broadcasted_iota(I32, padded.shape, 0)
        ends = padded
        d = 1
        while d < N_EXPERTS:
            ends = ends + jnp.where(ex >= d, pltpu.roll(ends, d, 0), 0)
            d *= 2
        starts = ends - padded
        total = ends[N_EXPERTS - 1:N_EXPERTS, :]
        nvalid = lax.shift_right_logical(total, shift)
        tile0 = lax.broadcasted_iota(I32, padded.shape, 1) * tm
        te = jnp.sum(jnp.where(ends <= tile0, 1, 0), axis=0, keepdims=True)
        te_last = jnp.sum(jnp.where(ends <= total - tm, 1, 0), axis=0, keepdims=True)
        te_ref[...] = jnp.where(tile0[0:1, :] < total, te, te_last)
        nv_ref[...] = nvalid
        reps = tb // LANES
        e1, e2 = e1s[...], e2s[...]
        off1 = jnp.zeros(e1.shape, I32)
        off2 = jnp.zeros(e2.shape, I32)
        for k in range(N_EXPERTS):
            row = starts[k:k + 1, :]
            row = jnp.concatenate([row] * reps, axis=1) if reps > 1 else row
            off1 = jnp.where(e1 == k, row, off1)
            off2 = jnp.where(e2 == k, row, off2)
        p1_ref[...] = off1 + r1s[...]
        p2_ref[...] = off2 + r2s[...]


def _route(logits, n_real, tb, tm):
    m = logits.shape[0]
    nblk = m // tb
    assert tb % LANES == 0 and -(-(n_real * TOP_K) // tm) + N_EXPERTS <= LANES and tm & (tm - 1) == 0
    vec = pl.BlockSpec((1, tb), lambda i: (0, i))
    res = pl.BlockSpec((nblk, tb), lambda i: (0, 0))
    one = pl.BlockSpec((1, LANES), lambda i: (0, 0))
    return pl.pallas_call(
        functools.partial(_route_kernel, tb=tb, tm=tm, nblk=nblk, n_real=n_real),
        grid=(nblk,),
        in_specs=[pl.BlockSpec((tb, LANES), lambda i: (i, 0))],
        out_specs=[res, res, vec, vec, one, one],
        out_shape=[jax.ShapeDtypeStruct((nblk, tb), I32), jax.ShapeDtypeStruct((nblk, tb), I32),
                   jax.ShapeDtypeStruct((1, m), F32), jax.ShapeDtypeStruct((1, m), F32),
                   jax.ShapeDtypeStruct((1, LANES), I32), jax.ShapeDtypeStruct((1, LANES), I32)],
        scratch_shapes=[pltpu.VMEM((N_EXPERTS, LANES), F32)] + [pltpu.VMEM((nblk, tb), I32)] * 4,
        compiler_params=_params(("arbitrary",)),
        name="route",
    )(logits)


def _dispatch_kernel(p1_ref, p2_ref, x_ref, dst_in_ref, dst_ref, sem, *, tb):
    del dst_in_ref

    def start(r, c):
        src = x_ref.at[pl.ds(r, 1), :]
        pltpu.make_async_copy(src, dst_ref.at[pl.ds(p1_ref[0, r], 1), :], sem.at[0]).start(priority=0)
        pltpu.make_async_copy(src, dst_ref.at[pl.ds(p2_ref[0, r], 1), :], sem.at[1]).start(priority=1)
        return c

    lax.fori_loop(0, tb, start, 0, unroll=8)
    row0 = x_ref.at[pl.ds(0, 1), :]
    for _ in range(tb):
        pltpu.make_async_copy(row0, dst_ref.at[pl.ds(0, 1), :], sem.at[0]).wait()
        pltpu.make_async_copy(row0, dst_ref.at[pl.ds(0, 1), :], sem.at[1]).wait()


def _dispatch(x, pos1, pos2, dst, tb):
    m = x.shape[0]
    smem = pl.BlockSpec((None, 1, tb), lambda i: (i, 0, 0), memory_space=pltpu.SMEM)
    return pl.pallas_call(
        functools.partial(_dispatch_kernel, tb=tb),
        grid=(m // tb,),
        in_specs=[smem, smem, pl.BlockSpec((tb, D_MODEL), lambda i: (i, 0)),
                  pl.BlockSpec(memory_space=pl.ANY)],
        out_specs=pl.BlockSpec(memory_space=pl.ANY),
        out_shape=jax.ShapeDtypeStruct(dst.shape, F32),
        scratch_shapes=[pltpu.SemaphoreType.DMA((2,))],
        input_output_aliases={3: 0},
        compiler_params=_params(("arbitrary",)),
        name="dispatch",
    )(pos1.reshape(m // tb, 1, tb), pos2.reshape(m // tb, 1, tb), x, dst)


def _combine_kernel(p1_ref, p2_ref, y_ref, x1_ref, g2_ref, w1_ref, w2_ref, o_ref, buf1, buf2, sem, *, tb):
    def start(r, c):
        pltpu.make_async_copy(y_ref.at[pl.ds(p1_ref[0, r], 1), :], buf1.at[pl.ds(r, 1), :],
                              sem.at[0]).start(priority=0)
        pltpu.make_async_copy(y_ref.at[pl.ds(p2_ref[0, r], 1), :], buf2.at[pl.ds(r, 1), :],
                              sem.at[1]).start(priority=1)
        return c

    lax.fori_loop(0, tb, start, 0, unroll=8)
    for _ in range(tb):
        pltpu.make_async_copy(y_ref.at[pl.ds(0, 1), :], buf1.at[pl.ds(0, 1), :], sem.at[0]).wait()
        pltpu.make_async_copy(y_ref.at[pl.ds(0, 1), :], buf2.at[pl.ds(0, 1), :], sem.at[1]).wait()
    f = w1_ref[...] * buf1[...] + w2_ref[...] * buf2[...]
    o_ref[...] = x1_ref[...] + g2_ref[...] * f


def _combine(y, pos1, pos2, w1, w2, x1, layer, mod, tb):
    m = x1.shape[0]
    smem = pl.BlockSpec((None, 1, tb), lambda i: (i, 0, 0), memory_space=pltpu.SMEM)
    row = lambda w: pl.BlockSpec((tb, w), lambda i: (i, 0))
    return pl.pallas_call(
        functools.partial(_combine_kernel, tb=tb),
        grid=(m // tb,),
        in_specs=[smem, smem, pl.BlockSpec(memory_space=pl.ANY), row(D_MODEL),
                  mod.spec(layer, 5, tb), row(1), row(1)],
        out_specs=row(D_MODEL),
        out_shape=jax.ShapeDtypeStruct((m, D_MODEL), F32),
        scratch_shapes=[pltpu.VMEM((tb, D_MODEL), F32), pltpu.VMEM((tb, D_MODEL), F32),
                        pltpu.SemaphoreType.DMA((2,))],
        compiler_params=_params(("arbitrary",)),
        name="combine",
    )(pos1.reshape(m // tb, 1, tb), pos2.reshape(m // tb, 1, tb), y, x1, mod.arr,
      w1.reshape(m, 1), w2.reshape(m, 1))


def _moe_tiles(n_tokens, tm):
    return -(-(n_tokens * TOP_K) // tm) + N_EXPERTS


def _moe(groups, layer, wg, wu, wd, expert_base, sorted_buf, tm, tb):
    (h2_a, lg_a, x1_a, mod_a), (h2_b, lg_b, x1_b, mod_b) = groups
    ma, mb = h2_a.shape[0], h2_b.shape[0]
    assert ma % tb == 0 and mb <= tb and mb % LANES == 0
    n_tiles = _moe_tiles(ma + mb, tm)
    logits = jnp.concatenate([lg_a, lg_b, jnp.zeros((tb - mb, LANES), F32)], axis=0)
    pos1, pos2, w1, w2, te, nv = _route(logits, ma + mb, tb, tm)
    na = ma // tb
    hs = _dispatch(h2_a, pos1[:na], pos2[:na], sorted_buf, tb)
    hs = _dispatch(h2_b, pos1[na, :mb], pos2[na, :mb], hs, mb)
    y = _ffn(hs, wg, wu, wd, te[0, :n_tiles], nv[0, :1], tm, expert_base=expert_base)
    out_a = _combine(y, pos1[:na], pos2[:na], w1[0, :ma], w2[0, :ma], x1_a, layer, mod_a, tb)
    out_b = _combine(y, pos1[na, :mb], pos2[na, :mb], w1[0, ma:ma + mb], w2[0, ma:ma + mb], x1_b, layer, mod_b, mb)
    return out_a, out_b, hs


def _attn_decode_kernel(q_ref, ck_ref, cv_ref, kn_ref, vn_ref, sink_ref, o_ref, kw_ref, vw_ref):
    q = q_ref[...].astype(BF16)
    ck, cv = ck_ref[...], cv_ref[...]
    kn, vn = kn_ref[...], vn_ref[...]
    s = jnp.einsum('bhd,bwd->bhw', q, ck.astype(BF16), preferred_element_type=F32)
    knb = kn.astype(BF16).astype(F32)
    s_new = jnp.sum(q.astype(F32) * knb[:, None, :], axis=-1, keepdims=True)
    sink = sink_ref[...]
    m = jnp.maximum(jnp.maximum(jnp.max(s, axis=-1, keepdims=True), s_new), sink)
    p = jnp.exp(s - m)
    pn = jnp.exp(s_new - m)
    inv = 1.0 / (jnp.sum(p, axis=-1, keepdims=True) + pn + jnp.exp(sink - m))
    o = jnp.einsum('bhw,bwd->bhd', (p * inv).astype(BF16), cv.astype(BF16), preferred_element_type=F32)
    o_ref[...] = o + (pn * inv).astype(BF16).astype(F32) * vn.astype(BF16).astype(F32)[:, None, :]
    w = lax.broadcasted_iota(I32, ck.shape, 1)
    last = w == ck.shape[1] - 1
    kw_ref[...] = jnp.where(last, kn[:, None, :], pltpu.roll(ck, ck.shape[1] - 1, 1))
    vw_ref[...] = jnp.where(last, vn[:, None, :], pltpu.roll(cv, cv.shape[1] - 1, 1))


def _attention_decode(qkv, layer, ck_all, cv_all, sinks_col, tb):
    n = qkv.shape[0]
    win = ck_all.shape[2]
    qh = qkv[:, :ATTN_WIDTH].reshape(n, N_KV_HEADS, GQA_GROUP, HEAD_DIM)
    z = jnp.zeros_like(qh[:, 0])
    q_pl = jnp.concatenate([jnp.concatenate([qh[:, 0], z], axis=-1),
                            jnp.concatenate([z, qh[:, 1]], axis=-1)], axis=1)
    kn = qkv[:, ATTN_WIDTH:ATTN_WIDTH + KV_WIDTH].astype(F32)
    vn = qkv[:, ATTN_WIDTH + KV_WIDTH:].astype(F32)
    blk3 = lambda a, b: pl.BlockSpec((tb, a, b), lambda i: (i, 0, 0))
    cache = pl.BlockSpec((None, tb, win, KV_WIDTH), lambda i: (layer, i, 0, 0))
    blk2 = pl.BlockSpec((tb, KV_WIDTH), lambda i: (i, 0))
    o, ck_all, cv_all = pl.pallas_call(
        _attn_decode_kernel,
        grid=(n // tb,),
        in_specs=[blk3(N_Q_HEADS, LANES), cache, cache, blk2, blk2,
                  pl.BlockSpec((None, N_Q_HEADS, 1), lambda i: (layer, 0, 0))],
        out_specs=[blk3(N_Q_HEADS, LANES), cache, cache],
        out_shape=[jax.ShapeDtypeStruct((n, N_Q_HEADS, LANES), F32),
                   jax.ShapeDtypeStruct(ck_all.shape, F32),
                   jax.ShapeDtypeStruct(cv_all.shape, F32)],
        input_output_aliases={1: 1, 2: 2},
        compiler_params=_params(("arbitrary",)),
        name="attn_decode",
    )(q_pl, ck_all, cv_all, kn, vn, sinks_col)
    o = o.reshape(n, N_KV_HEADS, GQA_GROUP, N_KV_HEADS, HEAD_DIM)
    o = jnp.concatenate([o[:, 0, :, 0], o[:, 1, :, 1]], axis=1).reshape(n, ATTN_WIDTH)
    return o.astype(BF16), ck_all, cv_all


def _lru_decode_kernel(x_ref, gate_ref, buf_ref, h0_ref, cw_ref, cb_ref, wa_ref, ba_ref, wx_ref, bx_ref,
                       negsp_ref, o_ref, h_ref, nbuf_ref):
    x = x_ref[...]
    xc = x * cw_ref[CONV_K - 1:CONV_K, :] + cb_ref[...]
    for j in range(CONV_K - 1):
        xc = xc + buf_ref[:, j, :] * cw_ref[j:j + 1, :]
    a, u = _lru_gates(xc, wa_ref, ba_ref, wx_ref, bx_ref, negsp_ref)
    h = a * h0_ref[...] + u
    o_ref[...] = (h * _gelu_tanh(gate_ref[...].astype(F32))).astype(BF16)
    h_ref[...] = h
    for j in range(CONV_K - 2):
        nbuf_ref[:, j, :] = buf_ref[:, j + 1, :]
    nbuf_ref[:, CONV_K - 2, :] = x


def _lru_decode(lx, lgate, layer, buf_all, h0_all, lru_params):
    n = lx.shape[0]
    one = lambda shape: pl.BlockSpec(shape, lambda i: (0,) * len(shape))
    lay = lambda shape: pl.BlockSpec((None,) + shape, lambda i: (layer,) + (0,) * len(shape))
    nb = LRU_WIDTH // LANES
    return pl.pallas_call(
        _lru_decode_kernel,
        in_specs=[one((n, LRU_WIDTH)), one((n, LRU_WIDTH)), lay((n, CONV_K - 1, LRU_WIDTH)), lay((n, LRU_WIDTH)),
                  lay((CONV_K, LRU_WIDTH)), lay((1, LRU_WIDTH)), lay((nb, LANES, LANES)), lay((1, LRU_WIDTH)),
                  lay((nb, LANES, LANES)), lay((1, LRU_WIDTH)), lay((1, LRU_WIDTH))],
        out_shape=[jax.ShapeDtypeStruct((n, LRU_WIDTH), BF16),
                   jax.ShapeDtypeStruct((n, LRU_WIDTH), F32),
                   jax.ShapeDtypeStruct((n, CONV_K - 1, LRU_WIDTH), F32)],
        grid=(1,),
        compiler_params=_params(("arbitrary",)),
        name="lru_decode",
    )(lx, lgate, buf_all, h0_all, *lru_params)


def _ssd_pre_decode_kernel(xbc_ref, dt_ref, buf_ref, cw_ref, cb_ref, dtb_ref, aneg_ref, rexp_ref,
                           xs_ref, xdt_ref, dec_ref, bm_ref, cm_ref, nbuf_ref):
    xbc = xbc_ref[...]
    xc = xbc * cw_ref[CONV_K - 1:CONV_K, :] + cb_ref[...]
    for j in range(CONV_K - 1):
        xc = xc + buf_ref[:, j, :] * cw_ref[j:j + 1, :]
    xc = _silu(xc)
    xs = xc[:, :SSD_INNER]
    rexp = rexp_ref[...]
    dt = _softplus(dt_ref[...] + dtb_ref[...])
    xs_ref[...] = xs
    xdt_ref[...] = xs * _dot_exact_rhs(dt, rexp)
    dec_ref[...] = jnp.exp(_dot_exact_rhs(dt * aneg_ref[...], rexp))
    bm_ref[...] = xc[:, SSD_INNER:SSD_INNER + LANES]
    cm_ref[...] = xc[:, SSD_INNER + LANES:]
    for j in range(CONV_K - 2):
        nbuf_ref[:, j, :] = buf_ref[:, j + 1, :]
    nbuf_ref[:, CONV_K - 2, :] = xbc


def _ssd_state_decode_kernel(h0_ref, x_ref, dec_ref, bm_ref, cm_ref, rep_ref, tile_ref, sum_ref,
                             hn_ref, y_ref, *, tb):
    pn = SSD_HEAD_DIM * SSD_STATE
    x = x_ref[...].reshape(tb * SSD_HEADS, SSD_HEAD_DIM).astype(BF16)
    xrep = jnp.dot(x, rep_ref[...], preferred_element_type=F32).reshape(tb, SSD_HEADS, pn)
    bmb, cmb = bm_ref[...].astype(BF16), cm_ref[...].astype(BF16)
    first = lax.broadcasted_iota(I32, (tb, SSD_HEADS, pn), 1) < SSD_HEADS // SSD_GROUPS

    def tiled(v):
        t0 = jnp.dot(v, tile_ref[0], preferred_element_type=F32)
        t1 = jnp.dot(v, tile_ref[1], preferred_element_type=F32)
        return jnp.where(first, t0[:, None, :], t1[:, None, :])

    h_new = dec_ref[...][:, :, 0:1] * h0_ref[...] + xrep * tiled(bmb)
    hn_ref[...] = h_new
    hc = (h_new * tiled(cmb)).reshape(tb * SSD_HEADS, pn).astype(BF16)
    y_ref[...] = jnp.dot(hc, sum_ref[...], preferred_element_type=F32).reshape(tb, SSD_HEADS, SSD_HEAD_DIM)


def _ssd_post_decode_kernel(y_ref, xs_ref, z_ref, d_ref, gn_ref, o_ref):
    y = y_ref[...] + d_ref[...] * xs_ref[...]
    o_ref[...] = _ssd_gate_norm(y, z_ref[...].astype(F32), gn_ref[...]).astype(BF16)


def _ssd_decode_consts():
    pn = SSD_HEAD_DIM * SSD_STATE
    col_p = np.arange(pn) // SSD_STATE
    col_n = np.arange(pn) % SSD_STATE
    rep = (np.arange(SSD_HEAD_DIM)[:, None] == col_p[None, :]).astype(np.float32)
    row_g = np.arange(LANES) // SSD_STATE
    row_n = np.arange(LANES) % SSD_STATE
    tile = np.stack([((row_g[:, None] == g) & (row_n[:, None] == col_n[None, :])).astype(np.float32)
                     for g in range(SSD_GROUPS)])
    summ = rep.T
    return jnp.asarray(rep, BF16), jnp.asarray(tile, BF16), jnp.asarray(summ, BF16)


def _ssd_decode(xbc, dtr, z, layer, buf_all, h0_all, ssd_params, rexp, dec_consts, tb):
    n = xbc.shape[0]
    pn = SSD_HEAD_DIM * SSD_STATE
    cw_all, cb_all, dtb_all, aneg_all, d_all, gn_all = ssd_params
    f = lambda w: jax.ShapeDtypeStruct((n, w), F32)
    one = lambda shape: pl.BlockSpec(shape, lambda i: (0,) * len(shape))
    lay = lambda shape: pl.BlockSpec((None,) + shape, lambda i: (layer,) + (0,) * len(shape))
    xs, xdt, dec, bm, cm, nbuf = pl.pallas_call(
        _ssd_pre_decode_kernel,
        in_specs=[one((n, SSD_CONV_DIM)), one((n, LANES)), lay((n, CONV_K - 1, SSD_CONV_DIM)),
                  lay((CONV_K, SSD_CONV_DIM)), lay((1, SSD_CONV_DIM)), lay((1, LANES)), lay((1, LANES)),
                  one((LANES, SSD_INNER))],
        out_shape=[f(SSD_INNER), f(SSD_INNER), f(SSD_INNER), f(LANES), f(LANES),
                   jax.ShapeDtypeStruct((n, CONV_K - 1, SSD_CONV_DIM), F32)],
        grid=(1,),
        compiler_params=_params(("arbitrary",)),
        name="ssd_pre_decode",
    )(xbc, dtr, buf_all, cw_all, cb_all, dtb_all, aneg_all, rexp)
    rep, tile, summ = dec_consts
    b3 = lambda a, b: pl.BlockSpec((tb, a, b), lambda i: (i, 0, 0))
    b2 = pl.BlockSpec((tb, LANES), lambda i: (i, 0))
    full = lambda shape: pl.BlockSpec(shape, lambda i: (0,) * len(shape))
    state = pl.BlockSpec((None, tb, SSD_HEADS, pn), lambda i: (layer, i, 0, 0))
    h0_all, y = pl.pallas_call(
        functools.partial(_ssd_state_decode_kernel, tb=tb),
        grid=(n // tb,),
        in_specs=[state, b3(SSD_HEADS, SSD_HEAD_DIM), b3(SSD_HEADS, SSD_HEAD_DIM), b2, b2,
                  full((SSD_HEAD_DIM, pn)), full((SSD_GROUPS, LANES, pn)), full((pn, SSD_HEAD_DIM))],
        out_specs=[state, b3(SSD_HEADS, SSD_HEAD_DIM)],
        out_shape=[jax.ShapeDtypeStruct(h0_all.shape, F32),
                   jax.ShapeDtypeStruct((n, SSD_HEADS, SSD_HEAD_DIM), F32)],
        input_output_aliases={0: 0},
        compiler_params=_params(("arbitrary",)),
        name="ssd_state_decode",
    )(h0_all, xdt.reshape(n, SSD_HEADS, SSD_HEAD_DIM), dec.reshape(n, SSD_HEADS, SSD_HEAD_DIM), bm, cm,
      rep, tile, summ)
    o = pl.pallas_call(
        _ssd_post_decode_kernel,
        in_specs=[one((n, SSD_INNER)), one((n, SSD_INNER)), one((n, SSD_INNER)),
                  lay((1, SSD_INNER)), lay((1, SSD_INNER))],
        out_shape=jax.ShapeDtypeStruct((n, SSD_INNER), BF16),
        grid=(1,),
        compiler_params=_params(("arbitrary",)),
        name="ssd_post_decode",
    )(y.reshape(n, SSD_INNER), xs, z, d_all, gn_all)
    return o, h0_all, nbuf


def _pack_w_in(w):
    dt0 = C_GT
    dt = jnp.pad(w[..., dt0:dt0 + SSD_HEADS].astype(BF16), ((0, 0), (0, 0), (0, LANES - SSD_HEADS)))
    return jnp.concatenate([w[..., :dt0].astype(BF16), w[..., dt0 + SSD_HEADS:].astype(BF16), dt], axis=-1)


def kernel(x_prompt, x_sample, c_prompt, c_sample, cache_k, cache_v, state_lru_h, state_lru_conv, state_ssd_h, state_ssd_conv, w_mod, b_mod, g_norm_mix, g_norm_ffn, w_in, g_q, g_k, attn_sinks, lru_conv_w, lru_conv_b, lru_w_a, lru_b_a, lru_w_x, lru_b_x, lru_lambda, ssd_conv_w, ssd_conv_b, ssd_dt_bias, ssd_a_log, ssd_d, g_ssd_norm, w_br_attn, w_br_lru, w_br_ssd, w_out, ffn_w_gate, ffn_w_up, ffn_w_down, moe_w_router, moe_b_router, moe_w_gate, moe_w_up, moe_w_down):
    bp, seq = x_prompt.shape[:2]
    ns = x_sample.shape[0]
    win = cache_k.shape[2]
    assert x_sample.shape[1] == 1
    assert win == WINDOW and PAST_LEN >= WINDOW
    mp = bp * seq
    tm_p, tm_s = 512, ns
    lru_rows = 256
    pn = SSD_HEAD_DIM * SSD_STATE

    mod = _modulation(jnp.concatenate([c_sample, c_prompt], axis=0), w_mod, b_mod)
    mod_s = _Mod(mod, True, ns, 1)
    mod_p = _Mod(mod[:, ns:].reshape(DEPTH * bp * 6, 1, D_MODEL), False, bp, seq)
    row3 = lambda v: v.astype(F32)[:, None, :]
    w_packed = _pack_w_in(w_in)
    g_mix, g_ffn = row3(g_norm_mix), row3(g_norm_ffn)
    gq2, gk2 = row3(jnp.tile(g_q, (1, 2))), row3(jnp.tile(g_k, (1, 2)))
    sinks = attn_sinks.astype(F32)
    sinks_col = sinks[:, :, None]
    e2 = _head_avg_matrix()
    tab_p = _rope_tables(jnp.arange(seq))
    tab_s = _rope_tables(jnp.full((1,), PAST_LEN))
    wa, wx, negsp = _lru_weight_stack(lru_w_a, lru_w_x, lru_lambda)
    lru_params = (lru_conv_w.astype(F32), row3(lru_conv_b), wa, row3(lru_b_a), wx, row3(lru_b_x), negsp)
    ssd_params = _ssd_param_stack(ssd_conv_w, ssd_conv_b, ssd_dt_bias, ssd_a_log, ssd_d, g_ssd_norm)
    rexp = _head_expand_matrix()
    dec_consts = _ssd_decode_consts()
    br = (w_br_attn.astype(BF16), w_br_lru.astype(BF16), w_br_ssd.astype(BF16), w_out.astype(BF16))
    wr_all, br_all = _router_stack(moe_w_router, moe_b_router)
    n_moe = moe_w_gate.shape[0]
    flat = lambda w: w.astype(BF16).reshape((n_moe * N_EXPERTS,) + w.shape[2:])
    moe_w = (flat(moe_w_gate), flat(moe_w_up), flat(moe_w_down))
    ffn_w = (ffn_w_gate.astype(BF16), ffn_w_up.astype(BF16), ffn_w_down.astype(BF16))
    ck_all = cache_k.reshape(DEPTH, ns, win, KV_WIDTH)
    cv_all = cache_v.reshape(DEPTH, ns, win, KV_WIDTH)
    ssd_h0_all = state_ssd_h.reshape(DEPTH, ns, SSD_HEADS, pn)
    sorted_buf = jnp.zeros((_moe_tiles(mp + ns, tm_p) * tm_p, D_MODEL), F32)

    xp = x_prompt.reshape(mp, D_MODEL)
    xs = x_sample.reshape(ns, D_MODEL)
    outs_p = [[] for _ in range(6)]
    outs_s = [[] for _ in range(3)]
    attn_bias = _attn_bias()

    for l in range(DEPTH):
        is_moe = l % 2 == 1
        j = l // 2
        router = (wr_all, br_all, j) if is_moe else None

        qkv, gt, o_lru, lru_h, lru_buf, o_ssd, ssd_st, ssd_buf = _in_proj_mix(
            xp, l, g_mix, mod_p, w_packed, lru_params, ssd_params, rexp, tm_p, bp, seq)
        o_attn, k_win, v_win = _attention_prompt(qkv, l, sinks, attn_bias, tab_p, gq2, gk2, e2, bp, seq)
        merged_p = _merge(o_attn, o_lru, o_ssd, gt, xp, l, mod_p, g_ffn, *br, router, tm_p)
        for acc, t in zip(outs_p, (k_win, v_win, lru_h, lru_buf, ssd_st, ssd_buf)):
            acc.append(t)

        qkv, lx, lg, z, xbc, gt, dtr = _in_proj(xs, l, g_mix, mod_s, w_packed, tm_s, (tab_s, gq2, gk2, e2))
        o_attn, ck_all, cv_all = _attention_decode(qkv, l, ck_all, cv_all, sinks_col, 32)
        o_lru, lru_h, lru_buf = _lru_decode(lx, lg, l, state_lru_conv, state_lru_h, lru_params)
        o_ssd, ssd_h0_all, ssd_buf = _ssd_decode(xbc, dtr, z, l, state_ssd_conv, ssd_h0_all, ssd_params, rexp,
                                                 dec_consts, 16)
        merged_s = _merge(o_attn, o_lru, o_ssd, gt, xs, l, mod_s, g_ffn, *br, router, tm_s)
        for acc, t in zip(outs_s, (lru_h, lru_buf, ssd_buf)):
            acc.append(t)

        if is_moe:
            (x1_p, h2_p, lg_p), (x1_s, h2_s, lg_s) = merged_p, merged_s
            xp, xs, sorted_buf = _moe(((h2_p, lg_p, x1_p, mod_p), (h2_s, lg_s, x1_s, mod_s)), l, *moe_w,
                                      j * N_EXPERTS, sorted_buf, tm_p, 512)
        else:
            (x1_p, h2_p), (x1_s, h2_s) = merged_p, merged_s
            nt = mp // tm_p
            xp = _ffn(h2_p, *ffn_w, jnp.zeros((nt,), I32), jnp.full((1,), nt, I32), tm_p,
                      expert_base=j, residual=x1_p, mod=mod_p, layer=l)
            xs = _ffn(h2_s, *ffn_w, jnp.zeros((1,), I32), jnp.ones((1,), I32), tm_s,
                      expert_base=j, residual=x1_s, mod=mod_s, layer=l)

    st = [jnp.stack(a) for a in outs_p]
    ss = [jnp.stack(a) for a in outs_s]
    kv5 = lambda t, b: t.reshape(DEPTH, b, WINDOW, N_KV_HEADS, HEAD_DIM)
    return (xp.reshape(bp, seq, D_MODEL), xs.reshape(ns, 1, D_MODEL),
            kv5(st[0], bp), kv5(st[1], bp), st[2].reshape(DEPTH, bp, LRU_WIDTH), st[3],
            _ssd_state_to_hpn(st[4]), st[5],
            kv5(ck_all, ns), kv5(cv_all, ns), ss[0], ss[1],
            ssd_h0_all.reshape(DEPTH, ns, SSD_HEADS, SSD_HEAD_DIM, SSD_STATE), ss[2])
```

```python
import functools
import math

import numpy as np
import jax
import jax.numpy as jnp
from jax import lax
from jax.experimental import pallas as pl
from jax.experimental.pallas import tpu as pltpu

F32 = jnp.float32
BF16 = jnp.bfloat16
I32 = jnp.int32

D_MODEL = 1024
DEPTH = 4
PAST_LEN = 8192
N_Q_HEADS = 8
N_KV_HEADS = 2
HEAD_DIM = 64
GQA_GROUP = N_Q_HEADS // N_KV_HEADS
WINDOW = 128
ATTN_BLOCK = 128
ROT_DIM = HEAD_DIM // 4
ROPE_THETA = 500000.0
ATTN_WIDTH = N_Q_HEADS * HEAD_DIM
KV_WIDTH = N_KV_HEADS * HEAD_DIM
LRU_WIDTH = 512
LRU_BLOCKS = 8
LRU_BLOCK_DIM = LRU_WIDTH // LRU_BLOCKS
LRU_C = 8.0
CONV_K = 4
SSD_HEADS = 8
SSD_HEAD_DIM = 64
SSD_INNER = SSD_HEADS * SSD_HEAD_DIM
SSD_GROUPS = 2
SSD_STATE = 64
SSD_CHUNK = 128
SSD_CONV_DIM = SSD_INNER + 2 * SSD_GROUPS * SSD_STATE
D_FF = 2816
N_EXPERTS = 8
TOP_K = 2
EPS = 1e-6

LANES = 128
SUBLANES = 8
QKV_W = ATTN_WIDTH + 2 * KV_WIDTH
GATES_W = 3 * D_MODEL
C_QKV = 0
C_LX = C_QKV + QKV_W
C_LG = C_LX + LRU_WIDTH
C_Z = C_LG + LRU_WIDTH
C_XBC = C_Z + SSD_INNER
C_GT = C_XBC + SSD_CONV_DIM
C_DT = C_GT + GATES_W
IN_PACKED = C_DT + LANES
FF_CHUNK = D_FF // 2
VMEM_LIMIT = 56 * 1024 * 1024
TOKEN_TILE = 512
ROUTE_BLOCK = 512
DECODE_ATTN_BLOCK = 32
DECODE_STATE_BLOCK = 16


def _params(sem=None):
    if sem is None:
        return pltpu.CompilerParams(vmem_limit_bytes=VMEM_LIMIT)
    return pltpu.CompilerParams(dimension_semantics=sem, vmem_limit_bytes=VMEM_LIMIT)


def _layer_spec(shape, layer, ngrid):
    zeros = (0,) * len(shape)
    return pl.BlockSpec((None,) + tuple(shape), lambda *g: (layer,) + zeros)


def _sigmoid(x):
    return 0.5 * jnp.tanh(0.5 * x) + 0.5


def _silu(x):
    return x * _sigmoid(x)


def _softplus(x):
    return jnp.maximum(x, 0.0) + jnp.log(1.0 + jnp.exp(-jnp.abs(x)))


def _gelu_tanh(x):
    return 0.5 * x * (1.0 + jnp.tanh(math.sqrt(2.0 / math.pi) * (x + 0.044715 * (x * x * x))))


def _split3(a):
    a1 = a.astype(BF16)
    r1 = a - a1.astype(F32)
    a2 = r1.astype(BF16)
    a3 = (r1 - a2.astype(F32)).astype(BF16)
    return a1, a2, a3


def _dot_exact_rhs(a, m):
    a1, a2, a3 = _split3(a)
    d = lambda t: jnp.dot(t, m, preferred_element_type=F32)
    return d(a1) + d(a2) + d(a3)


def _dot_exact_lhs(m, a):
    a1, a2, a3 = _split3(a)
    d = lambda t: jnp.dot(m, t, preferred_element_type=F32)
    return d(a1) + d(a2) + d(a3)


def _rms(x, g):
    return x * lax.rsqrt(jnp.mean(x * x, axis=-1, keepdims=True) + EPS) * g


def _mod_kernel(c_ref, w_ref, b_ref, o_ref):
    s = _silu(c_ref[...]).astype(BF16)
    o_ref[...] = jnp.dot(s, w_ref[...].astype(BF16), preferred_element_type=F32) + b_ref[...]


def _modulation(c, w_mod, b_mod):
    rows = c.shape[0]
    tn = 1536
    nt = (6 * D_MODEL) // tn
    return pl.pallas_call(
        _mod_kernel,
        grid=(DEPTH, nt),
        in_specs=[
            pl.BlockSpec((rows, D_MODEL), lambda l, j: (0, 0)),
            pl.BlockSpec((None, D_MODEL, tn), lambda l, j: (l, 0, j)),
            pl.BlockSpec((None, 1, tn), lambda l, j: (l, 0, j)),
        ],
        out_specs=pl.BlockSpec((None, rows, tn), lambda l, j: (l, 0, j)),
        out_shape=jax.ShapeDtypeStruct((DEPTH, rows, 6 * D_MODEL), F32),
        compiler_params=_params(("arbitrary", "arbitrary")),
        name="modulation",
    )(c, w_mod, b_mod.reshape(DEPTH, 1, 6 * D_MODEL))


class _Mod:
    def __init__(self, arr, per_row, batch, seq_len):
        self.arr, self.per_row, self.batch, self.seq_len = arr, per_row, batch, seq_len

    def spec(self, layer, k, tm, nprefetch=0):
        if self.per_row:
            return pl.BlockSpec((None, tm, D_MODEL), lambda i, *_: (layer, i, k))
        b, s = self.batch, self.seq_len
        return pl.BlockSpec((None, 1, D_MODEL), lambda i, *_: ((layer * b + (i * tm) // s) * 6 + k, 0, 0))


def _in_kernel(*refs, qk_norm):
    x_ref, g_ref, sc_ref, sh_ref, w_ref = refs[:5]
    qkv_ref, lx_ref, lg_ref, z_ref, xbc_ref, gt_ref, dt_ref = refs[-7:]
    h = (_rms(x_ref[...], g_ref[...]) * (1.0 + sc_ref[...]) + sh_ref[...]).astype(BF16)

    def mm(lo, hi):
        return jnp.dot(h, w_ref[:, lo:hi], preferred_element_type=F32)

    if qk_norm:
        cos_ref, slo_ref, shi_ref, gq_ref, gk_ref, e2_ref = refs[5:11]
        cos_t, slo, shi, e2 = cos_ref[...], slo_ref[...], shi_ref[...], e2_ref[...]
        for j in range(ATTN_WIDTH // LANES):
            q_slab = _head_norm_rope(mm(j * LANES, (j + 1) * LANES), gq_ref[...], e2, cos_t, slo, shi)
            qkv_ref[:, j * LANES:(j + 1) * LANES] = (q_slab * (HEAD_DIM ** -0.5)).astype(BF16)
        k_slab = _head_norm_rope(mm(ATTN_WIDTH, ATTN_WIDTH + KV_WIDTH), gk_ref[...], e2, cos_t, slo, shi)
        qkv_ref[:, ATTN_WIDTH:ATTN_WIDTH + KV_WIDTH] = k_slab.astype(BF16)
        qkv_ref[:, ATTN_WIDTH + KV_WIDTH:] = mm(ATTN_WIDTH + KV_WIDTH, C_LX).astype(BF16)
    else:
        qkv_ref[...] = mm(C_QKV, C_LX).astype(BF16)
    lx_ref[...] = mm(C_LX, C_LG)
    lg_ref[...] = mm(C_LG, C_Z).astype(BF16)
    z_ref[...] = mm(C_Z, C_XBC).astype(BF16)
    xbc_ref[...] = mm(C_XBC, C_GT)
    gt_ref[...] = mm(C_GT, C_DT).astype(BF16)
    dt_ref[...] = mm(C_DT, IN_PACKED)


def _in_proj(x, layer, g_all, mod, w_all, tm, qk_norm=None):
    m = x.shape[0]
    row = lambda w: pl.BlockSpec((tm, w), lambda i: (i, 0))
    in_specs = [row(D_MODEL), _layer_spec((1, D_MODEL), layer, 1), mod.spec(layer, 1, tm), mod.spec(layer, 0, tm),
                _layer_spec((D_MODEL, IN_PACKED), layer, 1)]
    args = [x, g_all, mod.arr, mod.arr, w_all]
    if qk_norm is not None:
        tables, gq_all, gk_all, e2 = qk_norm
        tab = pl.BlockSpec((1, LANES), lambda i: (0, 0))
        in_specs += [tab, tab, tab, _layer_spec((1, LANES), layer, 1), _layer_spec((1, LANES), layer, 1),
                     pl.BlockSpec((LANES, LANES), lambda i: (0, 0))]
        args += [*tables, gq_all, gk_all, e2]
    return pl.pallas_call(
        functools.partial(_in_kernel, qk_norm=qk_norm is not None),
        grid=(m // tm,),
        in_specs=in_specs,
        out_specs=[row(QKV_W), row(LRU_WIDTH), row(LRU_WIDTH), row(SSD_INNER),
                   row(SSD_CONV_DIM), row(GATES_W), row(LANES)],
        out_shape=[jax.ShapeDtypeStruct((m, QKV_W), BF16),
                   jax.ShapeDtypeStruct((m, LRU_WIDTH), F32),
                   jax.ShapeDtypeStruct((m, LRU_WIDTH), BF16),
                   jax.ShapeDtypeStruct((m, SSD_INNER), BF16),
                   jax.ShapeDtypeStruct((m, SSD_CONV_DIM), F32),
                   jax.ShapeDtypeStruct((m, GATES_W), BF16),
                   jax.ShapeDtypeStruct((m, LANES), F32)],
        compiler_params=_params(("arbitrary",)),
        name="in_proj",
    )(*args)


def _head_norm_rope(t, g2, e2, cos_t, sin_lo, sin_hi):
    sq = t * t
    hi = sq.astype(BF16)
    lo = (sq - hi.astype(F32)).astype(BF16)
    ms = jnp.dot(hi, e2, preferred_element_type=F32) + jnp.dot(lo, e2, preferred_element_type=F32)
    tn = t * lax.rsqrt(ms + EPS) * g2
    up = pltpu.roll(tn, LANES - ROT_DIM // 2, 1)
    dn = pltpu.roll(tn, ROT_DIM // 2, 1)
    return tn * cos_t + up * sin_lo + dn * sin_hi


def _rope_tables(pos):
    half = ROT_DIM // 2
    inv_freq = ROPE_THETA ** (-jnp.arange(0, ROT_DIM, 2, dtype=F32) / ROT_DIM)
    ang = pos.astype(F32)[:, None] * inv_freq[None, :]
    cos, sin = jnp.cos(ang), jnp.sin(ang)
    n = pos.shape[0]
    ones = jnp.ones((n, HEAD_DIM - ROT_DIM), F32)
    zeros = jnp.zeros((n, HEAD_DIM - ROT_DIM), F32)
    zh = jnp.zeros((n, half), F32)
    cos_t = jnp.concatenate([cos, cos, ones], axis=1)
    sin_lo = jnp.concatenate([-sin, zh, zeros], axis=1)
    sin_hi = jnp.concatenate([zh, sin, zeros], axis=1)
    rep = lambda t: jnp.concatenate([t, t], axis=1)
    return rep(cos_t), rep(sin_lo), rep(sin_hi)


def _head_avg_matrix():
    r = np.arange(LANES)
    return jnp.asarray((r[:, None] // HEAD_DIM == r[None, :] // HEAD_DIM).astype(np.float32) / HEAD_DIM, dtype=BF16)


def _place_kv(t, fill_left, fill_right):
    lane = lax.broadcasted_iota(I32, t.shape, 1)
    left = lane < HEAD_DIM
    sw = pltpu.roll(t, HEAD_DIM, 1)
    return [jnp.where(left, t, fill_left).astype(BF16), jnp.where(left, fill_right, sw).astype(BF16),
            jnp.where(left, sw, fill_left).astype(BF16), jnp.where(left, fill_right, t).astype(BF16)]


def _attn_kernel(sink_ref, qkv_ref, bias0_ref, bias1_ref, cos_ref, slo_ref, shi_ref, gq_ref, gk_ref, e2_ref,
                 o_ref, kwin_ref, vwin_ref, kvar, vvar, *, layer):
    @pl.when(pl.program_id(1) == 0)
    def _():
        kvar[...] = jnp.zeros_like(kvar)
        vvar[...] = jnp.zeros_like(vvar)

    k_f, v_f = _attn_rows(sink_ref, qkv_ref, 0, bias0_ref[...], bias1_ref[...], cos_ref[...], slo_ref[...],
                          shi_ref[...], gq_ref[...], gk_ref[...], e2_ref[...], o_ref, kvar, vvar, layer)
    kwin_ref[...] = k_f[ATTN_BLOCK:]
    vwin_ref[...] = v_f[ATTN_BLOCK:]


def _attn_rows(sink_ref, qkv_ref, r0, bias0, bias1, cos_t, slo, shi, gq, gk, e2, o_ref, kvar, vvar, layer):
    blk = ATTN_BLOCK
    tile = slice(r0, r0 + 2 * blk)
    lane = lax.broadcasted_iota(I32, (blk, LANES), 1)
    zero = jnp.zeros((blk, LANES), F32)
    one_mid = jnp.where(lane == HEAD_DIM, 1.0, 0.0)
    one_first = jnp.where(lane == 0, 1.0, 0.0)
    k_f = _head_norm_rope(qkv_ref[tile, ATTN_WIDTH:ATTN_WIDTH + KV_WIDTH].astype(F32), gk, e2, cos_t, slo, shi)
    v_f = qkv_ref[tile, ATTN_WIDTH + KV_WIDTH:].astype(F32)
    q_n = []
    for j in range(ATTN_WIDTH // LANES):
        q_slab = _head_norm_rope(qkv_ref[tile, j * LANES:(j + 1) * LANES].astype(F32), gq, e2, cos_t, slo, shi)
        q_n.append((q_slab * (HEAD_DIM ** -0.5)).astype(BF16))
    k_blocks = [[kvar[i] for i in range(4)]]
    v_blocks = [[vvar[i] for i in range(4)]]
    for qb in range(2):
        k_blocks.append(_place_kv(k_f[qb * blk:(qb + 1) * blk], zero, zero))
        v_blocks.append(_place_kv(v_f[qb * blk:(qb + 1) * blk], one_mid, one_first))

    row2 = lax.broadcasted_iota(I32, (2 * blk, 1), 0)
    left2 = lax.broadcasted_iota(I32, (2 * blk, LANES), 1) < HEAD_DIM
    for qb in range(2):
        bias = bias0 if qb == 0 else bias1
        rows = slice(qb * blk, (qb + 1) * blk)
        orow = slice(r0 + qb * blk, r0 + (qb + 1) * blk)
        for g in range(N_KV_HEADS):
            q2 = jnp.concatenate([q_n[2 * g][rows], q_n[2 * g + 1][rows]], axis=0)
            acc = None
            for side in range(2):
                var = 2 * g + side
                keys = jnp.concatenate([k_blocks[qb][var], k_blocks[qb + 1][var]], axis=0)
                vals = jnp.concatenate([v_blocks[qb][var], v_blocks[qb + 1][var]], axis=0)
                s = lax.dot_general(q2, keys, (((1,), (1,)), ((), ())), preferred_element_type=F32) + bias
                h_top = GQA_GROUP * g + side
                sink = jnp.where(row2 < blk, sink_ref[layer, h_top], sink_ref[layer, h_top + 2])
                m = jnp.maximum(jnp.max(s, axis=-1, keepdims=True), sink)
                p = jnp.exp(s - m).astype(BF16)
                o = jnp.dot(p, vals, preferred_element_type=F32)
                ones_lane = HEAD_DIM if side == 0 else 0
                den = o[:, ones_lane:ones_lane + 1] + jnp.exp(sink - m)
                o = o * (1.0 / den)
                o = jnp.where(left2, o, 0.0) if side == 0 else jnp.where(left2, 0.0, o)
                acc = o if acc is None else acc + o
            o_ref[orow, (2 * g) * LANES:(2 * g + 1) * LANES] = acc[:blk].astype(BF16)
            o_ref[orow, (2 * g + 1) * LANES:(2 * g + 2) * LANES] = acc[blk:].astype(BF16)

    for i in range(4):
        kvar[i] = k_blocks[2][i]
        vvar[i] = v_blocks[2][i]
    return k_f, v_f


def _attn_bias():
    r = np.arange(2 * ATTN_BLOCK)[:, None] % ATTN_BLOCK
    c = np.arange(2 * ATTN_BLOCK)[None, :]
    band = (c >= r) & (c <= r + WINDOW)
    first = band & (c >= ATTN_BLOCK)
    to_bias = lambda ok: np.where(ok, 0.0, -np.inf).astype(np.float32)
    return jnp.asarray(np.stack([to_bias(first), to_bias(band)]))


def _attention_prompt(qkv, layer, sinks_all, bias, tables, gq_all, gk_all, e2, batch, seq_len):
    tq = 2 * ATTN_BLOCK
    nb = seq_len // tq
    win = pl.BlockSpec((None, ATTN_BLOCK, KV_WIDTH), lambda b, n: (b, 0, 0))
    tab = pl.BlockSpec((tq, LANES), lambda b, n: (n, 0))
    return pl.pallas_call(
        functools.partial(_attn_kernel, layer=layer),
        grid=(batch, nb),
        in_specs=[pl.BlockSpec(memory_space=pltpu.SMEM),
                  pl.BlockSpec((tq, QKV_W), lambda b, n: (b * nb + n, 0)),
                  pl.BlockSpec((None, tq, tq), lambda b, n: (jnp.minimum(n, 1), 0, 0)),
                  pl.BlockSpec((None, tq, tq), lambda b, n: (1, 0, 0)),
                  tab, tab, tab, _layer_spec((1, LANES), layer, 2), _layer_spec((1, LANES), layer, 2),
                  pl.BlockSpec((LANES, LANES), lambda b, n: (0, 0))],
        out_specs=[pl.BlockSpec((tq, ATTN_WIDTH), lambda b, n: (b * nb + n, 0)), win, win],
        out_shape=[jax.ShapeDtypeStruct((batch * seq_len, ATTN_WIDTH), BF16),
                   jax.ShapeDtypeStruct((batch, ATTN_BLOCK, KV_WIDTH), F32),
                   jax.ShapeDtypeStruct((batch, ATTN_BLOCK, KV_WIDTH), F32)],
        scratch_shapes=[pltpu.VMEM((4, ATTN_BLOCK, LANES), BF16), pltpu.VMEM((4, ATTN_BLOCK, LANES), BF16)],
        compiler_params=_params(("arbitrary", "arbitrary")),
        name="attn_prompt",
    )(sinks_all, qkv, bias, bias, *tables, gq_all, gk_all, e2)


def _conv_chunk(tail, x, w_ref, b_ref, rows):
    width = x.shape[1]
    x3 = x.reshape(rows // SUBLANES, SUBLANES, width)
    full = jnp.concatenate([tail[...][None], x3], axis=0)
    t = lax.broadcasted_iota(I32, x3.shape, 1)
    y = x3 * w_ref[CONV_K - 1:CONV_K, :] + b_ref[...]
    for j in range(1, CONV_K):
        rot = pltpu.roll(full, j, 1)
        y = y + jnp.where(t >= j, rot[1:], rot[:-1]) * w_ref[CONV_K - 1 - j:CONV_K - j, :]
    return y.reshape(rows, width)


def _lru_gates(xc, wa_ref, ba_ref, wx_ref, bx_ref, negsp_ref):
    xb = xc.astype(BF16)
    ra, ri = [], []
    for s in range(LRU_WIDTH // LANES):
        sl = slice(s * LANES, (s + 1) * LANES)
        ra.append(jnp.dot(xb[:, sl], wa_ref[s], preferred_element_type=F32))
        ri.append(jnp.dot(xb[:, sl], wx_ref[s], preferred_element_type=F32))
    r = _sigmoid(jnp.concatenate(ra, axis=1) + ba_ref[...])
    i = _sigmoid(jnp.concatenate(ri, axis=1) + bx_ref[...])
    log_a = r * negsp_ref[...]
    a = jnp.exp(log_a)
    u = jnp.sqrt(1.0 - a * a) * (i * xc)
    return a, u


def _lru_scan(x, gate, cw_ref, cb_ref, wa_ref, ba_ref, wx_ref, bx_ref, negsp_ref, xbuf, hcar, rows):
    xc = _conv_chunk(xbuf, x, cw_ref, cb_ref, rows)
    a, u = _lru_gates(xc, wa_ref, ba_ref, wx_ref, bx_ref, negsp_ref)

    ng = rows // SUBLANES
    a = a.reshape(ng, SUBLANES, LRU_WIDTH)
    u = u.reshape(ng, SUBLANES, LRU_WIDTH)
    t = lax.broadcasted_iota(I32, a.shape, 1)
    d = 1
    while d < SUBLANES:
        keep = t >= d
        a_s = jnp.where(keep, pltpu.roll(a, d, 1), 1.0)
        u_s = jnp.where(keep, pltpu.roll(u, d, 1), 0.0)
        u = a * u_s + u
        a = a * a_s
        d *= 2
    carry = hcar[...]
    groups = []
    for r in range(ng):
        hg = a[r] * carry + u[r]
        groups.append(hg)
        carry = hg[SUBLANES - 1:SUBLANES, :]
    h = jnp.concatenate(groups, axis=0)
    hcar[...] = carry
    xbuf[...] = x[rows - SUBLANES:rows, :]
    return h * _gelu_tanh(gate), carry


def _lru_weight_stack(lru_w_a, lru_w_x, lru_lambda):
    def pair(w):
        w = w.astype(BF16).reshape(DEPTH, LRU_BLOCKS // 2, 2, LRU_BLOCK_DIM, LRU_BLOCK_DIM)
        z = jnp.zeros((DEPTH, LRU_BLOCKS // 2, LRU_BLOCK_DIM, LRU_BLOCK_DIM), BF16)
        top = jnp.concatenate([w[:, :, 0], z], axis=3)
        bot = jnp.concatenate([z, w[:, :, 1]], axis=3)
        return jnp.concatenate([top, bot], axis=2)
    return pair(lru_w_a), pair(lru_w_x), (-LRU_C * jax.nn.softplus(-lru_lambda.astype(F32)))[:, None, :]


def _lru_param_specs(layer, ngrid):
    ls = lambda shape: _layer_spec(shape, layer, ngrid)
    nb = LRU_WIDTH // LANES
    return [ls((CONV_K, LRU_WIDTH)), ls((1, LRU_WIDTH)), ls((nb, LANES, LANES)), ls((1, LRU_WIDTH)),
            ls((nb, LANES, LANES)), ls((1, LRU_WIDTH)), ls((1, LRU_WIDTH))]


def _ssd_gate_norm(y, z, gn):
    u = y * _silu(z)
    gw = SSD_INNER // SSD_GROUPS
    parts = []
    for g in range(SSD_GROUPS):
        ug = u[:, g * gw:(g + 1) * gw]
        parts.append(ug * lax.rsqrt(jnp.mean(ug * ug, axis=-1, keepdims=True) + EPS))
    return jnp.concatenate(parts, axis=1) * gn


def _ssd_chunk(xbc, dt_raw, z, cw_ref, cb_ref, dtb_ref, aneg_ref, d_ref, gn_ref, rexp_ref, xbuf, state):
    q = SSD_CHUNK
    gw = SSD_INNER // SSD_GROUPS
    xc = _silu(_conv_chunk(xbuf, xbc, cw_ref, cb_ref, q))
    xs = xc[:, :SSD_INNER]
    bm = xc[:, SSD_INNER:SSD_INNER + LANES]
    cm = xc[:, SSD_INNER + LANES:]

    rexp = rexp_ref[...]
    dt = _softplus(dt_raw + dtb_ref[...])
    dta = dt * aneg_ref[...]
    li = lax.broadcasted_iota(I32, (q, q), 0)
    si = lax.broadcasted_iota(I32, (q, q), 1)
    causal = li >= si
    tri = jnp.where(causal, 1.0, 0.0).astype(BF16)
    a_cs = _dot_exact_lhs(tri, dta)
    a_cs_t = a_cs.T
    dt_e = _dot_exact_rhs(dt, rexp)
    a_e = _dot_exact_rhs(a_cs, rexp)
    a_last = a_e[q - 1:q, :]

    xdt = xs * dt_e
    lane = lax.broadcasted_iota(I32, (q, LANES), 1)
    left = lane < SSD_STATE
    zero = jnp.zeros((q, LANES), F32)
    cmb = cm.astype(BF16)
    bmb = bm.astype(BF16)
    cm_g = (jnp.where(left, cm, zero).astype(BF16), jnp.where(left, zero, cm).astype(BF16))
    scores = [lax.dot_general(cm_g[g], bmb, (((1,), (1,)), ((), ())), preferred_element_type=F32)
              for g in range(SSD_GROUPS)]

    y_off = jnp.dot(cmb, state[...].astype(BF16), preferred_element_type=F32) * jnp.exp(a_e)

    ys = []
    for j in range(SSD_INNER // LANES):
        xs_slab = xdt[:, j * LANES:(j + 1) * LANES]
        x_side = (jnp.where(left, xs_slab, zero).astype(BF16), jnp.where(left, zero, xs_slab).astype(BF16))
        acc = y_off[:, j * LANES:(j + 1) * LANES]
        for side in range(2):
            h = 2 * j + side
            g = h // (SSD_HEADS // SSD_GROUPS)
            col = a_e[:, h * SSD_HEAD_DIM:h * SSD_HEAD_DIM + 1]
            row = a_cs_t[h:h + 1, :]
            decay = jnp.exp(jnp.where(causal, col - row, -jnp.inf))
            acc = acc + jnp.dot((scores[g] * decay).astype(BF16), x_side[side], preferred_element_type=F32)
        ys.append(acc + d_ref[:, j * LANES:(j + 1) * LANES] * xs[:, j * LANES:(j + 1) * LANES])
    y = jnp.concatenate(ys, axis=1)
    out = _ssd_gate_norm(y, z, gn_ref[...])

    x_end = (xdt * jnp.exp(a_last - a_e)).astype(BF16)
    upd = jnp.dot(bm.T.astype(BF16), x_end, preferred_element_type=F32)
    rg = lax.broadcasted_iota(I32, (LANES, SSD_INNER), 0) // SSD_STATE
    cg = lax.broadcasted_iota(I32, (LANES, SSD_INNER), 1) // gw
    state[...] = jnp.where(rg == cg, jnp.exp(a_last) * state[...] + upd, 0.0)
    xbuf[...] = xbc[q - SUBLANES:q, :]
    return out


def _in_mix_kernel(*refs, rows, tiles_per_seq):
    x_ref, g_ref, sc_ref, sh_ref, w_ref = refs[:5]
    lru_p = refs[5:12]
    ssd_p = refs[12:19]
    qkv_ref, gt_ref, ol_ref, lh_ref, lbuf_ref, os_ref, st_ref, sbuf_ref = refs[19:27]
    xbuf_l, hcar, xbuf_s, state = refs[27:]

    @pl.when(pl.program_id(0) % tiles_per_seq == 0)
    def _():
        for s in (xbuf_l, hcar, xbuf_s, state):
            s[...] = jnp.zeros_like(s)

    h = (_rms(x_ref[...], g_ref[...]) * (1.0 + sc_ref[...]) + sh_ref[...]).astype(BF16)

    def mm(lo, hi):
        return jnp.dot(h, w_ref[:, lo:hi], preferred_element_type=F32)

    lx = mm(C_LX, C_LG)
    gate = mm(C_LG, C_Z)
    z = mm(C_Z, C_XBC)
    xbc = mm(C_XBC, C_GT)
    dt = mm(C_DT, IN_PACKED)
    qkv_ref[...] = mm(C_QKV, C_LX).astype(BF16)
    gt_ref[...] = mm(C_GT, C_DT).astype(BF16)
    o, carry = _lru_scan(lx, gate, *lru_p, xbuf_l, hcar, rows)
    ol_ref[...] = o.astype(BF16)
    lh_ref[...] = carry
    lbuf_ref[...] = lx[rows - (CONV_K - 1):rows, :]
    for c in range(rows // SSD_CHUNK):
        sl = slice(c * SSD_CHUNK, (c + 1) * SSD_CHUNK)
        out = _ssd_chunk(xbc[sl], dt[sl], z[sl], *ssd_p, xbuf_s, state)
        os_ref[sl, :] = out.astype(BF16)
    st_ref[...] = state[...]
    sbuf_ref[...] = xbc[rows - (CONV_K - 1):rows, :]


def _in_proj_mix(x, layer, g_all, mod, w_all, lru_params, ssd_params, rexp, tm, batch, seq_len):
    m = x.shape[0]
    tps = seq_len // tm
    row = lambda w: pl.BlockSpec((tm, w), lambda i: (i, 0))
    per_seq = lambda r, w: pl.BlockSpec((None, r, w), lambda i: (i // tps, 0, 0))
    return pl.pallas_call(
        functools.partial(_in_mix_kernel, rows=tm, tiles_per_seq=tps),
        grid=(m // tm,),
        in_specs=[row(D_MODEL), _layer_spec((1, D_MODEL), layer, 1), mod.spec(layer, 1, tm), mod.spec(layer, 0, tm),
                  _layer_spec((D_MODEL, IN_PACKED), layer, 1)] + _lru_param_specs(layer, 1)
                 + _ssd_param_specs(layer, 1) + [pl.BlockSpec((LANES, SSD_INNER), lambda i: (0, 0))],
        out_specs=[row(QKV_W), row(GATES_W), row(LRU_WIDTH), per_seq(1, LRU_WIDTH), per_seq(CONV_K - 1, LRU_WIDTH),
                   row(SSD_INNER), per_seq(LANES, SSD_INNER), per_seq(CONV_K - 1, SSD_CONV_DIM)],
        out_shape=[jax.ShapeDtypeStruct((m, QKV_W), BF16),
                   jax.ShapeDtypeStruct((m, GATES_W), BF16),
                   jax.ShapeDtypeStruct((m, LRU_WIDTH), BF16),
                   jax.ShapeDtypeStruct((batch, 1, LRU_WIDTH), F32),
                   jax.ShapeDtypeStruct((batch, CONV_K - 1, LRU_WIDTH), F32),
                   jax.ShapeDtypeStruct((m, SSD_INNER), BF16),
                   jax.ShapeDtypeStruct((batch, LANES, SSD_INNER), F32),
                   jax.ShapeDtypeStruct((batch, CONV_K - 1, SSD_CONV_DIM), F32)],
        scratch_shapes=[pltpu.VMEM((SUBLANES, LRU_WIDTH), F32), pltpu.VMEM((1, LRU_WIDTH), F32),
                        pltpu.VMEM((SUBLANES, SSD_CONV_DIM), F32), pltpu.VMEM((LANES, SSD_INNER), F32)],
        compiler_params=_params(("arbitrary",)),
        name="in_proj_mix",
    )(x, g_all, mod.arr, mod.arr, w_all, *lru_params, *ssd_params, rexp)


def _ssd_param_stack(ssd_conv_w, ssd_conv_b, ssd_dt_bias, ssd_a_log, ssd_d, g_ssd_norm):
    pad = lambda v: jnp.pad(v.astype(F32), ((0, 0), (0, LANES - SSD_HEADS)))[:, None, :]
    return (ssd_conv_w.astype(F32), ssd_conv_b.astype(F32)[:, None, :], pad(ssd_dt_bias),
            pad(-jnp.exp(ssd_a_log.astype(F32))),
            jnp.repeat(ssd_d.astype(F32), SSD_HEAD_DIM, axis=1)[:, None, :],
            g_ssd_norm.astype(F32)[:, None, :])


def _ssd_param_specs(layer, ngrid):
    ls = lambda shape: _layer_spec(shape, layer, ngrid)
    return [ls((CONV_K, SSD_CONV_DIM)), ls((1, SSD_CONV_DIM)), ls((1, LANES)), ls((1, LANES)),
            ls((1, SSD_INNER)), ls((1, SSD_INNER))]


def _head_expand_matrix():
    h = np.arange(LANES)[:, None]
    c = np.arange(SSD_INNER)[None, :]
    return jnp.asarray((h == c // SSD_HEAD_DIM).astype(np.float32), dtype=BF16)


def _ssd_state_to_hpn(st):
    lead = st.shape[:-2]
    hpg = SSD_HEADS // SSD_GROUPS
    gw = SSD_INNER // SSD_GROUPS
    parts = []
    for g in range(SSD_GROUPS):
        blk = st[..., g * SSD_STATE:(g + 1) * SSD_STATE, g * gw:(g + 1) * gw]
        blk = blk.reshape(lead + (SSD_STATE, hpg, SSD_HEAD_DIM))
        parts.append(jnp.moveaxis(blk, -3, -1))
    return jnp.concatenate(parts, axis=-3)


def _merge_kernel(*refs, with_router):
    (oa_ref, ol_ref, os_ref, gt_ref, x_ref, g1_ref, sc2_ref, sh2_ref, gn_ref,
     wa_ref, wl_ref, ws_ref, wo_ref) = refs[:13]
    if with_router:
        wr_ref, br_ref, x1_ref, h2_ref, lg_ref = refs[13:]
    else:
        x1_ref, h2_ref = refs[13:]
    d = D_MODEL
    mm = lambda a, w: jnp.dot(a[...], w[...], preferred_element_type=F32)
    gate = lambda lo: _sigmoid(gt_ref[:, lo:lo + d].astype(F32))
    merged = (gate(0) * mm(oa_ref, wa_ref) + gate(d) * mm(ol_ref, wl_ref) + gate(2 * d) * mm(os_ref, ws_ref))
    mix = jnp.dot(merged.astype(BF16), wo_ref[...], preferred_element_type=F32)
    x1 = x_ref[...] + g1_ref[...] * mix
    x1_ref[...] = x1
    h2 = _rms(x1, gn_ref[...]) * (1.0 + sc2_ref[...]) + sh2_ref[...]
    h2_ref[...] = h2.astype(h2_ref.dtype)
    if with_router:
        hi = h2.astype(BF16)
        lo = (h2 - hi.astype(F32)).astype(BF16)
        wr = wr_ref[...]
        t = jnp.dot(hi, wr, preferred_element_type=F32) + jnp.dot(lo, wr, preferred_element_type=F32)
        lg_ref[...] = t + pltpu.roll(t, LANES - N_EXPERTS, 1) + br_ref[...]


def _router_stack(moe_w_router, moe_b_router):
    w = moe_w_router.astype(F32)
    hi = w.astype(BF16)
    lo = (w - hi.astype(F32)).astype(BF16)
    n = w.shape[0]
    wr = jnp.concatenate([hi, lo, jnp.zeros((n, D_MODEL, LANES - 2 * N_EXPERTS), BF16)], axis=2)
    br = jnp.pad(moe_b_router.astype(F32), ((0, 0), (0, LANES - N_EXPERTS)))[:, None, :]
    return wr, br


def _merge(oa, ol, os_, gt, x, layer, mod, gn_all, wa_all, wl_all, ws_all, wo_all, router, tm):
    m = x.shape[0]
    row = lambda w: pl.BlockSpec((tm, w), lambda i: (i, 0))
    ls = lambda shape: _layer_spec(shape, layer, 1)
    in_specs = [row(ATTN_WIDTH), row(LRU_WIDTH), row(SSD_INNER), row(GATES_W), row(D_MODEL),
                mod.spec(layer, 2, tm), mod.spec(layer, 4, tm), mod.spec(layer, 3, tm),
                ls((1, D_MODEL)), ls((ATTN_WIDTH, D_MODEL)), ls((LRU_WIDTH, D_MODEL)),
                ls((SSD_INNER, D_MODEL)), ls((D_MODEL, D_MODEL))]
    args = [oa, ol, os_, gt, x, mod.arr, mod.arr, mod.arr, gn_all, wa_all, wl_all, ws_all, wo_all]
    out_specs = [row(D_MODEL), row(D_MODEL)]
    out_shape = [jax.ShapeDtypeStruct((m, D_MODEL), F32),
                 jax.ShapeDtypeStruct((m, D_MODEL), F32 if router is not None else BF16)]
    if router is not None:
        wr_all, br_all, j = router
        in_specs += [_layer_spec((D_MODEL, LANES), j, 1), _layer_spec((1, LANES), j, 1)]
        args += [wr_all, br_all]
        out_specs.append(row(LANES))
        out_shape.append(jax.ShapeDtypeStruct((m, LANES), F32))
    return pl.pallas_call(
        functools.partial(_merge_kernel, with_router=router is not None),
        grid=(m // tm,),
        in_specs=in_specs, out_specs=out_specs, out_shape=out_shape,
        compiler_params=_params(("arbitrary",)),
        name="merge",
    )(*args)


def _ffn_kernel(te_ref, nv_ref, *refs, fused_residual):
    if fused_residual:
        x_ref, wg_ref, wu_ref, wd_ref, res_ref, g2_ref, o_ref = refs
    else:
        x_ref, wg_ref, wu_ref, wd_ref, o_ref = refs
    i = pl.program_id(0)
    j = pl.program_id(1)

    @pl.when(j == 0)
    def _():
        o_ref[...] = jnp.zeros_like(o_ref)

    @pl.when(i < nv_ref[0])
    def _():
        h = x_ref[...].astype(BF16)
        a = jnp.dot(h, wg_ref[...], preferred_element_type=F32)
        b = jnp.dot(h, wu_ref[...], preferred_element_type=F32)
        t = (_silu(a) * b).astype(BF16)
        o_ref[...] += jnp.dot(t, wd_ref[...], preferred_element_type=F32)

    if fused_residual:
        @pl.when(j == pl.num_programs(1) - 1)
        def _():
            o_ref[...] = res_ref[...] + g2_ref[...] * o_ref[...]


def _ffn(x, wg, wu, wd, tile_expert, n_valid, tm, expert_base=0, residual=None, mod=None, layer=None):
    m = x.shape[0]
    nj = D_FF // FF_CHUNK
    last = nj - 1

    def jj(i, j, nv):
        return jnp.where(i < nv[0], j, last)

    in_specs = [pl.BlockSpec((tm, D_MODEL), lambda i, j, te, nv: (jnp.minimum(i, nv[0] - 1), 0)),
                pl.BlockSpec((None, D_MODEL, FF_CHUNK), lambda i, j, te, nv: (expert_base + te[i], 0, jj(i, j, nv))),
                pl.BlockSpec((None, D_MODEL, FF_CHUNK), lambda i, j, te, nv: (expert_base + te[i], 0, jj(i, j, nv))),
                pl.BlockSpec((None, FF_CHUNK, D_MODEL), lambda i, j, te, nv: (expert_base + te[i], jj(i, j, nv), 0))]
    args = [x, wg, wu, wd]
    fused = residual is not None
    if fused:
        in_specs += [pl.BlockSpec((tm, D_MODEL), lambda i, j, te, nv: (i, 0)), mod.spec(layer, 5, tm)]
        args += [residual, mod.arr]
    return pl.pallas_call(
        functools.partial(_ffn_kernel, fused_residual=fused),
        grid_spec=pltpu.PrefetchScalarGridSpec(
            num_scalar_prefetch=2, grid=(m // tm, nj), in_specs=in_specs,
            out_specs=pl.BlockSpec((tm, D_MODEL), lambda i, j, te, nv: (i, 0))),
        out_shape=jax.ShapeDtypeStruct((m, D_MODEL), F32),
        compiler_params=_params(("arbitrary", "arbitrary")),
        name="ffn",
    )(tile_expert, n_valid, *args)


def _route_kernel(lg_ref, p1_ref, p2_ref, w1_ref, w2_ref, te_ref, nv_ref, carry, e1s, e2s, r1s, r2s,
                  *, tb, tm, nblk, n_real):
    step = pl.program_id(0)

    @pl.when(step == 0)
    def _():
        carry[...] = jnp.zeros_like(carry)

    l8 = lg_ref[...].T[0:N_EXPERTS, :]
    e = lax.broadcasted_iota(I32, l8.shape, 0)
    m1 = jnp.max(l8, axis=0, keepdims=True)
    i1 = jnp.min(jnp.where(l8 == m1, e, N_EXPERTS), axis=0, keepdims=True)
    rest = jnp.where(e == i1, -jnp.inf, l8)
    m2 = jnp.max(rest, axis=0, keepdims=True)
    i2 = jnp.min(jnp.where(rest == m2, e, N_EXPERTS), axis=0, keepdims=True)
    t = jnp.exp(m2 - m1)
    w1 = 1.0 / (1.0 + t)
    real = step * tb + lax.broadcasted_iota(I32, l8.shape, 1) < n_real
    sel1 = jnp.where(real & (e == i1), 1.0, 0.0)
    sel2 = jnp.where(real & (e == i2), 1.0, 0.0)
    sel = sel1 + sel2
    a = lax.broadcasted_iota(I32, (tb, tb), 0)
    b = lax.broadcasted_iota(I32, (tb, tb), 1)
    upper = jnp.where(a <= b, 1.0, 0.0).astype(BF16)
    incl = jnp.dot(sel.astype(BF16), upper, preferred_element_type=F32)
    rank = carry[:, 0:1] + incl - sel
    e1s[pl.ds(step, 1), :] = i1
    e2s[pl.ds(step, 1), :] = i2
    r1s[pl.ds(step, 1), :] = jnp.sum(sel1 * rank, axis=0, keepdims=True).astype(I32)
    r2s[pl.ds(step, 1), :] = jnp.sum(sel2 * rank, axis=0, keepdims=True).astype(I32)
    w1_ref[...] = w1
    w2_ref[...] = t * w1
    carry[...] = carry[...] + incl[:, tb - 1:tb]

    @pl.when(step == nblk - 1)
    def _():
        counts = carry[...].astype(I32)
        shift = tm.bit_length() - 1
        padded = lax.shift_right_logical(counts + (tm - 1), shift) << shift
        ex = lax.broadcasted_iota(I32, padded.shape, 0)
        ends = padded
        d = 1
        while d < N_EXPERTS:
            ends = ends + jnp.where(ex >= d, pltpu.roll(ends, d, 0), 0)
            d *= 2
        starts = ends - padded
        total = ends[N_EXPERTS - 1:N_EXPERTS, :]
        nvalid = lax.shift_right_logical(total, shift)
        tile0 = lax.broadcasted_iota(I32, padded.shape, 1) * tm
        te = jnp.sum(jnp.where(ends <= tile0, 1, 0), axis=0, keepdims=True)
        te_last = jnp.sum(jnp.where(ends <= total - tm, 1, 0), axis=0, keepdims=True)
        te_ref[...] = jnp.where(tile0[0:1, :] < total, te, te_last)
        nv_ref[...] = nvalid
        reps = tb // LANES
        e1, e2 = e1s[...], e2s[...]
        off1 = jnp.zeros(e1.shape, I32)
        off2 = jnp.zeros(e2.shape, I32)
        for k in range(N_EXPERTS):
            row = starts[k:k + 1, :]
            row = jnp.concatenate([row] * reps, axis=1) if reps > 1 else row
            off1 = jnp.where(e1 == k, row, off1)
            off2 = jnp.where(e2 == k, row, off2)
        p1_ref[...] = off1 + r1s[...]
        p2_ref[...] = off2 + r2s[...]


def _route(logits, n_real, tb, tm):
    m = logits.shape[0]
    nblk = m // tb
    assert tb % LANES == 0 and -(-(n_real * TOP_K) // tm) + N_EXPERTS <= LANES and tm & (tm - 1) == 0
    vec = pl.BlockSpec((1, tb), lambda i: (0, i))
    res = pl.BlockSpec((nblk, tb), lambda i: (0, 0))
    one = pl.BlockSpec((1, LANES), lambda i: (0, 0))
    return pl.pallas_call(
        functools.partial(_route_kernel, tb=tb, tm=tm, nblk=nblk, n_real=n_real),
        grid=(nblk,),
        in_specs=[pl.BlockSpec((tb, LANES), lambda i: (i, 0))],
        out_specs=[res, res, vec, vec, one, one],
        out_shape=[jax.ShapeDtypeStruct((nblk, tb), I32), jax.ShapeDtypeStruct((nblk, tb), I32),
                   jax.ShapeDtypeStruct((1, m), F32), jax.ShapeDtypeStruct((1, m), F32),
                   jax.ShapeDtypeStruct((1, LANES), I32), jax.ShapeDtypeStruct((1, LANES), I32)],
        scratch_shapes=[pltpu.VMEM((N_EXPERTS, LANES), F32)] + [pltpu.VMEM((nblk, tb), I32)] * 4,
        compiler_params=_params(("arbitrary",)),
        name="route",
    )(logits)


def _dispatch_kernel(p1_ref, p2_ref, x_ref, dst_in_ref, dst_ref, sem, *, tb):
    del dst_in_ref

    def start(r, c):
        src = x_ref.at[pl.ds(r, 1), :]
        pltpu.make_async_copy(src, dst_ref.at[pl.ds(p1_ref[0, r], 1), :], sem.at[0]).start()
        pltpu.make_async_copy(src, dst_ref.at[pl.ds(p2_ref[0, r], 1), :], sem.at[1]).start()
        return c

    lax.fori_loop(0, tb, start, 0, unroll=8)
    pltpu.make_async_copy(x_ref, dst_ref.at[pl.ds(0, tb), :], sem.at[0]).wait()
    pltpu.make_async_copy(x_ref, dst_ref.at[pl.ds(0, tb), :], sem.at[1]).wait()


def _dispatch(x, pos1, pos2, dst, tb):
    m = x.shape[0]
    smem = pl.BlockSpec((None, 1, tb), lambda i: (i, 0, 0), memory_space=pltpu.SMEM)
    return pl.pallas_call(
        functools.partial(_dispatch_kernel, tb=tb),
        grid=(m // tb,),
        in_specs=[smem, smem, pl.BlockSpec((tb, D_MODEL), lambda i: (i, 0)),
                  pl.BlockSpec(memory_space=pl.ANY)],
        out_specs=pl.BlockSpec(memory_space=pl.ANY),
        out_shape=jax.ShapeDtypeStruct(dst.shape, F32),
        scratch_shapes=[pltpu.SemaphoreType.DMA((2,))],
        input_output_aliases={3: 0},
        compiler_params=_params(("arbitrary",)),
        name="dispatch",
    )(pos1.reshape(m // tb, 1, tb), pos2.reshape(m // tb, 1, tb), x, dst)


def _combine_kernel(p1_ref, p2_ref, y_ref, x1_ref, g2_ref, w1_ref, w2_ref, o_ref, buf1, buf2, sem, *, tb):
    def start(r, c):
        pltpu.make_async_copy(y_ref.at[pl.ds(p1_ref[0, r], 1), :], buf1.at[pl.ds(r, 1), :], sem.at[0]).start()
        pltpu.make_async_copy(y_ref.at[pl.ds(p2_ref[0, r], 1), :], buf2.at[pl.ds(r, 1), :], sem.at[1]).start()
        return c

    lax.fori_loop(0, tb, start, 0, unroll=8)
    pltpu.make_async_copy(y_ref.at[pl.ds(0, tb), :], buf1, sem.at[0]).wait()
    pltpu.make_async_copy(y_ref.at[pl.ds(0, tb), :], buf2, sem.at[1]).wait()
    f = w1_ref[...] * buf1[...] + w2_ref[...] * buf2[...]
    o_ref[...] = x1_ref[...] + g2_ref[...] * f


def _combine(y, pos1, pos2, w1, w2, x1, layer, mod, tb):
    m = x1.shape[0]
    smem = pl.BlockSpec((None, 1, tb), lambda i: (i, 0, 0), memory_space=pltpu.SMEM)
    row = lambda w: pl.BlockSpec((tb, w), lambda i: (i, 0))
    return pl.pallas_call(
        functools.partial(_combine_kernel, tb=tb),
        grid=(m // tb,),
        in_specs=[smem, smem, pl.BlockSpec(memory_space=pl.ANY), row(D_MODEL),
                  mod.spec(layer, 5, tb), row(1), row(1)],
        out_specs=row(D_MODEL),
        out_shape=jax.ShapeDtypeStruct((m, D_MODEL), F32),
        scratch_shapes=[pltpu.VMEM((tb, D_MODEL), F32), pltpu.VMEM((tb, D_MODEL), F32),
                        pltpu.SemaphoreType.DMA((2,))],
        compiler_params=_params(("arbitrary",)),
        name="combine",
    )(pos1.reshape(m // tb, 1, tb), pos2.reshape(m // tb, 1, tb), y, x1, mod.arr,
      w1.reshape(m, 1), w2.reshape(m, 1))


def _moe_tiles(n_tokens, tm):
    return -(-(n_tokens * TOP_K) // tm) + N_EXPERTS


def _moe(groups, layer, wg, wu, wd, expert_base, sorted_buf, tm, tb):
    (h2_a, lg_a, x1_a, mod_a), (h2_b, lg_b, x1_b, mod_b) = groups
    ma, mb = h2_a.shape[0], h2_b.shape[0]
    assert ma % tb == 0 and mb <= tb and mb % LANES == 0
    n_tiles = _moe_tiles(ma + mb, tm)
    logits = jnp.concatenate([lg_a, lg_b, jnp.zeros((tb - mb, LANES), F32)], axis=0)
    pos1, pos2, w1, w2, te, nv = _route(logits, ma + mb, tb, tm)
    na = ma // tb
    hs = _dispatch(h2_a, pos1[:na], pos2[:na], sorted_buf, tb)
    hs = _dispatch(h2_b, pos1[na, :mb], pos2[na, :mb], hs, mb)
    y = _ffn(hs, wg, wu, wd, te[0, :n_tiles], nv[0, :1], tm, expert_base=expert_base)
    out_a = _combine(y, pos1[:na], pos2[:na], w1[0, :ma], w2[0, :ma], x1_a, layer, mod_a, tb)
    out_b = _combine(y, pos1[na, :mb], pos2[na, :mb], w1[0, ma:ma + mb], w2[0, ma:ma + mb], x1_b, layer, mod_b, mb)
    return out_a, out_b, hs


def _attn_decode_kernel(q_ref, ck_ref, cv_ref, kn_ref, vn_ref, sink_ref, o_ref, kw_ref, vw_ref):
    q = q_ref[...].astype(BF16)
    ck, cv = ck_ref[...], cv_ref[...]
    kn, vn = kn_ref[...], vn_ref[...]
    s = jnp.einsum('bhd,bwd->bhw', q, ck.astype(BF16), preferred_element_type=F32)
    knb = kn.astype(BF16).astype(F32)
    s_new = jnp.sum(q.astype(F32) * knb[:, None, :], axis=-1, keepdims=True)
    sink = sink_ref[...]
    m = jnp.maximum(jnp.maximum(jnp.max(s, axis=-1, keepdims=True), s_new), sink)
    p = jnp.exp(s - m)
    pn = jnp.exp(s_new - m)
    inv = 1.0 / (jnp.sum(p, axis=-1, keepdims=True) + pn + jnp.exp(sink - m))
    o = jnp.einsum('bhw,bwd->bhd', (p * inv).astype(BF16), cv.astype(BF16), preferred_element_type=F32)
    o_ref[...] = o + (pn * inv).astype(BF16).astype(F32) * vn.astype(BF16).astype(F32)[:, None, :]
    w = lax.broadcasted_iota(I32, ck.shape, 1)
    last = w == ck.shape[1] - 1
    kw_ref[...] = jnp.where(last, kn[:, None, :], pltpu.roll(ck, ck.shape[1] - 1, 1))
    vw_ref[...] = jnp.where(last, vn[:, None, :], pltpu.roll(cv, cv.shape[1] - 1, 1))


def _attention_decode(qkv, layer, ck_all, cv_all, sinks_col, tb):
    n = qkv.shape[0]
    win = ck_all.shape[2]
    qh = qkv[:, :ATTN_WIDTH].reshape(n, N_KV_HEADS, GQA_GROUP, HEAD_DIM)
    z = jnp.zeros_like(qh[:, 0])
    q_pl = jnp.concatenate([jnp.concatenate([qh[:, 0], z], axis=-1),
                            jnp.concatenate([z, qh[:, 1]], axis=-1)], axis=1)
    kn = qkv[:, ATTN_WIDTH:ATTN_WIDTH + KV_WIDTH].astype(F32)
    vn = qkv[:, ATTN_WIDTH + KV_WIDTH:].astype(F32)
    blk3 = lambda a, b: pl.BlockSpec((tb, a, b), lambda i: (i, 0, 0))
    cache = pl.BlockSpec((None, tb, win, KV_WIDTH), lambda i: (layer, i, 0, 0))
    blk2 = pl.BlockSpec((tb, KV_WIDTH), lambda i: (i, 0))
    o, ck_all, cv_all = pl.pallas_call(
        _attn_decode_kernel,
        grid=(n // tb,),
        in_specs=[blk3(N_Q_HEADS, LANES), cache, cache, blk2, blk2,
                  pl.BlockSpec((None, N_Q_HEADS, 1), lambda i: (layer, 0, 0))],
        out_specs=[blk3(N_Q_HEADS, LANES), cache, cache],
        out_shape=[jax.ShapeDtypeStruct((n, N_Q_HEADS, LANES), F32),
                   jax.ShapeDtypeStruct(ck_all.shape, F32),
                   jax.ShapeDtypeStruct(cv_all.shape, F32)],
        input_output_aliases={1: 1, 2: 2},
        compiler_params=_params(("arbitrary",)),
        name="attn_decode",
    )(q_pl, ck_all, cv_all, kn, vn, sinks_col)
    o = o.reshape(n, N_KV_HEADS, GQA_GROUP, N_KV_HEADS, HEAD_DIM)
    o = jnp.concatenate([o[:, 0, :, 0], o[:, 1, :, 1]], axis=1).reshape(n, ATTN_WIDTH)
    return o.astype(BF16), ck_all, cv_all


def _lru_decode_kernel(x_ref, gate_ref, buf_ref, h0_ref, cw_ref, cb_ref, wa_ref, ba_ref, wx_ref, bx_ref,
                       negsp_ref, o_ref, h_ref, nbuf_ref):
    x = x_ref[...]
    xc = x * cw_ref[CONV_K - 1:CONV_K, :] + cb_ref[...]
    for j in range(CONV_K - 1):
        xc = xc + buf_ref[:, j, :] * cw_ref[j:j + 1, :]
    a, u = _lru_gates(xc, wa_ref, ba_ref, wx_ref, bx_ref, negsp_ref)
    h = a * h0_ref[...] + u
    o_ref[...] = (h * _gelu_tanh(gate_ref[...].astype(F32))).astype(BF16)
    h_ref[...] = h
    for j in range(CONV_K - 2):
        nbuf_ref[:, j, :] = buf_ref[:, j + 1, :]
    nbuf_ref[:, CONV_K - 2, :] = x


def _lru_decode(lx, lgate, layer, buf_all, h0_all, lru_params):
    n = lx.shape[0]
    one = lambda shape: pl.BlockSpec(shape, lambda i: (0,) * len(shape))
    lay = lambda shape: pl.BlockSpec((None,) + shape, lambda i: (layer,) + (0,) * len(shape))
    nb = LRU_WIDTH // LANES
    return pl.pallas_call(
        _lru_decode_kernel,
        in_specs=[one((n, LRU_WIDTH)), one((n, LRU_WIDTH)), lay((n, CONV_K - 1, LRU_WIDTH)), lay((n, LRU_WIDTH)),
                  lay((CONV_K, LRU_WIDTH)), lay((1, LRU_WIDTH)), lay((nb, LANES, LANES)), lay((1, LRU_WIDTH)),
                  lay((nb, LANES, LANES)), lay((1, LRU_WIDTH)), lay((1, LRU_WIDTH))],
        out_shape=[jax.ShapeDtypeStruct((n, LRU_WIDTH), BF16),
                   jax.ShapeDtypeStruct((n, LRU_WIDTH), F32),
                   jax.ShapeDtypeStruct((n, CONV_K - 1, LRU_WIDTH), F32)],
        grid=(1,),
        compiler_params=_params(("arbitrary",)),
        name="lru_decode",
    )(lx, lgate, buf_all, h0_all, *lru_params)


def _ssd_pre_decode_kernel(xbc_ref, dt_ref, buf_ref, cw_ref, cb_ref, dtb_ref, aneg_ref, rexp_ref,
                           xs_ref, xdt_ref, dec_ref, bm_ref, cm_ref, nbuf_ref):
    xbc = xbc_ref[...]
    xc = xbc * cw_ref[CONV_K - 1:CONV_K, :] + cb_ref[...]
    for j in range(CONV_K - 1):
        xc = xc + buf_ref[:, j, :] * cw_ref[j:j + 1, :]
    xc = _silu(xc)
    xs = xc[:, :SSD_INNER]
    rexp = rexp_ref[...]
    dt = _softplus(dt_ref[...] + dtb_ref[...])
    xs_ref[...] = xs
    xdt_ref[...] = xs * _dot_exact_rhs(dt, rexp)
    dec_ref[...] = jnp.exp(_dot_exact_rhs(dt * aneg_ref[...], rexp))
    bm_ref[...] = xc[:, SSD_INNER:SSD_INNER + LANES]
    cm_ref[...] = xc[:, SSD_INNER + LANES:]
    for j in range(CONV_K - 2):
        nbuf_ref[:, j, :] = buf_ref[:, j + 1, :]
    nbuf_ref[:, CONV_K - 2, :] = xbc


def _ssd_state_decode_kernel(h0_ref, x_ref, dec_ref, bm_ref, cm_ref, rep_ref, tile_ref, sum_ref,
                             hn_ref, y_ref, *, tb):
    pn = SSD_HEAD_DIM * SSD_STATE
    x = x_ref[...].reshape(tb * SSD_HEADS, SSD_HEAD_DIM).astype(BF16)
    xrep = jnp.dot(x, rep_ref[...], preferred_element_type=F32).reshape(tb, SSD_HEADS, pn)
    bmb, cmb = bm_ref[...].astype(BF16), cm_ref[...].astype(BF16)
    first = lax.broadcasted_iota(I32, (tb, SSD_HEADS, pn), 1) < SSD_HEADS // SSD_GROUPS

    def tiled(v):
        t0 = jnp.dot(v, tile_ref[0], preferred_element_type=F32)
        t1 = jnp.dot(v, tile_ref[1], preferred_element_type=F32)
        return jnp.where(first, t0[:, None, :], t1[:, None, :])

    h_new = dec_ref[...][:, :, 0:1] * h0_ref[...] + xrep * tiled(bmb)
    hn_ref[...] = h_new
    hc = (h_new * tiled(cmb)).reshape(tb * SSD_HEADS, pn).astype(BF16)
    y_ref[...] = jnp.dot(hc, sum_ref[...], preferred_element_type=F32).reshape(tb, SSD_HEADS, SSD_HEAD_DIM)


def _ssd_post_decode_kernel(y_ref, xs_ref, z_ref, d_ref, gn_ref, o_ref):
    y = y_ref[...] + d_ref[...] * xs_ref[...]
    o_ref[...] = _ssd_gate_norm(y, z_ref[...].astype(F32), gn_ref[...]).astype(BF16)


def _ssd_decode_consts():
    pn = SSD_HEAD_DIM * SSD_STATE
    col_p = np.arange(pn) // SSD_STATE
    col_n = np.arange(pn) % SSD_STATE
    rep = (np.arange(SSD_HEAD_DIM)[:, None] == col_p[None, :]).astype(np.float32)
    row_g = np.arange(LANES) // SSD_STATE
    row_n = np.arange(LANES) % SSD_STATE
    tile = np.stack([((row_g[:, None] == g) & (row_n[:, None] == col_n[None, :])).astype(np.float32)
                     for g in range(SSD_GROUPS)])
    summ = rep.T
    return jnp.asarray(rep, BF16), jnp.asarray(tile, BF16), jnp.asarray(summ, BF16)


def _ssd_decode(xbc, dtr, z, layer, buf_all, h0_all, ssd_params, rexp, dec_consts, tb):
    n = xbc.shape[0]
    pn = SSD_HEAD_DIM * SSD_STATE
    cw_all, cb_all, dtb_all, aneg_all, d_all, gn_all = ssd_params
    f = lambda w: jax.ShapeDtypeStruct((n, w), F32)
    one = lambda shape: pl.BlockSpec(shape, lambda i: (0,) * len(shape))
    lay = lambda shape: pl.BlockSpec((None,) + shape, lambda i: (layer,) + (0,) * len(shape))
    xs, xdt, dec, bm, cm, nbuf = pl.pallas_call(
        _ssd_pre_decode_kernel,
        in_specs=[one((n, SSD_CONV_DIM)), one((n, LANES)), lay((n, CONV_K - 1, SSD_CONV_DIM)),
                  lay((CONV_K, SSD_CONV_DIM)), lay((1, SSD_CONV_DIM)), lay((1, LANES)), lay((1, LANES)),
                  one((LANES, SSD_INNER))],
        out_shape=[f(SSD_INNER), f(SSD_INNER), f(SSD_INNER), f(LANES), f(LANES),
                   jax.ShapeDtypeStruct((n, CONV_K - 1, SSD_CONV_DIM), F32)],
        grid=(1,),
        compiler_params=_params(("arbitrary",)),
        name="ssd_pre_decode",
    )(xbc, dtr, buf_all, cw_all, cb_all, dtb_all, aneg_all, rexp)
    rep, tile, summ = dec_consts
    b3 = lambda a, b: pl.BlockSpec((tb, a, b), lambda i: (i, 0, 0))
    b2 = pl.BlockSpec((tb, LANES), lambda i: (i, 0))
    full = lambda shape: pl.BlockSpec(shape, lambda i: (0,) * len(shape))
    state = pl.BlockSpec((None, tb, SSD_HEADS, pn), lambda i: (layer, i, 0, 0))
    h0_all, y = pl.pallas_call(
        functools.partial(_ssd_state_decode_kernel, tb=tb),
        grid=(n // tb,),
        in_specs=[state, b3(SSD_HEADS, SSD_HEAD_DIM), b3(SSD_HEADS, SSD_HEAD_DIM), b2, b2,
                  full((SSD_HEAD_DIM, pn)), full((SSD_GROUPS, LANES, pn)), full((pn, SSD_HEAD_DIM))],
        out_specs=[state, b3(SSD_HEADS, SSD_HEAD_DIM)],
        out_shape=[jax.ShapeDtypeStruct(h0_all.shape, F32),
                   jax.ShapeDtypeStruct((n, SSD_HEADS, SSD_HEAD_DIM), F32)],
        input_output_aliases={0: 0},
        compiler_params=_params(("arbitrary",)),
        name="ssd_state_decode",
    )(h0_all, xdt.reshape(n, SSD_HEADS, SSD_HEAD_DIM), dec.reshape(n, SSD_HEADS, SSD_HEAD_DIM), bm, cm,
      rep, tile, summ)
    o = pl.pallas_call(
        _ssd_post_decode_kernel,
        in_specs=[one((n, SSD_INNER)), one((n, SSD_INNER)), one((n, SSD_INNER)),
                  lay((1, SSD_INNER)), lay((1, SSD_INNER))],
        out_shape=jax.ShapeDtypeStruct((n, SSD_INNER), BF16),
        grid=(1,),
        compiler_params=_params(("arbitrary",)),
        name="ssd_post_decode",
    )(y.reshape(n, SSD_INNER), xs, z, d_all, gn_all)
    return o, h0_all, nbuf


def _pack_w_in(w):
    dt0 = C_GT
    dt = jnp.pad(w[..., dt0:dt0 + SSD_HEADS].astype(BF16), ((0, 0), (0, 0), (0, LANES - SSD_HEADS)))
    return jnp.concatenate([w[..., :dt0].astype(BF16), w[..., dt0 + SSD_HEADS:].astype(BF16), dt], axis=-1)


def kernel(x_prompt, x_sample, c_prompt, c_sample, cache_k, cache_v, state_lru_h, state_lru_conv, state_ssd_h, state_ssd_conv, w_mod, b_mod, g_norm_mix, g_norm_ffn, w_in, g_q, g_k, attn_sinks, lru_conv_w, lru_conv_b, lru_w_a, lru_b_a, lru_w_x, lru_b_x, lru_lambda, ssd_conv_w, ssd_conv_b, ssd_dt_bias, ssd_a_log, ssd_d, g_ssd_norm, w_br_attn, w_br_lru, w_br_ssd, w_out, ffn_w_gate, ffn_w_up, ffn_w_down, moe_w_router, moe_b_router, moe_w_gate, moe_w_up, moe_w_down):
    bp, seq = x_prompt.shape[:2]
    ns = x_sample.shape[0]
    win = cache_k.shape[2]
    assert x_sample.shape[1] == 1
    assert win == WINDOW and PAST_LEN >= WINDOW
    mp = bp * seq
    tm_p, tm_s = TOKEN_TILE, ns
    pn = SSD_HEAD_DIM * SSD_STATE

    mod = _modulation(jnp.concatenate([c_sample, c_prompt], axis=0), w_mod, b_mod)
    mod_s = _Mod(mod, True, ns, 1)
    mod_p = _Mod(mod[:, ns:].reshape(DEPTH * bp * 6, 1, D_MODEL), False, bp, seq)
    row3 = lambda v: v.astype(F32)[:, None, :]
    w_packed = _pack_w_in(w_in)
    g_mix, g_ffn = row3(g_norm_mix), row3(g_norm_ffn)
    gq2, gk2 = row3(jnp.tile(g_q, (1, 2))), row3(jnp.tile(g_k, (1, 2)))
    sinks = attn_sinks.astype(F32)
    sinks_col = sinks[:, :, None]
    e2 = _head_avg_matrix()
    tab_p = _rope_tables(jnp.arange(seq))
    tab_s = _rope_tables(jnp.full((1,), PAST_LEN))
    wa, wx, negsp = _lru_weight_stack(lru_w_a, lru_w_x, lru_lambda)
    lru_params = (lru_conv_w.astype(F32), row3(lru_conv_b), wa, row3(lru_b_a), wx, row3(lru_b_x), negsp)
    ssd_params = _ssd_param_stack(ssd_conv_w, ssd_conv_b, ssd_dt_bias, ssd_a_log, ssd_d, g_ssd_norm)
    rexp = _head_expand_matrix()
    dec_consts = _ssd_decode_consts()
    br = (w_br_attn.astype(BF16), w_br_lru.astype(BF16), w_br_ssd.astype(BF16), w_out.astype(BF16))
    wr_all, br_all = _router_stack(moe_w_router, moe_b_router)
    n_moe = moe_w_gate.shape[0]
    flat = lambda w: w.astype(BF16).reshape((n_moe * N_EXPERTS,) + w.shape[2:])
    moe_w = (flat(moe_w_gate), flat(moe_w_up), flat(moe_w_down))
    ffn_w = (ffn_w_gate.astype(BF16), ffn_w_up.astype(BF16), ffn_w_down.astype(BF16))
    ck_all = cache_k.reshape(DEPTH, ns, win, KV_WIDTH)
    cv_all = cache_v.reshape(DEPTH, ns, win, KV_WIDTH)
    ssd_h0_all = state_ssd_h.reshape(DEPTH, ns, SSD_HEADS, pn)
    sorted_buf = jnp.zeros((_moe_tiles(mp + ns, tm_p) * tm_p, D_MODEL), F32)

    xp = x_prompt.reshape(mp, D_MODEL)
    xs = x_sample.reshape(ns, D_MODEL)
    outs_p = [[] for _ in range(6)]
    outs_s = [[] for _ in range(3)]
    attn_bias = _attn_bias()

    for l in range(DEPTH):
        is_moe = l % 2 == 1
        j = l // 2
        router = (wr_all, br_all, j) if is_moe else None

        qkv, gt, o_lru, lru_h, lru_buf, o_ssd, ssd_st, ssd_buf = _in_proj_mix(
            xp, l, g_mix, mod_p, w_packed, lru_params, ssd_params, rexp, tm_p, bp, seq)
        o_attn, k_win, v_win = _attention_prompt(qkv, l, sinks, attn_bias, tab_p, gq2, gk2, e2, bp, seq)
        merged_p = _merge(o_attn, o_lru, o_ssd, gt, xp, l, mod_p, g_ffn, *br, router, tm_p)
        for acc, t in zip(outs_p, (k_win, v_win, lru_h, lru_buf, ssd_st, ssd_buf)):
            acc.append(t)

        qkv, lx, lg, z, xbc, gt, dtr = _in_proj(xs, l, g_mix, mod_s, w_packed, tm_s, (tab_s, gq2, gk2, e2))
        o_attn, ck_all, cv_all = _attention_decode(qkv, l, ck_all, cv_all, sinks_col, DECODE_ATTN_BLOCK)
        o_lru, lru_h, lru_buf = _lru_decode(lx, lg, l, state_lru_conv, state_lru_h, lru_params)
        o_ssd, ssd_h0_all, ssd_buf = _ssd_decode(xbc, dtr, z, l, state_ssd_conv, ssd_h0_all, ssd_params, rexp,
                                                 dec_consts, DECODE_STATE_BLOCK)
        merged_s = _merge(o_attn, o_lru, o_ssd, gt, xs, l, mod_s, g_ffn, *br, router, tm_s)
        for acc, t in zip(outs_s, (lru_h, lru_buf, ssd_buf)):
            acc.append(t)

        if is_moe:
            (x1_p, h2_p, lg_p), (x1_s, h2_s, lg_s) = merged_p, merged_s
            xp, xs, sorted_buf = _moe(((h2_p, lg_p, x1_p, mod_p), (h2_s, lg_s, x1_s, mod_s)), l, *moe_w,
                                      j * N_EXPERTS, sorted_buf, tm_p, ROUTE_BLOCK)
        else:
            (x1_p, h2_p), (x1_s, h2_s) = merged_p, merged_s
            nt = mp // tm_p
            xp = _ffn(h2_p, *ffn_w, jnp.zeros((nt,), I32), jnp.full((1,), nt, I32), tm_p,
                      expert_base=j, residual=x1_p, mod=mod_p, layer=l)
            xs = _ffn(h2_s, *ffn_w, jnp.zeros((1,), I32), jnp.ones((1,), I32), tm_s,
                      expert_base=j, residual=x1_s, mod=mod_s, layer=l)

    st = [jnp.stack(a) for a in outs_p]
    ss = [jnp.stack(a) for a in outs_s]
    kv5 = lambda t, b: t.reshape(DEPTH, b, WINDOW, N_KV_HEADS, HEAD_DIM)
    return (xp.reshape(bp, seq, D_MODEL), xs.reshape(ns, 1, D_MODEL),
            kv5(st[0], bp), kv5(st[1], bp), st[2].reshape(DEPTH, bp, LRU_WIDTH), st[3],
            _ssd_state_to_hpn(st[4]), st[5],
            kv5(ck_all, ns), kv5(cv_all, ns), ss[0], ss[1],
            ssd_h0_all.reshape(DEPTH, ns, SSD_HEADS, SSD_HEAD_DIM, SSD_STATE), ss[2])
```

```python
import functools
import math

import numpy as np
import jax
import jax.numpy as jnp
from jax import lax
from jax.experimental import pallas as pl
from jax.experimental.pallas import tpu as pltpu

F32 = jnp.float32
BF16 = jnp.bfloat16
I32 = jnp.int32

D_MODEL = 1024
DEPTH = 4
PAST_LEN = 8192
N_Q_HEADS = 8
N_KV_HEADS = 2
HEAD_DIM = 64
GQA_GROUP = N_Q_HEADS // N_KV_HEADS
WINDOW = 128
ATTN_BLOCK = 128
ROT_DIM = HEAD_DIM // 4
ROPE_THETA = 500000.0
ATTN_WIDTH = N_Q_HEADS * HEAD_DIM
KV_WIDTH = N_KV_HEADS * HEAD_DIM
LRU_WIDTH = 512
LRU_BLOCKS = 8
LRU_BLOCK_DIM = LRU_WIDTH // LRU_BLOCKS
LRU_C = 8.0
CONV_K = 4
SSD_HEADS = 8
SSD_HEAD_DIM = 64
SSD_INNER = SSD_HEADS * SSD_HEAD_DIM
SSD_GROUPS = 2
SSD_STATE = 64
SSD_CHUNK = 128
SSD_CONV_DIM = SSD_INNER + 2 * SSD_GROUPS * SSD_STATE
D_FF = 2816
N_EXPERTS = 8
TOP_K = 2
EPS = 1e-6

LANES = 128
SUBLANES = 8
QKV_W = ATTN_WIDTH + 2 * KV_WIDTH
GATES_W = 3 * D_MODEL
C_QKV = 0
C_LX = C_QKV + QKV_W
C_LG = C_LX + LRU_WIDTH
C_Z = C_LG + LRU_WIDTH
C_XBC = C_Z + SSD_INNER
C_GT = C_XBC + SSD_CONV_DIM
C_DT = C_GT + GATES_W
IN_PACKED = C_DT + LANES
FF_CHUNK = D_FF // 2
VMEM_LIMIT = 56 * 1024 * 1024
TOKEN_TILE = 512
ROUTE_BLOCK = 1024
DECODE_ATTN_BLOCK = 32
DECODE_STATE_BLOCK = 16


def _params(sem=None):
    if sem is None:
        return pltpu.CompilerParams(vmem_limit_bytes=VMEM_LIMIT)
    return pltpu.CompilerParams(dimension_semantics=sem, vmem_limit_bytes=VMEM_LIMIT)


def _layer_spec(shape, layer, ngrid):
    zeros = (0,) * len(shape)
    return pl.BlockSpec((None,) + tuple(shape), lambda *g: (layer,) + zeros)


def _sigmoid(x):
    return 0.5 * jnp.tanh(0.5 * x) + 0.5


def _silu(x):
    return x * _sigmoid(x)


def _softplus(x):
    return jnp.maximum(x, 0.0) + jnp.log(1.0 + jnp.exp(-jnp.abs(x)))


def _gelu_tanh(x):
    return 0.5 * x * (1.0 + jnp.tanh(math.sqrt(2.0 / math.pi) * (x + 0.044715 * (x * x * x))))


def _split3(a):
    a1 = a.astype(BF16)
    r1 = a - a1.astype(F32)
    a2 = r1.astype(BF16)
    a3 = (r1 - a2.astype(F32)).astype(BF16)
    return a1, a2, a3


def _dot_exact_rhs(a, m):
    a1, a2, a3 = _split3(a)
    d = lambda t: jnp.dot(t, m, preferred_element_type=F32)
    return d(a1) + d(a2) + d(a3)


def _dot_exact_lhs(m, a):
    a1, a2, a3 = _split3(a)
    d = lambda t: jnp.dot(m, t, preferred_element_type=F32)
    return d(a1) + d(a2) + d(a3)


def _rms(x, g):
    return x * lax.rsqrt(jnp.mean(x * x, axis=-1, keepdims=True) + EPS) * g


def _mod_kernel(c_ref, w_ref, b_ref, o_ref):
    s = _silu(c_ref[...]).astype(BF16)
    o_ref[...] = jnp.dot(s, w_ref[...].astype(BF16), preferred_element_type=F32) + b_ref[...]


def _modulation(c, w_mod, b_mod):
    rows = c.shape[0]
    tn = 1536
    nt = (6 * D_MODEL) // tn
    return pl.pallas_call(
        _mod_kernel,
        grid=(DEPTH, nt),
        in_specs=[
            pl.BlockSpec((rows, D_MODEL), lambda l, j: (0, 0)),
            pl.BlockSpec((None, D_MODEL, tn), lambda l, j: (l, 0, j)),
            pl.BlockSpec((None, 1, tn), lambda l, j: (l, 0, j)),
        ],
        out_specs=pl.BlockSpec((None, rows, tn), lambda l, j: (l, 0, j)),
        out_shape=jax.ShapeDtypeStruct((DEPTH, rows, 6 * D_MODEL), F32),
        compiler_params=_params(("arbitrary", "arbitrary")),
        name="modulation",
    )(c, w_mod, b_mod.reshape(DEPTH, 1, 6 * D_MODEL))


class _Mod:
    def __init__(self, arr, per_row, batch, seq_len):
        self.arr, self.per_row, self.batch, self.seq_len = arr, per_row, batch, seq_len

    def spec(self, layer, k, tm, nprefetch=0):
        if self.per_row:
            return pl.BlockSpec((None, tm, D_MODEL), lambda i, *_: (layer, i, k))
        b, s = self.batch, self.seq_len
        assert s % tm == 0
        return pl.BlockSpec((None, 1, D_MODEL), lambda i, *_: ((layer * b + (i * tm) // s) * 6 + k, 0, 0))


def _in_kernel(*refs, qk_norm):
    x_ref, g_ref, sc_ref, sh_ref, w_ref = refs[:5]
    qkv_ref, lx_ref, lg_ref, z_ref, xbc_ref, gt_ref, dt_ref = refs[-7:]
    h = (_rms(x_ref[...], g_ref[...]) * (1.0 + sc_ref[...]) + sh_ref[...]).astype(BF16)

    def mm(lo, hi):
        return jnp.dot(h, w_ref[:, lo:hi], preferred_element_type=F32)

    if qk_norm:
        cos_ref, slo_ref, shi_ref, gq_ref, gk_ref, e2_ref = refs[5:11]
        cos_t, slo, shi, e2 = cos_ref[...], slo_ref[...], shi_ref[...], e2_ref[...]
        for j in range(ATTN_WIDTH // LANES):
            q_slab = _head_norm_rope(mm(j * LANES, (j + 1) * LANES), gq_ref[...], e2, cos_t, slo, shi)
            qkv_ref[:, j * LANES:(j + 1) * LANES] = (q_slab * (HEAD_DIM ** -0.5)).astype(BF16)
        k_slab = _head_norm_rope(mm(ATTN_WIDTH, ATTN_WIDTH + KV_WIDTH), gk_ref[...], e2, cos_t, slo, shi)
        qkv_ref[:, ATTN_WIDTH:ATTN_WIDTH + KV_WIDTH] = k_slab.astype(BF16)
        qkv_ref[:, ATTN_WIDTH + KV_WIDTH:] = mm(ATTN_WIDTH + KV_WIDTH, C_LX).astype(BF16)
    else:
        qkv_ref[...] = mm(C_QKV, C_LX).astype(BF16)
    lx_ref[...] = mm(C_LX, C_LG)
    lg_ref[...] = mm(C_LG, C_Z).astype(BF16)
    z_ref[...] = mm(C_Z, C_XBC).astype(BF16)
    xbc_ref[...] = mm(C_XBC, C_GT)
    gt_ref[...] = mm(C_GT, C_DT).astype(BF16)
    dt_ref[...] = mm(C_DT, IN_PACKED)


def _in_proj(x, layer, g_all, mod, w_all, tm, qk_norm=None):
    m = x.shape[0]
    row = lambda w: pl.BlockSpec((tm, w), lambda i: (i, 0))
    in_specs = [row(D_MODEL), _layer_spec((1, D_MODEL), layer, 1), mod.spec(layer, 1, tm), mod.spec(layer, 0, tm),
                _layer_spec((D_MODEL, IN_PACKED), layer, 1)]
    args = [x, g_all, mod.arr, mod.arr, w_all]
    if qk_norm is not None:
        tables, gq_all, gk_all, e2 = qk_norm
        tab = pl.BlockSpec((1, LANES), lambda i: (0, 0))
        in_specs += [tab, tab, tab, _layer_spec((1, LANES), layer, 1), _layer_spec((1, LANES), layer, 1),
                     pl.BlockSpec((LANES, LANES), lambda i: (0, 0))]
        args += [*tables, gq_all, gk_all, e2]
    return pl.pallas_call(
        functools.partial(_in_kernel, qk_norm=qk_norm is not None),
        grid=(m // tm,),
        in_specs=in_specs,
        out_specs=[row(QKV_W), row(LRU_WIDTH), row(LRU_WIDTH), row(SSD_INNER),
                   row(SSD_CONV_DIM), row(GATES_W), row(LANES)],
        out_shape=[jax.ShapeDtypeStruct((m, QKV_W), BF16),
                   jax.ShapeDtypeStruct((m, LRU_WIDTH), F32),
                   jax.ShapeDtypeStruct((m, LRU_WIDTH), BF16),
                   jax.ShapeDtypeStruct((m, SSD_INNER), BF16),
                   jax.ShapeDtypeStruct((m, SSD_CONV_DIM), F32),
                   jax.ShapeDtypeStruct((m, GATES_W), BF16),
                   jax.ShapeDtypeStruct((m, LANES), F32)],
        compiler_params=_params(("arbitrary",)),
        name="in_proj",
    )(*args)


def _head_norm_rope(t, g2, e2, cos_t, sin_lo, sin_hi):
    sq = t * t
    hi = sq.astype(BF16)
    lo = (sq - hi.astype(F32)).astype(BF16)
    ms = jnp.dot(hi, e2, preferred_element_type=F32) + jnp.dot(lo, e2, preferred_element_type=F32)
    tn = t * lax.rsqrt(ms + EPS) * g2
    up = pltpu.roll(tn, LANES - ROT_DIM // 2, 1)
    dn = pltpu.roll(tn, ROT_DIM // 2, 1)
    return tn * cos_t + up * sin_lo + dn * sin_hi


def _rope_tables(pos):
    half = ROT_DIM // 2
    inv_freq = ROPE_THETA ** (-jnp.arange(0, ROT_DIM, 2, dtype=F32) / ROT_DIM)
    ang = pos.astype(F32)[:, None] * inv_freq[None, :]
    cos, sin = jnp.cos(ang), jnp.sin(ang)
    n = pos.shape[0]
    ones = jnp.ones((n, HEAD_DIM - ROT_DIM), F32)
    zeros = jnp.zeros((n, HEAD_DIM - ROT_DIM), F32)
    zh = jnp.zeros((n, half), F32)
    cos_t = jnp.concatenate([cos, cos, ones], axis=1)
    sin_lo = jnp.concatenate([-sin, zh, zeros], axis=1)
    sin_hi = jnp.concatenate([zh, sin, zeros], axis=1)
    rep = lambda t: jnp.concatenate([t, t], axis=1)
    return rep(cos_t), rep(sin_lo), rep(sin_hi)


def _head_avg_matrix():
    r = np.arange(LANES)
    return jnp.asarray((r[:, None] // HEAD_DIM == r[None, :] // HEAD_DIM).astype(np.float32) / HEAD_DIM, dtype=BF16)


def _place_kv(t, fill_left, fill_right):
    lane = lax.broadcasted_iota(I32, t.shape, 1)
    left = lane < HEAD_DIM
    sw = pltpu.roll(t, HEAD_DIM, 1)
    return [jnp.where(left, t, fill_left).astype(BF16), jnp.where(left, fill_right, sw).astype(BF16),
            jnp.where(left, sw, fill_left).astype(BF16), jnp.where(left, fill_right, t).astype(BF16)]


def _attn_kernel(sink_ref, qkv_ref, bias0_ref, bias1_ref, cos_ref, slo_ref, shi_ref, gq_ref, gk_ref, e2_ref,
                 o_ref, kwin_ref, vwin_ref, kvar, vvar, *, layer):
    @pl.when(pl.program_id(1) == 0)
    def _():
        kvar[...] = jnp.zeros_like(kvar)
        vvar[...] = jnp.zeros_like(vvar)

    k_f, v_f = _attn_rows(sink_ref, qkv_ref, 0, bias0_ref[...], bias1_ref[...], cos_ref[...], slo_ref[...],
                          shi_ref[...], gq_ref[...], gk_ref[...], e2_ref[...], o_ref, kvar, vvar, layer)
    kwin_ref[...] = k_f[ATTN_BLOCK:]
    vwin_ref[...] = v_f[ATTN_BLOCK:]


def _attn_rows(sink_ref, qkv_ref, r0, bias0, bias1, cos_t, slo, shi, gq, gk, e2, o_ref, kvar, vvar, layer):
    blk = ATTN_BLOCK
    tile = slice(r0, r0 + 2 * blk)
    lane = lax.broadcasted_iota(I32, (blk, LANES), 1)
    zero = jnp.zeros((blk, LANES), F32)
    one_mid = jnp.where(lane == HEAD_DIM, 1.0, 0.0)
    one_first = jnp.where(lane == 0, 1.0, 0.0)
    k_f = _head_norm_rope(qkv_ref[tile, ATTN_WIDTH:ATTN_WIDTH + KV_WIDTH].astype(F32), gk, e2, cos_t, slo, shi)
    v_f = qkv_ref[tile, ATTN_WIDTH + KV_WIDTH:].astype(F32)
    q_n = []
    for j in range(ATTN_WIDTH // LANES):
        q_slab = _head_norm_rope(qkv_ref[tile, j * LANES:(j + 1) * LANES].astype(F32), gq, e2, cos_t, slo, shi)
        q_n.append((q_slab * (HEAD_DIM ** -0.5)).astype(BF16))
    k_blocks = [[kvar[i] for i in range(4)]]
    v_blocks = [[vvar[i] for i in range(4)]]
    for qb in range(2):
        k_blocks.append(_place_kv(k_f[qb * blk:(qb + 1) * blk], zero, zero))
        v_blocks.append(_place_kv(v_f[qb * blk:(qb + 1) * blk], one_mid, one_first))

    row2 = lax.broadcasted_iota(I32, (2 * blk, 1), 0)
    left2 = lax.broadcasted_iota(I32, (2 * blk, LANES), 1) < HEAD_DIM
    for qb in range(2):
        bias = bias0 if qb == 0 else bias1
        rows = slice(qb * blk, (qb + 1) * blk)
        orow = slice(r0 + qb * blk, r0 + (qb + 1) * blk)
        for g in range(N_KV_HEADS):
            q2 = jnp.concatenate([q_n[2 * g][rows], q_n[2 * g + 1][rows]], axis=0)
            acc = None
            for side in range(2):
                var = 2 * g + side
                keys = jnp.concatenate([k_blocks[qb][var], k_blocks[qb + 1][var]], axis=0)
                vals = jnp.concatenate([v_blocks[qb][var], v_blocks[qb + 1][var]], axis=0)
                s = lax.dot_general(q2, keys, (((1,), (1,)), ((), ())), preferred_element_type=F32) + bias
                h_top = GQA_GROUP * g + side
                sink = jnp.where(row2 < blk, sink_ref[layer, h_top], sink_ref[layer, h_top + 2])
                m = jnp.maximum(jnp.max(s, axis=-1, keepdims=True), sink)
                p = jnp.exp(s - m).astype(BF16)
                o = jnp.dot(p, vals, preferred_element_type=F32)
                ones_lane = HEAD_DIM if side == 0 else 0
                den = o[:, ones_lane:ones_lane + 1] + jnp.exp(sink - m)
                o = o * (1.0 / den)
                o = jnp.where(left2, o, 0.0) if side == 0 else jnp.where(left2, 0.0, o)
                acc = o if acc is None else acc + o
            o_ref[orow, (2 * g) * LANES:(2 * g + 1) * LANES] = acc[:blk].astype(BF16)
            o_ref[orow, (2 * g + 1) * LANES:(2 * g + 2) * LANES] = acc[blk:].astype(BF16)

    for i in range(4):
        kvar[i] = k_blocks[2][i]
        vvar[i] = v_blocks[2][i]
    return k_f, v_f


def _attn_bias():
    r = np.arange(2 * ATTN_BLOCK)[:, None] % ATTN_BLOCK
    c = np.arange(2 * ATTN_BLOCK)[None, :]
    band = (c >= r) & (c <= r + WINDOW)
    first = band & (c >= ATTN_BLOCK)
    to_bias = lambda ok: np.where(ok, 0.0, -np.inf).astype(np.float32)
    return jnp.asarray(np.stack([to_bias(first), to_bias(band)]))


def _attention_prompt(qkv, layer, sinks_all, bias, tables, gq_all, gk_all, e2, batch, seq_len):
    tq = 2 * ATTN_BLOCK
    nb = seq_len // tq
    win = pl.BlockSpec((None, ATTN_BLOCK, KV_WIDTH), lambda b, n: (b, 0, 0))
    tab = pl.BlockSpec((tq, LANES), lambda b, n: (n, 0))
    return pl.pallas_call(
        functools.partial(_attn_kernel, layer=layer),
        grid=(batch, nb),
        in_specs=[pl.BlockSpec(memory_space=pltpu.SMEM),
                  pl.BlockSpec((tq, QKV_W), lambda b, n: (b * nb + n, 0)),
                  pl.BlockSpec((None, tq, tq), lambda b, n: (jnp.minimum(n, 1), 0, 0)),
                  pl.BlockSpec((None, tq, tq), lambda b, n: (1, 0, 0)),
                  tab, tab, tab, _layer_spec((1, LANES), layer, 2), _layer_spec((1, LANES), layer, 2),
                  pl.BlockSpec((LANES, LANES), lambda b, n: (0, 0))],
        out_specs=[pl.BlockSpec((tq, ATTN_WIDTH), lambda b, n: (b * nb + n, 0)), win, win],
        out_shape=[jax.ShapeDtypeStruct((batch * seq_len, ATTN_WIDTH), BF16),
                   jax.ShapeDtypeStruct((batch, ATTN_BLOCK, KV_WIDTH), F32),
                   jax.ShapeDtypeStruct((batch, ATTN_BLOCK, KV_WIDTH), F32)],
        scratch_shapes=[pltpu.VMEM((4, ATTN_BLOCK, LANES), BF16), pltpu.VMEM((4, ATTN_BLOCK, LANES), BF16)],
        compiler_params=_params(("arbitrary", "arbitrary")),
        name="attn_prompt",
    )(sinks_all, qkv, bias, bias, *tables, gq_all, gk_all, e2)


def _conv_chunk(tail, x, w_ref, b_ref, rows):
    width = x.shape[1]
    x3 = x.reshape(rows // SUBLANES, SUBLANES, width)
    full = jnp.concatenate([tail[...][None], x3], axis=0)
    t = lax.broadcasted_iota(I32, x3.shape, 1)
    y = x3 * w_ref[CONV_K - 1:CONV_K, :] + b_ref[...]
    for j in range(1, CONV_K):
        rot = pltpu.roll(full, j, 1)
        y = y + jnp.where(t >= j, rot[1:], rot[:-1]) * w_ref[CONV_K - 1 - j:CONV_K - j, :]
    return y.reshape(rows, width)


def _lru_gates(xc, wa_ref, ba_ref, wx_ref, bx_ref, negsp_ref):
    xb = xc.astype(BF16)
    ra, ri = [], []
    for s in range(LRU_WIDTH // LANES):
        sl = slice(s * LANES, (s + 1) * LANES)
        ra.append(jnp.dot(xb[:, sl], wa_ref[s], preferred_element_type=F32))
        ri.append(jnp.dot(xb[:, sl], wx_ref[s], preferred_element_type=F32))
    r = _sigmoid(jnp.concatenate(ra, axis=1) + ba_ref[...])
    i = _sigmoid(jnp.concatenate(ri, axis=1) + bx_ref[...])
    log_a = r * negsp_ref[...]
    a = jnp.exp(log_a)
    u = jnp.sqrt(1.0 - a * a) * (i * xc)
    return a, u


def _lru_scan(x, gate, cw_ref, cb_ref, wa_ref, ba_ref, wx_ref, bx_ref, negsp_ref, xbuf, hcar, rows):
    xc = _conv_chunk(xbuf, x, cw_ref, cb_ref, rows)
    a, u = _lru_gates(xc, wa_ref, ba_ref, wx_ref, bx_ref, negsp_ref)

    ng = rows // SUBLANES
    a = a.reshape(ng, SUBLANES, LRU_WIDTH)
    u = u.reshape(ng, SUBLANES, LRU_WIDTH)
    t = lax.broadcasted_iota(I32, a.shape, 1)
    d = 1
    while d < SUBLANES:
        keep = t >= d
        a_s = jnp.where(keep, pltpu.roll(a, d, 1), 1.0)
        u_s = jnp.where(keep, pltpu.roll(u, d, 1), 0.0)
        u = a * u_s + u
        a = a * a_s
        d *= 2
    carry = hcar[...]
    groups = []
    for r in range(ng):
        hg = a[r] * carry + u[r]
        groups.append(hg)
        carry = hg[SUBLANES - 1:SUBLANES, :]
    h = jnp.concatenate(groups, axis=0)
    hcar[...] = carry
    xbuf[...] = x[rows - SUBLANES:rows, :]
    return h * _gelu_tanh(gate), carry


def _lru_weight_stack(lru_w_a, lru_w_x, lru_lambda):
    def pair(w):
        w = w.astype(BF16).reshape(DEPTH, LRU_BLOCKS // 2, 2, LRU_BLOCK_DIM, LRU_BLOCK_DIM)
        z = jnp.zeros((DEPTH, LRU_BLOCKS // 2, LRU_BLOCK_DIM, LRU_BLOCK_DIM), BF16)
        top = jnp.concatenate([w[:, :, 0], z], axis=3)
        bot = jnp.concatenate([z, w[:, :, 1]], axis=3)
        return jnp.concatenate([top, bot], axis=2)
    return pair(lru_w_a), pair(lru_w_x), (-LRU_C * jax.nn.softplus(-lru_lambda.astype(F32)))[:, None, :]


def _lru_param_specs(layer, ngrid):
    ls = lambda shape: _layer_spec(shape, layer, ngrid)
    nb = LRU_WIDTH // LANES
    return [ls((CONV_K, LRU_WIDTH)), ls((1, LRU_WIDTH)), ls((nb, LANES, LANES)), ls((1, LRU_WIDTH)),
            ls((nb, LANES, LANES)), ls((1, LRU_WIDTH)), ls((1, LRU_WIDTH))]


def _ssd_gate_norm(y, z, gn):
    u = y * _silu(z)
    gw = SSD_INNER // SSD_GROUPS
    parts = []
    for g in range(SSD_GROUPS):
        ug = u[:, g * gw:(g + 1) * gw]
        parts.append(ug * lax.rsqrt(jnp.mean(ug * ug, axis=-1, keepdims=True) + EPS))
    return jnp.concatenate(parts, axis=1) * gn


def _ssd_chunk(xbc, dt_raw, z, cw_ref, cb_ref, dtb_ref, aneg_ref, d_ref, gn_ref, rexp_ref, xbuf, state):
    q = SSD_CHUNK
    gw = SSD_INNER // SSD_GROUPS
    xc = _silu(_conv_chunk(xbuf, xbc, cw_ref, cb_ref, q))
    xs = xc[:, :SSD_INNER]
    bm = xc[:, SSD_INNER:SSD_INNER + LANES]
    cm = xc[:, SSD_INNER + LANES:]

    rexp = rexp_ref[...]
    dt = _softplus(dt_raw + dtb_ref[...])
    dta = dt * aneg_ref[...]
    li = lax.broadcasted_iota(I32, (q, q), 0)
    si = lax.broadcasted_iota(I32, (q, q), 1)
    causal = li >= si
    tri = jnp.where(causal, 1.0, 0.0).astype(BF16)
    a_cs = _dot_exact_lhs(tri, dta)
    a_cs_t = a_cs.T
    dt_e = _dot_exact_rhs(dt, rexp)
    a_e = _dot_exact_rhs(a_cs, rexp)
    a_last = a_e[q - 1:q, :]

    xdt = xs * dt_e
    lane = lax.broadcasted_iota(I32, (q, LANES), 1)
    left = lane < SSD_STATE
    zero = jnp.zeros((q, LANES), F32)
    cmb = cm.astype(BF16)
    bmb = bm.astype(BF16)
    cm_g = (jnp.where(left, cm, zero).astype(BF16), jnp.where(left, zero, cm).astype(BF16))
    scores = [lax.dot_general(cm_g[g], bmb, (((1,), (1,)), ((), ())), preferred_element_type=F32)
              for g in range(SSD_GROUPS)]

    y_off = jnp.dot(cmb, state[...].astype(BF16), preferred_element_type=F32) * jnp.exp(a_e)

    ys = []
    for j in range(SSD_INNER // LANES):
        xs_slab = xdt[:, j * LANES:(j + 1) * LANES]
        x_side = (jnp.where(left, xs_slab, zero).astype(BF16), jnp.where(left, zero, xs_slab).astype(BF16))
        acc = y_off[:, j * LANES:(j + 1) * LANES]
        for side in range(2):
            h = 2 * j + side
            g = h // (SSD_HEADS // SSD_GROUPS)
            col = a_e[:, h * SSD_HEAD_DIM:h * SSD_HEAD_DIM + 1]
            row = a_cs_t[h:h + 1, :]
            decay = jnp.exp(jnp.where(causal, col - row, -jnp.inf))
            acc = acc + jnp.dot((scores[g] * decay).astype(BF16), x_side[side], preferred_element_type=F32)
        ys.append(acc + d_ref[:, j * LANES:(j + 1) * LANES] * xs[:, j * LANES:(j + 1) * LANES])
    y = jnp.concatenate(ys, axis=1)
    out = _ssd_gate_norm(y, z, gn_ref[...])

    x_end = (xdt * jnp.exp(a_last - a_e)).astype(BF16)
    upd = jnp.dot(bm.T.astype(BF16), x_end, preferred_element_type=F32)
    rg = lax.broadcasted_iota(I32, (LANES, SSD_INNER), 0) // SSD_STATE
    cg = lax.broadcasted_iota(I32, (LANES, SSD_INNER), 1) // gw
    state[...] = jnp.where(rg == cg, jnp.exp(a_last) * state[...] + upd, 0.0)
    xbuf[...] = xbc[q - SUBLANES:q, :]
    return out


def _in_mix_kernel(*refs, rows, tiles_per_seq):
    x_ref, g_ref, sc_ref, sh_ref, w_ref = refs[:5]
    lru_p = refs[5:12]
    ssd_p = refs[12:19]
    qkv_ref, gt_ref, ol_ref, lh_ref, lbuf_ref, os_ref, st_ref, sbuf_ref = refs[19:27]
    xbuf_l, hcar, xbuf_s, state = refs[27:]

    @pl.when(pl.program_id(0) % tiles_per_seq == 0)
    def _():
        for s in (xbuf_l, hcar, xbuf_s, state):
            s[...] = jnp.zeros_like(s)

    h = (_rms(x_ref[...], g_ref[...]) * (1.0 + sc_ref[...]) + sh_ref[...]).astype(BF16)

    def mm(lo, hi):
        return jnp.dot(h, w_ref[:, lo:hi], preferred_element_type=F32)

    lx = mm(C_LX, C_LG)
    gate = mm(C_LG, C_Z)
    z = mm(C_Z, C_XBC)
    xbc = mm(C_XBC, C_GT)
    dt = mm(C_DT, IN_PACKED)
    qkv_ref[...] = mm(C_QKV, C_LX).astype(BF16)
    gt_ref[...] = mm(C_GT, C_DT).astype(BF16)
    o, carry = _lru_scan(lx, gate, *lru_p, xbuf_l, hcar, rows)
    ol_ref[...] = o.astype(BF16)
    lh_ref[...] = carry
    lbuf_ref[...] = lx[rows - (CONV_K - 1):rows, :]
    for c in range(rows // SSD_CHUNK):
        sl = slice(c * SSD_CHUNK, (c + 1) * SSD_CHUNK)
        out = _ssd_chunk(xbc[sl], dt[sl], z[sl], *ssd_p, xbuf_s, state)
        os_ref[sl, :] = out.astype(BF16)
    st_ref[...] = state[...]
    sbuf_ref[...] = xbc[rows - (CONV_K - 1):rows, :]


def _in_proj_mix(x, layer, g_all, mod, w_all, lru_params, ssd_params, rexp, tm, batch, seq_len):
    m = x.shape[0]
    tps = seq_len // tm
    row = lambda w: pl.BlockSpec((tm, w), lambda i: (i, 0))
    per_seq = lambda r, w: pl.BlockSpec((None, r, w), lambda i: (i // tps, 0, 0))
    return pl.pallas_call(
        functools.partial(_in_mix_kernel, rows=tm, tiles_per_seq=tps),
        grid=(m // tm,),
        in_specs=[row(D_MODEL), _layer_spec((1, D_MODEL), layer, 1), mod.spec(layer, 1, tm), mod.spec(layer, 0, tm),
                  _layer_spec((D_MODEL, IN_PACKED), layer, 1)] + _lru_param_specs(layer, 1)
                 + _ssd_param_specs(layer, 1) + [pl.BlockSpec((LANES, SSD_INNER), lambda i: (0, 0))],
        out_specs=[row(QKV_W), row(GATES_W), row(LRU_WIDTH), per_seq(1, LRU_WIDTH), per_seq(CONV_K - 1, LRU_WIDTH),
                   row(SSD_INNER), per_seq(LANES, SSD_INNER), per_seq(CONV_K - 1, SSD_CONV_DIM)],
        out_shape=[jax.ShapeDtypeStruct((m, QKV_W), BF16),
                   jax.ShapeDtypeStruct((m, GATES_W), BF16),
                   jax.ShapeDtypeStruct((m, LRU_WIDTH), BF16),
                   jax.ShapeDtypeStruct((batch, 1, LRU_WIDTH), F32),
                   jax.ShapeDtypeStruct((batch, CONV_K - 1, LRU_WIDTH), F32),
                   jax.ShapeDtypeStruct((m, SSD_INNER), BF16),
                   jax.ShapeDtypeStruct((batch, LANES, SSD_INNER), F32),
                   jax.ShapeDtypeStruct((batch, CONV_K - 1, SSD_CONV_DIM), F32)],
        scratch_shapes=[pltpu.VMEM((SUBLANES, LRU_WIDTH), F32), pltpu.VMEM((1, LRU_WIDTH), F32),
                        pltpu.VMEM((SUBLANES, SSD_CONV_DIM), F32), pltpu.VMEM((LANES, SSD_INNER), F32)],
        compiler_params=_params(("arbitrary",)),
        name="in_proj_mix",
    )(x, g_all, mod.arr, mod.arr, w_all, *lru_params, *ssd_params, rexp)


def _ssd_param_stack(ssd_conv_w, ssd_conv_b, ssd_dt_bias, ssd_a_log, ssd_d, g_ssd_norm):
    pad = lambda v: jnp.pad(v.astype(F32), ((0, 0), (0, LANES - SSD_HEADS)))[:, None, :]
    return (ssd_conv_w.astype(F32), ssd_conv_b.astype(F32)[:, None, :], pad(ssd_dt_bias),
            pad(-jnp.exp(ssd_a_log.astype(F32))),
            jnp.repeat(ssd_d.astype(F32), SSD_HEAD_DIM, axis=1)[:, None, :],
            g_ssd_norm.astype(F32)[:, None, :])


def _ssd_param_specs(layer, ngrid):
    ls = lambda shape: _layer_spec(shape, layer, ngrid)
    return [ls((CONV_K, SSD_CONV_DIM)), ls((1, SSD_CONV_DIM)), ls((1, LANES)), ls((1, LANES)),
            ls((1, SSD_INNER)), ls((1, SSD_INNER))]


def _head_expand_matrix():
    h = np.arange(LANES)[:, None]
    c = np.arange(SSD_INNER)[None, :]
    return jnp.asarray((h == c // SSD_HEAD_DIM).astype(np.float32), dtype=BF16)


def _ssd_state_to_hpn(st):
    lead = st.shape[:-2]
    hpg = SSD_HEADS // SSD_GROUPS
    gw = SSD_INNER // SSD_GROUPS
    parts = []
    for g in range(SSD_GROUPS):
        blk = st[..., g * SSD_STATE:(g + 1) * SSD_STATE, g * gw:(g + 1) * gw]
        blk = blk.reshape(lead + (SSD_STATE, hpg, SSD_HEAD_DIM))
        parts.append(jnp.moveaxis(blk, -3, -1))
    return jnp.concatenate(parts, axis=-3)


def _merge_kernel(*refs, with_router):
    (oa_ref, ol_ref, os_ref, gt_ref, x_ref, g1_ref, sc2_ref, sh2_ref, gn_ref,
     wa_ref, wl_ref, ws_ref, wo_ref) = refs[:13]
    if with_router:
        wr_ref, br_ref, x1_ref, h2_ref, lg_ref = refs[13:]
    else:
        x1_ref, h2_ref = refs[13:]
    d = D_MODEL
    mm = lambda a, w: jnp.dot(a[...], w[...], preferred_element_type=F32)
    gate = lambda lo: _sigmoid(gt_ref[:, lo:lo + d].astype(F32))
    merged = (gate(0) * mm(oa_ref, wa_ref) + gate(d) * mm(ol_ref, wl_ref) + gate(2 * d) * mm(os_ref, ws_ref))
    mix = jnp.dot(merged.astype(BF16), wo_ref[...], preferred_element_type=F32)
    x1 = x_ref[...] + g1_ref[...] * mix
    x1_ref[...] = x1
    h2 = _rms(x1, gn_ref[...]) * (1.0 + sc2_ref[...]) + sh2_ref[...]
    h2_ref[...] = h2.astype(h2_ref.dtype)
    if with_router:
        hi = h2.astype(BF16)
        lo = (h2 - hi.astype(F32)).astype(BF16)
        wr = wr_ref[...]
        t = jnp.dot(hi, wr, preferred_element_type=F32) + jnp.dot(lo, wr, preferred_element_type=F32)
        lg_ref[...] = t + pltpu.roll(t, LANES - N_EXPERTS, 1) + br_ref[...]


def _router_stack(moe_w_router, moe_b_router):
    w = moe_w_router.astype(F32)
    hi = w.astype(BF16)
    lo = (w - hi.astype(F32)).astype(BF16)
    n = w.shape[0]
    wr = jnp.concatenate([hi, lo, jnp.zeros((n, D_MODEL, LANES - 2 * N_EXPERTS), BF16)], axis=2)
    br = jnp.pad(moe_b_router.astype(F32), ((0, 0), (0, LANES - N_EXPERTS)))[:, None, :]
    return wr, br


def _merge(oa, ol, os_, gt, x, layer, mod, gn_all, wa_all, wl_all, ws_all, wo_all, router, tm):
    m = x.shape[0]
    row = lambda w: pl.BlockSpec((tm, w), lambda i: (i, 0))
    ls = lambda shape: _layer_spec(shape, layer, 1)
    in_specs = [row(ATTN_WIDTH), row(LRU_WIDTH), row(SSD_INNER), row(GATES_W), row(D_MODEL),
                mod.spec(layer, 2, tm), mod.spec(layer, 4, tm), mod.spec(layer, 3, tm),
                ls((1, D_MODEL)), ls((ATTN_WIDTH, D_MODEL)), ls((LRU_WIDTH, D_MODEL)),
                ls((SSD_INNER, D_MODEL)), ls((D_MODEL, D_MODEL))]
    args = [oa, ol, os_, gt, x, mod.arr, mod.arr, mod.arr, gn_all, wa_all, wl_all, ws_all, wo_all]
    out_specs = [row(D_MODEL), row(D_MODEL)]
    out_shape = [jax.ShapeDtypeStruct((m, D_MODEL), F32),
                 jax.ShapeDtypeStruct((m, D_MODEL), F32 if router is not None else BF16)]
    if router is not None:
        wr_all, br_all, j = router
        in_specs += [_layer_spec((D_MODEL, LANES), j, 1), _layer_spec((1, LANES), j, 1)]
        args += [wr_all, br_all]
        out_specs.append(row(LANES))
        out_shape.append(jax.ShapeDtypeStruct((m, LANES), F32))
    return pl.pallas_call(
        functools.partial(_merge_kernel, with_router=router is not None),
        grid=(m // tm,),
        in_specs=in_specs, out_specs=out_specs, out_shape=out_shape,
        compiler_params=_params(("arbitrary",)),
        name="merge",
    )(*args)


def _ffn_kernel(te_ref, nv_ref, *refs, fused_residual):
    if fused_residual:
        x_ref, wg_ref, wu_ref, wd_ref, res_ref, g2_ref, o_ref = refs
    else:
        x_ref, wg_ref, wu_ref, wd_ref, o_ref = refs
    i = pl.program_id(0)
    j = pl.program_id(1)

    @pl.when(j == 0)
    def _():
        o_ref[...] = jnp.zeros_like(o_ref)

    @pl.when(i < nv_ref[0])
    def _():
        h = x_ref[...].astype(BF16)
        a = jnp.dot(h, wg_ref[...], preferred_element_type=F32)
        b = jnp.dot(h, wu_ref[...], preferred_element_type=F32)
        t = (_silu(a) * b).astype(BF16)
        o_ref[...] += jnp.dot(t, wd_ref[...], preferred_element_type=F32)

    if fused_residual:
        @pl.when(j == pl.num_programs(1) - 1)
        def _():
            o_ref[...] = res_ref[...] + g2_ref[...] * o_ref[...]


def _ffn(x, wg, wu, wd, tile_expert, n_valid, tm, expert_base=0, residual=None, mod=None, layer=None):
    m = x.shape[0]
    nj = D_FF // FF_CHUNK
    last = nj - 1

    def jj(i, j, nv):
        return jnp.where(i < nv[0], j, last)

    in_specs = [pl.BlockSpec((tm, D_MODEL), lambda i, j, te, nv: (jnp.minimum(i, nv[0] - 1), 0)),
                pl.BlockSpec((None, D_MODEL, FF_CHUNK), lambda i, j, te, nv: (expert_base + te[i], 0, jj(i, j, nv))),
                pl.BlockSpec((None, D_MODEL, FF_CHUNK), lambda i, j, te, nv: (expert_base + te[i], 0, jj(i, j, nv))),
                pl.BlockSpec((None, FF_CHUNK, D_MODEL), lambda i, j, te, nv: (expert_base + te[i], jj(i, j, nv), 0))]
    args = [x, wg, wu, wd]
    fused = residual is not None
    if fused:
        in_specs += [pl.BlockSpec((tm, D_MODEL), lambda i, j, te, nv: (i, 0)), mod.spec(layer, 5, tm)]
        args += [residual, mod.arr]
    return pl.pallas_call(
        functools.partial(_ffn_kernel, fused_residual=fused),
        grid_spec=pltpu.PrefetchScalarGridSpec(
            num_scalar_prefetch=2, grid=(m // tm, nj), in_specs=in_specs,
            out_specs=pl.BlockSpec((tm, D_MODEL), lambda i, j, te, nv: (i, 0))),
        out_shape=jax.ShapeDtypeStruct((m, D_MODEL), F32),
        compiler_params=_params(("arbitrary", "arbitrary")),
        name="ffn",
    )(tile_expert, n_valid, *args)


def _route_kernel(lg_ref, p1_ref, p2_ref, w1_ref, w2_ref, te_ref, nv_ref, carry, e1s, e2s, r1s, r2s,
                  *, tb, tm, nblk, n_real):
    step = pl.program_id(0)

    @pl.when(step == 0)
    def _():
        carry[...] = jnp.zeros_like(carry)

    l8 = lg_ref[...].T[0:N_EXPERTS, :]
    e = lax.broadcasted_iota(I32, l8.shape, 0)
    m1 = jnp.max(l8, axis=0, keepdims=True)
    i1 = jnp.min(jnp.where(l8 == m1, e, N_EXPERTS), axis=0, keepdims=True)
    rest = jnp.where(e == i1, -jnp.inf, l8)
    m2 = jnp.max(rest, axis=0, keepdims=True)
    i2 = jnp.min(jnp.where(rest == m2, e, N_EXPERTS), axis=0, keepdims=True)
    t = jnp.exp(m2 - m1)
    w1 = 1.0 / (1.0 + t)
    real = step * tb + lax.broadcasted_iota(I32, l8.shape, 1) < n_real
    sel1 = jnp.where(real & (e == i1), 1.0, 0.0)
    sel2 = jnp.where(real & (e == i2), 1.0, 0.0)
    sel = sel1 + sel2
    a = lax.broadcasted_iota(I32, (tb, tb), 0)
    b = lax.broadcasted_iota(I32, (tb, tb), 1)
    upper = jnp.where(a <= b, 1.0, 0.0).astype(BF16)
    incl = jnp.dot(sel.astype(BF16), upper, preferred_element_type=F32)
    rank = carry[:, 0:1] + incl - sel
    e1s[pl.ds(step, 1), :] = i1
    e2s[pl.ds(step, 1), :] = i2
    r1s[pl.ds(step, 1), :] = jnp.sum(sel1 * rank, axis=0, keepdims=True).astype(I32)
    r2s[pl.ds(step, 1), :] = jnp.sum(sel2 * rank, axis=0, keepdims=True).astype(I32)
    w1_ref[...] = w1
    w2_ref[...] = t * w1
    carry[...] = carry[...] + incl[:, tb - 1:tb]

    @pl.when(step == nblk - 1)
    def _():
        counts = carry[...].astype(I32)
        shift = tm.bit_length() - 1
        padded = lax.shift_right_logical(counts + (tm - 1), shift) << shift
        ex = lax.broadcasted_iota(I32, padded.shape, 0)
        ends = padded
        d = 1
        while d < N_EXPERTS:
            ends = ends + jnp.where(ex >= d, pltpu.roll(ends, d, 0), 0)
            d *= 2
        starts = ends - padded
        total = ends[N_EXPERTS - 1:N_EXPERTS, :]
        nvalid = lax.shift_right_logical(total, shift)
        tile0 = lax.broadcasted_iota(I32, padded.shape, 1) * tm
        te = jnp.sum(jnp.where(ends <= tile0, 1, 0), axis=0, keepdims=True)
        te_last = jnp.sum(jnp.where(ends <= total - tm, 1, 0), axis=0, keepdims=True)
        te_ref[...] = jnp.where(tile0[0:1, :] < total, te, te_last)
        nv_ref[...] = nvalid
        reps = tb // LANES
        e1, e2 = e1s[...], e2s[...]
        off1 = jnp.zeros(e1.shape, I32)
        off2 = jnp.zeros(e2.shape, I32)
        for k in range(N_EXPERTS):
            row = starts[k:k + 1, :]
            row = jnp.concatenate([row] * reps, axis=1) if reps > 1 else row
            off1 = jnp.where(e1 == k, row, off1)
            off2 = jnp.where(e2 == k, row, off2)
        p1_ref[...] = off1 + r1s[...]
        p2_ref[...] = off2 + r2s[...]


def _route(logits, n_real, tb, tm):
    m = logits.shape[0]
    nblk = m // tb
    assert tb % LANES == 0 and -(-(n_real * TOP_K) // tm) + N_EXPERTS <= LANES and tm & (tm - 1) == 0
    vec = pl.BlockSpec((1, tb), lambda i: (0, i))
    res = pl.BlockSpec((nblk, tb), lambda i: (0, 0))
    one = pl.BlockSpec((1, LANES), lambda i: (0, 0))
    return pl.pallas_call(
        functools.partial(_route_kernel, tb=tb, tm=tm, nblk=nblk, n_real=n_real),
        grid=(nblk,),
        in_specs=[pl.BlockSpec((tb, LANES), lambda i: (i, 0))],
        out_specs=[res, res, vec, vec, one, one],
        out_shape=[jax.ShapeDtypeStruct((nblk, tb), I32), jax.ShapeDtypeStruct((nblk, tb), I32),
                   jax.ShapeDtypeStruct((1, m), F32), jax.ShapeDtypeStruct((1, m), F32),
                   jax.ShapeDtypeStruct((1, LANES), I32), jax.ShapeDtypeStruct((1, LANES), I32)],
        scratch_shapes=[pltpu.VMEM((N_EXPERTS, LANES), F32)] + [pltpu.VMEM((nblk, tb), I32)] * 4,
        compiler_params=_params(("arbitrary",)),
        name="route",
    )(logits)


def _dispatch_kernel(p1_ref, p2_ref, x_ref, dst_in_ref, dst_ref, sem, *, tb):
    del dst_in_ref

    def start(r, c):
        src = x_ref.at[pl.ds(r, 1), :]
        pltpu.make_async_copy(src, dst_ref.at[pl.ds(p1_ref[0, r], 1), :], sem.at[0]).start()
        pltpu.make_async_copy(src, dst_ref.at[pl.ds(p2_ref[0, r], 1), :], sem.at[1]).start()
        return c

    lax.fori_loop(0, tb, start, 0, unroll=8)
    pltpu.make_async_copy(x_ref, dst_ref.at[pl.ds(0, tb), :], sem.at[0]).wait()
    pltpu.make_async_copy(x_ref, dst_ref.at[pl.ds(0, tb), :], sem.at[1]).wait()


def _dispatch(x, pos1, pos2, dst, tb):
    m = x.shape[0]
    smem = pl.BlockSpec((None, 1, tb), lambda i: (i, 0, 0), memory_space=pltpu.SMEM)
    return pl.pallas_call(
        functools.partial(_dispatch_kernel, tb=tb),
        grid=(m // tb,),
        in_specs=[smem, smem, pl.BlockSpec((tb, D_MODEL), lambda i: (i, 0)),
                  pl.BlockSpec(memory_space=pl.ANY)],
        out_specs=pl.BlockSpec(memory_space=pl.ANY),
        out_shape=jax.ShapeDtypeStruct(dst.shape, F32),
        scratch_shapes=[pltpu.SemaphoreType.DMA((2,))],
        input_output_aliases={3: 0},
        compiler_params=_params(("arbitrary",)),
        name="dispatch",
    )(pos1.reshape(m // tb, 1, tb), pos2.reshape(m // tb, 1, tb), x, dst)


def _combine_kernel(p1_ref, p2_ref, y_ref, x1_ref, g2_ref, w1_ref, w2_ref, o_ref, buf1, buf2, sem, *, tb):
    def start(r, c):
        pltpu.make_async_copy(y_ref.at[pl.ds(p1_ref[0, r], 1), :], buf1.at[pl.ds(r, 1), :], sem.at[0]).start()
        pltpu.make_async_copy(y_ref.at[pl.ds(p2_ref[0, r], 1), :], buf2.at[pl.ds(r, 1), :], sem.at[1]).start()
        return c

    lax.fori_loop(0, tb, start, 0, unroll=8)
    pltpu.make_async_copy(y_ref.at[pl.ds(0, tb), :], buf1, sem.at[0]).wait()
    pltpu.make_async_copy(y_ref.at[pl.ds(0, tb), :], buf2, sem.at[1]).wait()
    f = w1_ref[...] * buf1[...] + w2_ref[...] * buf2[...]
    o_ref[...] = x1_ref[...] + g2_ref[...] * f


def _combine(y, pos1, pos2, w1, w2, x1, layer, mod, tb):
    m = x1.shape[0]
    smem = pl.BlockSpec((None, 1, tb), lambda i: (i, 0, 0), memory_space=pltpu.SMEM)
    row = lambda w: pl.BlockSpec((tb, w), lambda i: (i, 0))
    return pl.pallas_call(
        functools.partial(_combine_kernel, tb=tb),
        grid=(m // tb,),
        in_specs=[smem, smem, pl.BlockSpec(memory_space=pl.ANY), row(D_MODEL),
                  mod.spec(layer, 5, tb), row(1), row(1)],
        out_specs=row(D_MODEL),
        out_shape=jax.ShapeDtypeStruct((m, D_MODEL), F32),
        scratch_shapes=[pltpu.VMEM((tb, D_MODEL), F32), pltpu.VMEM((tb, D_MODEL), F32),
                        pltpu.SemaphoreType.DMA((2,))],
        compiler_params=_params(("arbitrary",)),
        name="combine",
    )(pos1.reshape(m // tb, 1, tb), pos2.reshape(m // tb, 1, tb), y, x1, mod.arr,
      w1.reshape(m, 1), w2.reshape(m, 1))


def _moe_tiles(n_tokens, tm):
    return -(-(n_tokens * TOP_K) // tm) + N_EXPERTS


def _moe(groups, layer, wg, wu, wd, expert_base, sorted_buf, tm, tb):
    (h2_a, lg_a, x1_a, mod_a), (h2_b, lg_b, x1_b, mod_b) = groups
    ma, mb = h2_a.shape[0], h2_b.shape[0]
    assert ma % tb == 0 and mb <= tb and mb % LANES == 0
    n_tiles = _moe_tiles(ma + mb, tm)
    logits = jnp.concatenate([lg_a, lg_b, jnp.zeros((tb - mb, LANES), F32)], axis=0)
    pos1, pos2, w1, w2, te, nv = _route(logits, ma + mb, tb, tm)
    na = ma // tb
    hs = _dispatch(h2_a, pos1[:na], pos2[:na], sorted_buf, tb)
    hs = _dispatch(h2_b, pos1[na, :mb], pos2[na, :mb], hs, mb)
    y = _ffn(hs, wg, wu, wd, te[0, :n_tiles], nv[0, :1], tm, expert_base=expert_base)
    out_a = _combine(y, pos1[:na], pos2[:na], w1[0, :ma], w2[0, :ma], x1_a, layer, mod_a, tb)
    out_b = _combine(y, pos1[na, :mb], pos2[na, :mb], w1[0, ma:ma + mb], w2[0, ma:ma + mb], x1_b, layer, mod_b, mb)
    return out_a, out_b, hs


def _attn_decode_kernel(q_ref, ck_ref, cv_ref, kn_ref, vn_ref, sink_ref, o_ref, kw_ref, vw_ref):
    q = q_ref[...].astype(BF16)
    ck, cv = ck_ref[...], cv_ref[...]
    kn, vn = kn_ref[...], vn_ref[...]
    s = jnp.einsum('bhd,bwd->bhw', q, ck.astype(BF16), preferred_element_type=F32)
    knb = kn.astype(BF16).astype(F32)
    s_new = jnp.sum(q.astype(F32) * knb[:, None, :], axis=-1, keepdims=True)
    sink = sink_ref[...]
    m = jnp.maximum(jnp.maximum(jnp.max(s, axis=-1, keepdims=True), s_new), sink)
    p = jnp.exp(s - m)
    pn = jnp.exp(s_new - m)
    inv = 1.0 / (jnp.sum(p, axis=-1, keepdims=True) + pn + jnp.exp(sink - m))
    o = jnp.einsum('bhw,bwd->bhd', (p * inv).astype(BF16), cv.astype(BF16), preferred_element_type=F32)
    o_ref[...] = o + (pn * inv).astype(BF16).astype(F32) * vn.astype(BF16).astype(F32)[:, None, :]
    w = lax.broadcasted_iota(I32, ck.shape, 1)
    last = w == ck.shape[1] - 1
    kw_ref[...] = jnp.where(last, kn[:, None, :], pltpu.roll(ck, ck.shape[1] - 1, 1))
    vw_ref[...] = jnp.where(last, vn[:, None, :], pltpu.roll(cv, cv.shape[1] - 1, 1))


def _attention_decode(qkv, layer, ck_all, cv_all, sinks_col, tb):
    n = qkv.shape[0]
    win = ck_all.shape[2]
    qh = qkv[:, :ATTN_WIDTH].reshape(n, N_KV_HEADS, GQA_GROUP, HEAD_DIM)
    z = jnp.zeros_like(qh[:, 0])
    q_pl = jnp.concatenate([jnp.concatenate([qh[:, 0], z], axis=-1),
                            jnp.concatenate([z, qh[:, 1]], axis=-1)], axis=1)
    kn = qkv[:, ATTN_WIDTH:ATTN_WIDTH + KV_WIDTH].astype(F32)
    vn = qkv[:, ATTN_WIDTH + KV_WIDTH:].astype(F32)
    blk3 = lambda a, b: pl.BlockSpec((tb, a, b), lambda i: (i, 0, 0))
    cache = pl.BlockSpec((None, tb, win, KV_WIDTH), lambda i: (layer, i, 0, 0))
    blk2 = pl.BlockSpec((tb, KV_WIDTH), lambda i: (i, 0))
    o, ck_all, cv_all = pl.pallas_call(
        _attn_decode_kernel,
        grid=(n // tb,),
        in_specs=[blk3(N_Q_HEADS, LANES), cache, cache, blk2, blk2,
                  pl.BlockSpec((None, N_Q_HEADS, 1), lambda i: (layer, 0, 0))],
        out_specs=[blk3(N_Q_HEADS, LANES), cache, cache],
        out_shape=[jax.ShapeDtypeStruct((n, N_Q_HEADS, LANES), F32),
                   jax.ShapeDtypeStruct(ck_all.shape, F32),
                   jax.ShapeDtypeStruct(cv_all.shape, F32)],
        input_output_aliases={1: 1, 2: 2},
        compiler_params=_params(("arbitrary",)),
        name="attn_decode",
    )(q_pl, ck_all, cv_all, kn, vn, sinks_col)
    o = o.reshape(n, N_KV_HEADS, GQA_GROUP, N_KV_HEADS, HEAD_DIM)
    o = jnp.concatenate([o[:, 0, :, 0], o[:, 1, :, 1]], axis=1).reshape(n, ATTN_WIDTH)
    return o.astype(BF16), ck_all, cv_all


def _lru_decode_kernel(x_ref, gate_ref, buf_ref, h0_ref, cw_ref, cb_ref, wa_ref, ba_ref, wx_ref, bx_ref,
                       negsp_ref, o_ref, h_ref, nbuf_ref):
    x = x_ref[...]
    xc = x * cw_ref[CONV_K - 1:CONV_K, :] + cb_ref[...]
    for j in range(CONV_K - 1):
        xc = xc + buf_ref[:, j, :] * cw_ref[j:j + 1, :]
    a, u = _lru_gates(xc, wa_ref, ba_ref, wx_ref, bx_ref, negsp_ref)
    h = a * h0_ref[...] + u
    o_ref[...] = (h * _gelu_tanh(gate_ref[...].astype(F32))).astype(BF16)
    h_ref[...] = h
    for j in range(CONV_K - 2):
        nbuf_ref[:, j, :] = buf_ref[:, j + 1, :]
    nbuf_ref[:, CONV_K - 2, :] = x


def _lru_decode(lx, lgate, layer, buf_all, h0_all, lru_params):
    n = lx.shape[0]
    one = lambda shape: pl.BlockSpec(shape, lambda i: (0,) * len(shape))
    lay = lambda shape: pl.BlockSpec((None,) + shape, lambda i: (layer,) + (0,) * len(shape))
    nb = LRU_WIDTH // LANES
    return pl.pallas_call(
        _lru_decode_kernel,
        in_specs=[one((n, LRU_WIDTH)), one((n, LRU_WIDTH)), lay((n, CONV_K - 1, LRU_WIDTH)), lay((n, LRU_WIDTH)),
                  lay((CONV_K, LRU_WIDTH)), lay((1, LRU_WIDTH)), lay((nb, LANES, LANES)), lay((1, LRU_WIDTH)),
                  lay((nb, LANES, LANES)), lay((1, LRU_WIDTH)), lay((1, LRU_WIDTH))],
        out_shape=[jax.ShapeDtypeStruct((n, LRU_WIDTH), BF16),
                   jax.ShapeDtypeStruct((n, LRU_WIDTH), F32),
                   jax.ShapeDtypeStruct((n, CONV_K - 1, LRU_WIDTH), F32)],
        grid=(1,),
        compiler_params=_params(("arbitrary",)),
        name="lru_decode",
    )(lx, lgate, buf_all, h0_all, *lru_params)


def _ssd_pre_decode_kernel(xbc_ref, dt_ref, buf_ref, cw_ref, cb_ref, dtb_ref, aneg_ref, rexp_ref,
                           xs_ref, xdt_ref, dec_ref, bm_ref, cm_ref, nbuf_ref):
    xbc = xbc_ref[...]
    xc = xbc * cw_ref[CONV_K - 1:CONV_K, :] + cb_ref[...]
    for j in range(CONV_K - 1):
        xc = xc + buf_ref[:, j, :] * cw_ref[j:j + 1, :]
    xc = _silu(xc)
    xs = xc[:, :SSD_INNER]
    rexp = rexp_ref[...]
    dt = _softplus(dt_ref[...] + dtb_ref[...])
    xs_ref[...] = xs
    xdt_ref[...] = xs * _dot_exact_rhs(dt, rexp)
    dec_ref[...] = jnp.exp(_dot_exact_rhs(dt * aneg_ref[...], rexp))
    bm_ref[...] = xc[:, SSD_INNER:SSD_INNER + LANES]
    cm_ref[...] = xc[:, SSD_INNER + LANES:]
    for j in range(CONV_K - 2):
        nbuf_ref[:, j, :] = buf_ref[:, j + 1, :]
    nbuf_ref[:, CONV_K - 2, :] = xbc


def _ssd_state_decode_kernel(h0_ref, x_ref, dec_ref, bm_ref, cm_ref, rep_ref, tile_ref, sum_ref,
                             hn_ref, y_ref, *, tb):
    pn = SSD_HEAD_DIM * SSD_STATE
    x = x_ref[...].reshape(tb * SSD_HEADS, SSD_HEAD_DIM).astype(BF16)
    xrep = jnp.dot(x, rep_ref[...], preferred_element_type=F32).reshape(tb, SSD_HEADS, pn)
    bmb, cmb = bm_ref[...].astype(BF16), cm_ref[...].astype(BF16)
    first = lax.broadcasted_iota(I32, (tb, SSD_HEADS, pn), 1) < SSD_HEADS // SSD_GROUPS

    def tiled(v):
        t0 = jnp.dot(v, tile_ref[0], preferred_element_type=F32)
        t1 = jnp.dot(v, tile_ref[1], preferred_element_type=F32)
        return jnp.where(first, t0[:, None, :], t1[:, None, :])

    h_new = dec_ref[...][:, :, 0:1] * h0_ref[...] + xrep * tiled(bmb)
    hn_ref[...] = h_new
    hc = (h_new * tiled(cmb)).reshape(tb * SSD_HEADS, pn).astype(BF16)
    y_ref[...] = jnp.dot(hc, sum_ref[...], preferred_element_type=F32).reshape(tb, SSD_HEADS, SSD_HEAD_DIM)


def _ssd_post_decode_kernel(y_ref, xs_ref, z_ref, d_ref, gn_ref, o_ref):
    y = y_ref[...] + d_ref[...] * xs_ref[...]
    o_ref[...] = _ssd_gate_norm(y, z_ref[...].astype(F32), gn_ref[...]).astype(BF16)


def _ssd_decode_consts():
    pn = SSD_HEAD_DIM * SSD_STATE
    col_p = np.arange(pn) // SSD_STATE
    col_n = np.arange(pn) % SSD_STATE
    rep = (np.arange(SSD_HEAD_DIM)[:, None] == col_p[None, :]).astype(np.float32)
    row_g = np.arange(LANES) // SSD_STATE
    row_n = np.arange(LANES) % SSD_STATE
    tile = np.stack([((row_g[:, None] == g) & (row_n[:, None] == col_n[None, :])).astype(np.float32)
                     for g in range(SSD_GROUPS)])
    summ = rep.T
    return jnp.asarray(rep, BF16), jnp.asarray(tile, BF16), jnp.asarray(summ, BF16)


def _ssd_decode(xbc, dtr, z, layer, buf_all, h0_all, ssd_params, rexp, dec_consts, tb):
    n = xbc.shape[0]
    pn = SSD_HEAD_DIM * SSD_STATE
    cw_all, cb_all, dtb_all, aneg_all, d_all, gn_all = ssd_params
    f = lambda w: jax.ShapeDtypeStruct((n, w), F32)
    one = lambda shape: pl.BlockSpec(shape, lambda i: (0,) * len(shape))
    lay = lambda shape: pl.BlockSpec((None,) + shape, lambda i: (layer,) + (0,) * len(shape))
    xs, xdt, dec, bm, cm, nbuf = pl.pallas_call(
        _ssd_pre_decode_kernel,
        in_specs=[one((n, SSD_CONV_DIM)), one((n, LANES)), lay((n, CONV_K - 1, SSD_CONV_DIM)),
                  lay((CONV_K, SSD_CONV_DIM)), lay((1, SSD_CONV_DIM)), lay((1, LANES)), lay((1, LANES)),
                  one((LANES, SSD_INNER))],
        out_shape=[f(SSD_INNER), f(SSD_INNER), f(SSD_INNER), f(LANES), f(LANES),
                   jax.ShapeDtypeStruct((n, CONV_K - 1, SSD_CONV_DIM), F32)],
        grid=(1,),
        compiler_params=_params(("arbitrary",)),
        name="ssd_pre_decode",
    )(xbc, dtr, buf_all, cw_all, cb_all, dtb_all, aneg_all, rexp)
    rep, tile, summ = dec_consts
    b3 = lambda a, b: pl.BlockSpec((tb, a, b), lambda i: (i, 0, 0))
    b2 = pl.BlockSpec((tb, LANES), lambda i: (i, 0))
    full = lambda shape: pl.BlockSpec(shape, lambda i: (0,) * len(shape))
    state = pl.BlockSpec((None, tb, SSD_HEADS, pn), lambda i: (layer, i, 0, 0))
    h0_all, y = pl.pallas_call(
        functools.partial(_ssd_state_decode_kernel, tb=tb),
        grid=(n // tb,),
        in_specs=[state, b3(SSD_HEADS, SSD_HEAD_DIM), b3(SSD_HEADS, SSD_HEAD_DIM), b2, b2,
                  full((SSD_HEAD_DIM, pn)), full((SSD_GROUPS, LANES, pn)), full((pn, SSD_HEAD_DIM))],
        out_specs=[state, b3(SSD_HEADS, SSD_HEAD_DIM)],
        out_shape=[jax.ShapeDtypeStruct(h0_all.shape, F32),
                   jax.ShapeDtypeStruct((n, SSD_HEADS, SSD_HEAD_DIM), F32)],
        input_output_aliases={0: 0},
        compiler_params=_params(("arbitrary",)),
        name="ssd_state_decode",
    )(h0_all, xdt.reshape(n, SSD_HEADS, SSD_HEAD_DIM), dec.reshape(n, SSD_HEADS, SSD_HEAD_DIM), bm, cm,
      rep, tile, summ)
    o = pl.pallas_call(
        _ssd_post_decode_kernel,
        in_specs=[one((n, SSD_INNER)), one((n, SSD_INNER)), one((n, SSD_INNER)),
                  lay((1, SSD_INNER)), lay((1, SSD_INNER))],
        out_shape=jax.ShapeDtypeStruct((n, SSD_INNER), BF16),
        grid=(1,),
        compiler_params=_params(("arbitrary",)),
        name="ssd_post_decode",
    )(y.reshape(n, SSD_INNER), xs, z, d_all, gn_all)
    return o, h0_all, nbuf


def _pack_w_in(w):
    dt0 = C_GT
    dt = jnp.pad(w[..., dt0:dt0 + SSD_HEADS].astype(BF16), ((0, 0), (0, 0), (0, LANES - SSD_HEADS)))
    return jnp.concatenate([w[..., :dt0].astype(BF16), w[..., dt0 + SSD_HEADS:].astype(BF16), dt], axis=-1)


def kernel(x_prompt, x_sample, c_prompt, c_sample, cache_k, cache_v, state_lru_h, state_lru_conv, state_ssd_h, state_ssd_conv, w_mod, b_mod, g_norm_mix, g_norm_ffn, w_in, g_q, g_k, attn_sinks, lru_conv_w, lru_conv_b, lru_w_a, lru_b_a, lru_w_x, lru_b_x, lru_lambda, ssd_conv_w, ssd_conv_b, ssd_dt_bias, ssd_a_log, ssd_d, g_ssd_norm, w_br_attn, w_br_lru, w_br_ssd, w_out, ffn_w_gate, ffn_w_up, ffn_w_down, moe_w_router, moe_b_router, moe_w_gate, moe_w_up, moe_w_down):
    bp, seq = x_prompt.shape[:2]
    ns = x_sample.shape[0]
    win = cache_k.shape[2]
    assert x_sample.shape[1] == 1
    assert win == WINDOW and PAST_LEN >= WINDOW
    mp = bp * seq
    tm_p, tm_s = TOKEN_TILE, ns
    pn = SSD_HEAD_DIM * SSD_STATE

    mod = _modulation(jnp.concatenate([c_sample, c_prompt], axis=0), w_mod, b_mod)
    mod_s = _Mod(mod, True, ns, 1)
    mod_p = _Mod(mod[:, ns:].reshape(DEPTH * bp * 6, 1, D_MODEL), False, bp, seq)
    row3 = lambda v: v.astype(F32)[:, None, :]
    w_packed = _pack_w_in(w_in)
    g_mix, g_ffn = row3(g_norm_mix), row3(g_norm_ffn)
    gq2, gk2 = row3(jnp.tile(g_q, (1, 2))), row3(jnp.tile(g_k, (1, 2)))
    sinks = attn_sinks.astype(F32)
    sinks_col = sinks[:, :, None]
    e2 = _head_avg_matrix()
    tab_p = _rope_tables(jnp.arange(seq))
    tab_s = _rope_tables(jnp.full((1,), PAST_LEN))
    wa, wx, negsp = _lru_weight_stack(lru_w_a, lru_w_x, lru_lambda)
    lru_params = (lru_conv_w.astype(F32), row3(lru_conv_b), wa, row3(lru_b_a), wx, row3(lru_b_x), negsp)
    ssd_params = _ssd_param_stack(ssd_conv_w, ssd_conv_b, ssd_dt_bias, ssd_a_log, ssd_d, g_ssd_norm)
    rexp = _head_expand_matrix()
    dec_consts = _ssd_decode_consts()
    br = (w_br_attn.astype(BF16), w_br_lru.astype(BF16), w_br_ssd.astype(BF16), w_out.astype(BF16))
    wr_all, br_all = _router_stack(moe_w_router, moe_b_router)
    n_moe = moe_w_gate.shape[0]
    flat = lambda w: w.astype(BF16).reshape((n_moe * N_EXPERTS,) + w.shape[2:])
    moe_w = (flat(moe_w_gate), flat(moe_w_up), flat(moe_w_down))
    ffn_w = (ffn_w_gate.astype(BF16), ffn_w_up.astype(BF16), ffn_w_down.astype(BF16))
    ck_all = cache_k.reshape(DEPTH, ns, win, KV_WIDTH)
    cv_all = cache_v.reshape(DEPTH, ns, win, KV_WIDTH)
    ssd_h0_all = state_ssd_h.reshape(DEPTH, ns, SSD_HEADS, pn)
    sorted_buf = jnp.zeros((_moe_tiles(mp + ns, tm_p) * tm_p, D_MODEL), F32)

    xp = x_prompt.reshape(mp, D_MODEL)
    xs = x_sample.reshape(ns, D_MODEL)
    outs_p = [[] for _ in range(6)]
    outs_s = [[] for _ in range(3)]
    attn_bias = _attn_bias()

    for l in range(DEPTH):
        is_moe = l % 2 == 1
        j = l // 2
        router = (wr_all, br_all, j) if is_moe else None

        qkv, gt, o_lru, lru_h, lru_buf, o_ssd, ssd_st, ssd_buf = _in_proj_mix(
            xp, l, g_mix, mod_p, w_packed, lru_params, ssd_params, rexp, tm_p, bp, seq)
        o_attn, k_win, v_win = _attention_prompt(qkv, l, sinks, attn_bias, tab_p, gq2, gk2, e2, bp, seq)
        merged_p = _merge(o_attn, o_lru, o_ssd, gt, xp, l, mod_p, g_ffn, *br, router, tm_p)
        for acc, t in zip(outs_p, (k_win, v_win, lru_h, lru_buf, ssd_st, ssd_buf)):
            acc.append(t)

        qkv, lx, lg, z, xbc, gt, dtr = _in_proj(xs, l, g_mix, mod_s, w_packed, tm_s, (tab_s, gq2, gk2, e2))
        o_attn, ck_all, cv_all = _attention_decode(qkv, l, ck_all, cv_all, sinks_col, DECODE_ATTN_BLOCK)
        o_lru, lru_h, lru_buf = _lru_decode(lx, lg, l, state_lru_conv, state_lru_h, lru_params)
        o_ssd, ssd_h0_all, ssd_buf = _ssd_decode(xbc, dtr, z, l, state_ssd_conv, ssd_h0_all, ssd_params, rexp,
                                                 dec_consts, DECODE_STATE_BLOCK)
        merged_s = _merge(o_attn, o_lru, o_ssd, gt, xs, l, mod_s, g_ffn, *br, router, tm_s)
        for acc, t in zip(outs_s, (lru_h, lru_buf, ssd_buf)):
            acc.append(t)

        if is_moe:
            (x1_p, h2_p, lg_p), (x1_s, h2_s, lg_s) = merged_p, merged_s
            xp, xs, sorted_buf = _moe(((h2_p, lg_p, x1_p, mod_p), (h2_s, lg_s, x1_s, mod_s)), l, *moe_w,
                                      j * N_EXPERTS, sorted_buf, tm_p, ROUTE_BLOCK)
        else:
            (x1_p, h2_p), (x1_s, h2_s) = merged_p, merged_s
            nt = mp // tm_p
            xp = _ffn(h2_p, *ffn_w, jnp.zeros((nt,), I32), jnp.full((1,), nt, I32), tm_p,
                      expert_base=j, residual=x1_p, mod=mod_p, layer=l)
            xs = _ffn(h2_s, *ffn_w, jnp.zeros((1,), I32), jnp.ones((1,), I32), tm_s,
                      expert_base=j, residual=x1_s, mod=mod_s, layer=l)

    st = [jnp.stack(a) for a in outs_p]
    ss = [jnp.stack(a) for a in outs_s]
    kv5 = lambda t, b: t.reshape(DEPTH, b, WINDOW, N_KV_HEADS, HEAD_DIM)
    return (xp.reshape(bp, seq, D_MODEL), xs.reshape(ns, 1, D_MODEL),
            kv5(st[0], bp), kv5(st[1], bp), st[2].reshape(DEPTH, bp, LRU_WIDTH), st[3],
            _ssd_state_to_hpn(st[4]), st[5],
            kv5(ck_all, ns), kv5(cv_all, ns), ss[0], ss[1],
            ssd_h0_all.reshape(DEPTH, ns, SSD_HEADS, SSD_HEAD_DIM, SSD_STATE), ss[2])
```
